```python
import math
import jax
import jax.numpy as jnp
from jax import lax
import numpy as np

D_MODEL = 1024
BATCH = 16
SEQ = 4096
DEPTH = 1

RMS_EPS = 1e-5
HG_EXPAND = 128
HG_HEADS = D_MODEL // HG_EXPAND
HG_DK = HG_EXPAND
HG_DV = D_MODEL // HG_HEADS
HG_FDIM = HG_HEADS * HG_DK
HG_VW = HG_HEADS * HG_DV
HG_CHUNK = 64
SWA_HEADS = 16
SWA_KV_HEADS = 4
SWA_GROUP = SWA_HEADS // SWA_KV_HEADS
SWA_HEAD_DIM = 64
SWA_QW = SWA_HEADS * SWA_HEAD_DIM
SWA_KW = SWA_KV_HEADS * SWA_HEAD_DIM
SWA_WINDOW = 128
SWA_BLOCK = 128
T5_BUCKETS = 32
T5_MAX_EXACT = 16
T5_MAX_DIST = 128
MEM_LEN = 256
XA_HEADS = 4
XA_HEAD_DIM = D_MODEL // XA_HEADS
N_EXPERTS = 32
TOP_K = 4
D_FF = D_MODEL
SWIGLU_ALPHA = 1.702
SWIGLU_LIMIT = 7.0
MOE_BLOCK = 128
IN_WIDTHS = (HG_FDIM, HG_FDIM, HG_VW, HG_VW, SWA_QW, SWA_KW, SWA_KW, D_MODEL, D_MODEL)
IN_COLS = sum(IN_WIDTHS)

kernel_name = 'hybrid_hgrn2_swa_sink_moe_block'


def rms_norm(x, gain):
    xf = x.astype(jnp.float32)
    y = xf * lax.rsqrt(jnp.mean(xf * xf, axis=-1, keepdims=True) + RMS_EPS)
    return (y * gain.astype(jnp.float32)).astype(x.dtype)


def split_points():
    pts, acc = [], 0
    for w in IN_WIDTHS[:-1]:
        acc += w
        pts.append(acc)
    return pts


def hgrn2_mixer(q, f_logit, i, g, lb, norm_gain):
    B, S, _ = q.shape
    nc = S // HG_CHUNK
    f32 = jnp.float32
    ff = f_logit.astype(f32)
    lbf = lb.astype(f32)
    log_f = jnp.logaddexp(jnp.log(lbf), jnp.log1p(-lbf) + jax.nn.log_sigmoid(ff))
    k = (1.0 - lbf) * jax.nn.sigmoid(-ff)
    qf = jax.nn.silu(q.astype(f32))

    def to_chunks(t, d):
        return t.reshape(B, nc, HG_CHUNK, HG_HEADS, d).transpose(1, 0, 3, 2, 4)

    xs = (to_chunks(qf, HG_DK), to_chunks(k, HG_DK), to_chunks(i.astype(f32), HG_DV), to_chunks(log_f, HG_DK))
    causal = jnp.tril(jnp.ones((HG_CHUNK, HG_CHUNK), dtype=bool))

    def step(state, inp):
        qb, kb, vb, gb = inp
        G = jnp.cumsum(gb, axis=2)
        o_inter = jnp.einsum('bhtk,bhkv->bhtv', qb * jnp.exp(G), state)
        diff = G[:, :, :, None, :] - G[:, :, None, :, :]
        decay = jnp.exp(jnp.where(causal[:, :, None], diff, -jnp.inf))
        attn = jnp.einsum('bhtk,bhsk,bhtsk->bhts', qb, kb, decay)
        o_intra = jnp.einsum('bhts,bhsv->bhtv', attn, vb)
        G_last = G[:, :, -1:, :]
        k_dec = kb * jnp.exp(G_last - G)
        new_state = jnp.exp(G_last[:, :, 0, :])[..., None] * state + jnp.einsum('bhsk,bhsv->bhkv', k_dec, vb)
        return new_state, o_inter + o_intra

    state0 = jnp.zeros((B, HG_HEADS, HG_DK, HG_DV), f32)
    _, o = lax.scan(step, state0, xs)
    o = o.transpose(1, 0, 3, 2, 4).reshape(B, S, HG_HEADS, HG_DV)
    o = rms_norm(o, norm_gain) * jax.nn.silu(g.astype(f32).reshape(B, S, HG_HEADS, HG_DV))
    return o.reshape(B, S, HG_VW).astype(i.dtype)


def t5_bucket(dist):
    n = jnp.maximum(dist, 0)
    nf = jnp.maximum(n, 1).astype(jnp.float32)
    large = T5_MAX_EXACT + (jnp.log(nf / T5_MAX_EXACT) / math.log(T5_MAX_DIST / T5_MAX_EXACT)
                            * (T5_BUCKETS - T5_MAX_EXACT)).astype(jnp.int32)
    large = jnp.minimum(large, T5_BUCKETS - 1)
    return jnp.where(n < T5_MAX_EXACT, n, large)


def swa_mixer(q, k, v, sinks, t5_table):
    B, S, _ = q.shape
    nb = S // SWA_BLOCK
    qb = q.reshape(B, nb, SWA_BLOCK, SWA_KV_HEADS, SWA_GROUP, SWA_HEAD_DIM)
    kr = k.reshape(B, nb, SWA_BLOCK, SWA_KV_HEADS, SWA_HEAD_DIM)
    vr = v.reshape(B, nb, SWA_BLOCK, SWA_KV_HEADS, SWA_HEAD_DIM)

    def with_prev(t):
        prev = jnp.pad(t[:, :-1], ((0, 0), (1, 0), (0, 0), (0, 0), (0, 0)))
        return jnp.concatenate([prev, t], axis=2)

    kb, vb = with_prev(kr), with_prev(vr)
    t_loc = jnp.arange(SWA_BLOCK, dtype=jnp.int32)[:, None]
    s_loc = jnp.arange(2 * SWA_BLOCK, dtype=jnp.int32)[None, :]
    dist = t_loc + SWA_BLOCK - s_loc
    in_band = (dist >= 0) & (dist < SWA_WINDOW)
    key_exists = (jnp.arange(nb)[:, None] > 0) | (s_loc >= SWA_BLOCK)
    mask = in_band[None] & key_exists[:, None, :]
    bias = t5_table.astype(jnp.float32)[t5_bucket(dist)]
    bias = bias.transpose(2, 0, 1).reshape(SWA_KV_HEADS, SWA_GROUP, 1, SWA_BLOCK, 2 * SWA_BLOCK)
    scores = jnp.einsum('bnqhgd,bnkhd->bhgnqk', qb, kb).astype(jnp.float32) * (SWA_HEAD_DIM ** -0.5) + bias
    scores = jnp.where(mask, scores, -jnp.inf)
    sink = jnp.broadcast_to(sinks.astype(jnp.float32).reshape(SWA_KV_HEADS, SWA_GROUP, 1, 1, 1),
                            scores.shape[:-1] + (1,))
    probs = jax.nn.softmax(jnp.concatenate([scores, sink], axis=-1), axis=-1)[..., :-1]
    out = jnp.einsum('bhgnqk,bnkhd->bnqhgd', probs.astype(v.dtype), vb)
    return out.reshape(B, S, SWA_QW)


def memory_cross_attention(h, mem, w_q, w_k, w_v, w_o):
    B, S, _ = h.shape
    M = mem.shape[1]
    q = (h @ w_q).reshape(B, S, XA_HEADS, XA_HEAD_DIM)
    k = (mem @ w_k).reshape(B, M, XA_HEADS, XA_HEAD_DIM)
    v = (mem @ w_v).reshape(B, M, XA_HEADS, XA_HEAD_DIM)
    s = jnp.einsum('bqhd,bkhd->bhqk', q, k).astype(jnp.float32) * (XA_HEAD_DIM ** -0.5)
    p = jax.nn.softmax(s, axis=-1).astype(h.dtype)
    o = jnp.einsum('bhqk,bkhd->bqhd', p, v).reshape(B, S, XA_HEADS * XA_HEAD_DIM)
    return o @ w_o


def moe_ffn(h, w_router, b_router, w_gate_up, b_gate_up, w_down, b_down):
    B, S, D = h.shape
    xt = h.reshape(-1, D)
    n = xt.shape[0]
    logits = (xt @ w_router + b_router).astype(jnp.float32)
    top_val, top_idx = lax.top_k(logits, TOP_K)
    gates = jax.nn.softmax(top_val, axis=-1)
    n_assign = n * TOP_K
    e_flat = top_idx.reshape(-1).astype(jnp.int32)
    tok_flat = jnp.repeat(jnp.arange(n, dtype=jnp.int32), TOP_K)
    g_flat = gates.reshape(-1)
    order = jnp.argsort(e_flat)
    e_sorted = e_flat[order]
    counts = jnp.zeros((N_EXPERTS,), jnp.int32).at[e_flat].add(1)
    starts = jnp.cumsum(counts) - counts
    padded = (counts + MOE_BLOCK - 1) // MOE_BLOCK * MOE_BLOCK
    pad_ends = jnp.cumsum(padded)
    pad_starts = pad_ends - padded
    rank = jnp.arange(n_assign, dtype=jnp.int32) - starts[e_sorted]
    dest = pad_starts[e_sorted] + rank
    cap = -(-n_assign // MOE_BLOCK) * MOE_BLOCK + N_EXPERTS * MOE_BLOCK
    n_blocks = cap // MOE_BLOCK
    tok_buf = jnp.zeros((cap,), jnp.int32).at[dest].set(tok_flat[order])
    gate_buf = jnp.zeros((cap,), jnp.float32).at[dest].set(g_flat[order])
    block_start = jnp.arange(n_blocks, dtype=jnp.int32) * MOE_BLOCK
    block_expert = jnp.minimum(jnp.searchsorted(pad_ends, block_start, side='right'), N_EXPERTS - 1)

    def expert_block(args):
        toks, gts, e = args
        xb = xt[toks]
        gu = xb @ w_gate_up[e] + b_gate_up[e]
        x_glu, x_lin = jnp.split(gu, 2, axis=-1)
        x_glu = jnp.minimum(x_glu, SWIGLU_LIMIT)
        x_lin = jnp.clip(x_lin, -SWIGLU_LIMIT, SWIGLU_LIMIT)
        act = x_glu * jax.nn.sigmoid(SWIGLU_ALPHA * x_glu) * (x_lin + 1.0)
        y = act @ w_down[e] + b_down[e]
        return (y.astype(jnp.float32) * gts[:, None])

    y_buf = lax.map(expert_block, (tok_buf.reshape(n_blocks, MOE_BLOCK),
                                   gate_buf.reshape(n_blocks, MOE_BLOCK), block_expert))
    out = jnp.zeros((n, D), jnp.float32).at[tok_buf].add(y_buf.reshape(cap, D))
    return out.astype(h.dtype).reshape(B, S, D)


def setup_inputs(seed: int = 0) -> dict:
    key = jax.random.key(seed)
    ks = jax.random.split(key, 26)
    L, D, E = DEPTH, D_MODEL, N_EXPERTS

    def nrm(k, shape, scale):
        return scale * jax.random.normal(k, shape, jnp.float32)

    return {
        'x': nrm(ks[0], (BATCH, SEQ, D), 1.0),
        'mem': nrm(ks[1], (BATCH, MEM_LEN, D), 1.0),
        't5_table': nrm(ks[2], (T5_BUCKETS, SWA_HEADS), 0.5),
        'lb_logits': nrm(ks[3], (DEPTH + 1, HG_FDIM), 0.5),
        'norm_mix': 1.0 + nrm(ks[4], (L, D), 0.1),
        'w_in': nrm(ks[5], (L, D, IN_COLS), D ** -0.5),
        'b_gates': nrm(ks[6], (L, 2, D), 0.1),
        'hg_norm_gain': 1.0 + nrm(ks[7], (L, HG_DV), 0.1),
        'swa_sinks': nrm(ks[8], (L, SWA_HEADS), 0.5),
        'w_branch_a': nrm(ks[9], (L, HG_VW, D), HG_VW ** -0.5),
        'w_branch_b': nrm(ks[10], (L, SWA_QW, D), SWA_QW ** -0.5),
        'w_mix_out': nrm(ks[11], (L, D, D), D ** -0.5),
        'norm_xa': 1.0 + nrm(ks[12], (L, D), 0.1),
        'xa_w_q': nrm(ks[13], (L, D, XA_HEADS * XA_HEAD_DIM), D ** -0.5),
        'xa_w_k': nrm(ks[14], (L, D, XA_HEADS * XA_HEAD_DIM), D ** -0.5),
        'xa_w_v': nrm(ks[15], (L, D, XA_HEADS * XA_HEAD_DIM), D ** -0.5),
        'xa_w_o': nrm(ks[16], (L, XA_HEADS * XA_HEAD_DIM, D), (XA_HEADS * XA_HEAD_DIM) ** -0.5),
        'norm_ffn': 1.0 + nrm(ks[17], (L, D), 0.1),
        'w_router': nrm(ks[18], (L, D, E), D ** -0.5),
        'b_router': nrm(ks[19], (L, E), 0.01),
        'w_gate_up': nrm(ks[20], (L, E, D, 2 * D_FF), D ** -0.5),
        'b_gate_up': nrm(ks[21], (L, E, 2 * D_FF), 0.02),
        'w_down': nrm(ks[22], (L, E, D_FF, D), D_FF ** -0.5),
        'b_down': nrm(ks[23], (L, E, D), 0.02),
        'norm_final': 1.0 + nrm(ks[24], (D,), 0.1),
    }


def reference(x, mem, t5_table, lb_logits, norm_mix, w_in, b_gates, hg_norm_gain, swa_sinks,
              w_branch_a, w_branch_b, w_mix_out, norm_xa, xa_w_q, xa_w_k, xa_w_v, xa_w_o,
              norm_ffn, w_router, b_router, w_gate_up, b_gate_up, w_down, b_down, norm_final):
    lower_bounds = jnp.cumsum(jax.nn.softmax(lb_logits.astype(jnp.float32), axis=0), axis=0)
    pts = split_points()
    for l in range(DEPTH):
        h = rms_norm(x, norm_mix[l])
        proj = h @ w_in[l]
        hq, hf, hi, hg, sq, sk, sv, ga, gb = jnp.split(proj, pts, axis=-1)
        o_a = hgrn2_mixer(hq, hf, hi, hg, lower_bounds[l], hg_norm_gain[l])
        o_b = swa_mixer(sq, sk, sv, swa_sinks[l], t5_table)
        gate_a = jax.nn.sigmoid(ga + b_gates[l, 0])
        gate_b = jax.nn.sigmoid(gb + b_gates[l, 1])
        merged = gate_a * (o_a @ w_branch_a[l]) + gate_b * (o_b @ w_branch_b[l])
        x = x + merged @ w_mix_out[l]
        x = x + memory_cross_attention(rms_norm(x, norm_xa[l]), mem,
                                       xa_w_q[l], xa_w_k[l], xa_w_v[l], xa_w_o[l])
        x = x + moe_ffn(rms_norm(x, norm_ffn[l]), w_router[l], b_router[l],
                        w_gate_up[l], b_gate_up[l], w_down[l], b_down[l])
    return rms_norm(x, norm_final)
```

```python
import functools
import math

import jax
import jax.numpy as jnp
import numpy as np
from jax import lax
from jax.experimental import pallas as pl
from jax.experimental.pallas import tpu as pltpu

F32 = jnp.float32
BF16 = jnp.bfloat16

RMS_EPS = 1e-5
D_MODEL = 1024
HG_HEADS = 8
HG_DK = 128
HG_DV = 128
HG_CHUNK = 64
HG_LEVELS = (32, 16)
HG_DIAG = 16
SWA_HEADS = 16
SWA_KV_HEADS = 4
SWA_GROUP = 4
SWA_HEAD_DIM = 64
SWA_WINDOW = 128
SWA_BLOCK = 128
T5_BUCKETS = 32
T5_MAX_EXACT = 16
T5_MAX_DIST = 128
XA_HEADS = 4
XA_HEAD_DIM = 256
N_EXPERTS = 32
TOP_K = 4
D_FF = 1024
SWIGLU_ALPHA = 1.702
SWIGLU_LIMIT = 7.0
MOE_BLOCK = 128
NEG_BIG = -1e30

VMEM_LIMIT = 56 * 1024 * 1024


def _cparams(sem):
    return pltpu.CompilerParams(dimension_semantics=sem, vmem_limit_bytes=VMEM_LIMIT)


def _rms(xf, gain):
    return xf * lax.rsqrt(jnp.mean(xf * xf, axis=-1, keepdims=True) + RMS_EPS) * gain


def _dot(a, b):
    return jnp.dot(a, b, preferred_element_type=F32)


def _dot_nt(a, b):
    return lax.dot_general(a, b, (((1,), (1,)), ((), ())), preferred_element_type=F32)


def _dot_tn(a, b):
    return lax.dot_general(a, b, (((0,), (0,)), ((), ())), preferred_element_type=F32)


def _norm_matmul_kernel(x_ref, g_ref, w_ref, o_ref, h_ref, *, use_norm):
    @pl.when(pl.program_id(1) == 0)
    def _():
        xf = x_ref[...].astype(F32)
        if use_norm:
            xf = _rms(xf, g_ref[...])
        h_ref[...] = xf.astype(BF16)

    o_ref[...] = _dot(h_ref[...], w_ref[...]).astype(o_ref.dtype)


def _norm_matmul(x, gain, w, *, use_norm, tm, tn, out_dtype):
    m, k = x.shape
    n = w.shape[1]
    tm = min(tm, m)
    tn = min(tn, n)
    return pl.pallas_call(
        functools.partial(_norm_matmul_kernel, use_norm=use_norm),
        grid=(m // tm, n // tn),
        in_specs=[
            pl.BlockSpec((tm, k), lambda i, j: (i, 0)),
            pl.BlockSpec((1, k), lambda i, j: (0, 0)),
            pl.BlockSpec((k, tn), lambda i, j: (0, j)),
        ],
        out_specs=pl.BlockSpec((tm, tn), lambda i, j: (i, j)),
        out_shape=jax.ShapeDtypeStruct((m, n), out_dtype),
        scratch_shapes=[pltpu.VMEM((tm, k), BF16)],
        compiler_params=_cparams(("parallel", "arbitrary")),
        name="norm_matmul",
    )(x, gain, w)


def _group_row(x, group, row):
    t, d = x.shape
    x3 = x.reshape(t // group, group, d)
    return jnp.broadcast_to(x3[:, row:row + 1, :], (t // group, group, d)).reshape(t, d)


def _hgrn_kernel(q_ref, f_ref, i_ref, g_ref, lb_ref, gain_ref, o_ref,
                 st_ref, qf_ref, kk_ref, gc_ref, v_ref, a_ref, lq_ref, lk_ref, *, t_rows):
    c = HG_CHUNK
    nchunk = t_rows // c

    @pl.when(pl.program_id(2) == 0)
    def _():
        st_ref[...] = jnp.zeros_like(st_ref)

    lb = lb_ref[...]
    ff = f_ref[...].astype(F32)
    e = jnp.exp(-jnp.abs(ff))
    r = 1.0 / (1.0 + e)
    sig_pos = jnp.where(ff >= 0, r, e * r)
    sig_neg = jnp.where(ff >= 0, e * r, r)
    logf = jnp.log(lb + (1.0 - lb) * sig_pos)
    kk = (1.0 - lb) * sig_neg
    qv = q_ref[...].astype(F32)
    qf = qv * (1.0 / (1.0 + jnp.exp(-qv)))

    row = lax.broadcasted_iota(jnp.int32, (t_rows, 1), 0)
    rin = row % c
    gcum = logf
    sh = 1
    while sh < c:
        gcum = gcum + jnp.where(rin >= sh, pltpu.roll(gcum, sh, 0), 0.0)
        sh *= 2

    qf_ref[...] = qf
    kk_ref[...] = kk
    gc_ref[...] = gcum
    v_ref[...] = i_ref[...].astype(F32)

    col = lax.broadcasted_iota(jnp.int32, (t_rows, c), 1)
    d = HG_DIAG
    rin_d = row % d
    base_d = (rin // d) * d
    a_diag = jnp.zeros((t_rows, c), F32)
    for j in range(d):
        gj = _group_row(gcum, d, j)
        kj = _group_row(kk, d, j)
        dec = jnp.exp(jnp.minimum(gcum - gj, 0.0))
        p = jnp.where(rin_d >= j, qf * dec * kj, 0.0)
        colv = jnp.sum(p, axis=-1, keepdims=True)
        a_diag = jnp.where(col == base_d + j, colv, a_diag)
    a_ref[...] = a_diag

    for li, b in enumerate(HG_LEVELS):
        gref = _group_row(gcum, 2 * b, b - 1)
        upper = (rin % (2 * b)) >= b
        qa = jnp.where(upper, qf * jnp.exp(jnp.minimum(gcum - gref, 0.0)), 0.0)
        kb = jnp.where(upper, 0.0, kk * jnp.exp(jnp.minimum(gref - gcum, 0.0)))
        lq_ref[li] = qa.astype(BF16)
        lk_ref[li] = kb.astype(BF16)

    gain = gain_ref[...]
    ri = lax.broadcasted_iota(jnp.int32, (c, c), 0)
    ci = lax.broadcasted_iota(jnp.int32, (c, c), 1)
    lvl_mask = []
    for b in HG_LEVELS:
        lvl_mask.append(((ri // b - ci // b) * 2 + (ci // b) % 2) == 2)

    for ch in range(nchunk):
        sl = slice(ch * c, (ch + 1) * c)
        g_c = gc_ref[sl, :]
        g_last = gc_ref[(ch + 1) * c - 1:(ch + 1) * c, :]
        a = a_ref[sl, :]
        for li in range(len(HG_LEVELS)):
            s = _dot_nt(lq_ref[li, sl, :], lk_ref[li, sl, :])
            a = a + jnp.where(lvl_mask[li], s, 0.0)
        v_c = v_ref[sl, :].astype(BF16)
        st = st_ref[...]
        q_dec = (qf_ref[sl, :] * jnp.exp(g_c)).astype(BF16)
        o = _dot_nt(q_dec, st.astype(BF16)) + _dot(a.astype(BF16), v_c)
        k_dec = (kk_ref[sl, :] * jnp.exp(g_last - g_c)).astype(BF16)
        st_ref[...] = st * jnp.exp(g_last) + _dot_tn(v_c, k_dec)
        on = _rms(o, gain)
        gv = g_ref[sl, :].astype(F32)
        o_ref[sl, :] = (on * gv * (1.0 / (1.0 + jnp.exp(-gv)))).astype(o_ref.dtype)


def _hgrn(proj, lb, gain, *, batch, seq, t_rows):
    m = batch * seq
    t_rows = min(t_rows, seq)
    nt = seq // t_rows
    h = HG_HEADS

    def spec(off):
        return pl.BlockSpec((t_rows, 128), lambda b, hh, t: (b * nt + t, off + hh))

    return pl.pallas_call(
        functools.partial(_hgrn_kernel, t_rows=t_rows),
        grid=(batch, h, nt),
        in_specs=[spec(0), spec(h), spec(2 * h), spec(3 * h),
                  pl.BlockSpec((1, 128), lambda b, hh, t: (0, hh)),
                  pl.BlockSpec((1, 128), lambda b, hh, t: (0, 0))],
        out_specs=pl.BlockSpec((t_rows, 128), lambda b, hh, t: (b * nt + t, hh)),
        out_shape=jax.ShapeDtypeStruct((m, h * HG_DV), BF16),
        scratch_shapes=[pltpu.VMEM((HG_DV, HG_DK), F32)]
        + [pltpu.VMEM((t_rows, 128), F32) for _ in range(4)]
        + [pltpu.VMEM((t_rows, HG_CHUNK), F32)]
        + [pltpu.VMEM((len(HG_LEVELS), t_rows, 128), BF16) for _ in range(2)],
        compiler_params=_cparams(("parallel", "parallel", "arbitrary")),
        name="hgrn2",
    )(proj, proj, proj, proj, lb, gain)


def _swa_kernel(sink_ref, q_ref, kp_ref, kc_ref, vp_ref, vc_ref, bias_ref, o_ref):
    blk = SWA_BLOCK
    dh = SWA_HEAD_DIM
    first = pl.program_id(1) == 0
    colk = lax.broadcasted_iota(jnp.int32, (blk, 2 * blk), 1)
    no_prev = first & (colk < blk)
    scale = dh ** -0.5
    outs = []
    for h in range(SWA_KV_HEADS):
        ks = slice(h * dh, (h + 1) * dh)
        kcat = jnp.concatenate([kp_ref[:, ks], kc_ref[:, ks]], axis=0)
        vcat = jnp.concatenate([vp_ref[:, ks], vc_ref[:, ks]], axis=0)
        for g in range(SWA_GROUP):
            hq = h * SWA_GROUP + g
            q = q_ref[:, hq * dh:(hq + 1) * dh]
            s = _dot_nt(q, kcat) * scale + bias_ref[hq]
            s = jnp.where(no_prev, NEG_BIG, s)
            sink = sink_ref[hq]
            mx = jnp.maximum(jnp.max(s, axis=-1, keepdims=True), sink)
            p = jnp.exp(s - mx)
            den = jnp.sum(p, axis=-1, keepdims=True) + jnp.exp(sink - mx)
            o = _dot(p.astype(BF16), vcat) / den
            outs.append(o)
    o_ref[...] = jnp.concatenate(outs, axis=-1).astype(o_ref.dtype)


def _swa(proj, sinks, bias, *, batch, seq, q_blk, k_blk, v_blk):
    m = batch * seq
    nb = seq // SWA_BLOCK
    blk = SWA_BLOCK
    kw = SWA_KV_HEADS * SWA_HEAD_DIM
    qw = SWA_HEADS * SWA_HEAD_DIM

    def cur(cb):
        return lambda b, n, s: (b * nb + n, cb)

    def prev(cb):
        return lambda b, n, s: (b * nb + jnp.maximum(n - 1, 0), cb)

    grid_spec = pltpu.PrefetchScalarGridSpec(
        num_scalar_prefetch=1,
        grid=(batch, nb),
        in_specs=[
            pl.BlockSpec((blk, qw), cur(q_blk)),
            pl.BlockSpec((blk, kw), prev(k_blk)),
            pl.BlockSpec((blk, kw), cur(k_blk)),
            pl.BlockSpec((blk, kw), prev(v_blk)),
            pl.BlockSpec((blk, kw), cur(v_blk)),
            pl.BlockSpec((SWA_HEADS, blk, 2 * blk), lambda b, n, s: (0, 0, 0)),
        ],
        out_specs=pl.BlockSpec((blk, qw), lambda b, n, s: (b * nb + n, 0)),
    )
    return pl.pallas_call(
        _swa_kernel,
        grid_spec=grid_spec,
        out_shape=jax.ShapeDtypeStruct((m, qw), BF16),
        compiler_params=_cparams(("parallel", "arbitrary")),
        name="swa",
    )(sinks, proj, proj, proj, proj, proj, bias)


def _t5_bias_table(t5_table):
    t_loc = np.arange(SWA_BLOCK, dtype=np.int32)[:, None]
    s_loc = np.arange(2 * SWA_BLOCK, dtype=np.int32)[None, :]
    dist = t_loc + SWA_BLOCK - s_loc
    n = np.maximum(dist, 0)
    nf = np.maximum(n, 1).astype(np.float32)
    large = T5_MAX_EXACT + (np.log(nf / np.float32(T5_MAX_EXACT)) / np.float32(math.log(T5_MAX_DIST / T5_MAX_EXACT))
                            * np.float32(T5_BUCKETS - T5_MAX_EXACT)).astype(np.int32)
    large = np.minimum(large, T5_BUCKETS - 1)
    bucket = np.where(n < T5_MAX_EXACT, n, large).astype(np.int32)
    in_band = (dist >= 0) & (dist < SWA_WINDOW)
    bias = t5_table.astype(F32)[jnp.asarray(bucket)]
    bias = jnp.where(jnp.asarray(in_band)[..., None], bias, NEG_BIG)
    return bias.transpose(2, 0, 1)


def _merge_kernel(x_ref, oa_ref, ob_ref, ga_ref, gb_ref, bg_ref, wa_ref, wb_ref, wm_ref, o_ref):
    a = _dot(oa_ref[...], wa_ref[...])
    b = _dot(ob_ref[...], wb_ref[...])
    bg = bg_ref[...]
    za = ga_ref[...].astype(F32) + bg[0:1]
    zb = gb_ref[...].astype(F32) + bg[1:2]
    merged = a / (1.0 + jnp.exp(-za)) + b / (1.0 + jnp.exp(-zb))
    o_ref[...] = x_ref[...] + _dot(merged.astype(BF16), wm_ref[...])


def _merge(x, o_a, o_b, proj, b_gates, wa, wb, wm, *, tm, ga_blk, gb_blk):
    m, d = x.shape
    tm = min(tm, m)
    row = lambda i: (i, 0)
    full = lambda i: (0, 0)
    return pl.pallas_call(
        _merge_kernel,
        grid=(m // tm,),
        in_specs=[
            pl.BlockSpec((tm, d), row),
            pl.BlockSpec((tm, d), row),
            pl.BlockSpec((tm, d), row),
            pl.BlockSpec((tm, d), lambda i: (i, ga_blk)),
            pl.BlockSpec((tm, d), lambda i: (i, gb_blk)),
            pl.BlockSpec((2, d), full),
            pl.BlockSpec((d, d), full),
            pl.BlockSpec((d, d), full),
            pl.BlockSpec((d, d), full),
        ],
        out_specs=pl.BlockSpec((tm, d), row),
        out_shape=jax.ShapeDtypeStruct((m, d), F32),
        compiler_params=_cparams(("parallel",)),
        name="merge",
    )(x, o_a, o_b, proj, proj, b_gates, wa, wb, wm)


def _xa_kernel(x_ref, gxa_ref, wq_ref, km_ref, vm_ref, wo_ref, gffn_ref, wr_ref, br_ref,
               x2_ref, hn_ref, idx_ref, gate_ref):
    x1 = x_ref[...]
    hx = _rms(x1, gxa_ref[...]).astype(BF16)
    q = _dot(hx, wq_ref[...]).astype(BF16)
    scale = XA_HEAD_DIM ** -0.5
    outs = []
    for h in range(XA_HEADS):
        sl = slice(h * XA_HEAD_DIM, (h + 1) * XA_HEAD_DIM)
        s = _dot_nt(q[:, sl], km_ref[:, sl]) * scale
        mx = jnp.max(s, axis=-1, keepdims=True)
        p = jnp.exp(s - mx)
        den = jnp.sum(p, axis=-1, keepdims=True)
        outs.append((_dot(p.astype(BF16), vm_ref[:, sl]) / den).astype(BF16))
    o = jnp.concatenate(outs, axis=-1)
    x2 = x1 + _dot(o, wo_ref[...])
    x2_ref[...] = x2
    hn = _rms(x2, gffn_ref[...])
    hn_ref[...] = hn
    logits = jnp.dot(hn, wr_ref[...], preferred_element_type=F32,
                     precision=lax.Precision.HIGHEST) + br_ref[...]
    lane = lax.broadcasted_iota(jnp.int32, logits.shape, 1)
    lane4 = lax.broadcasted_iota(jnp.int32, (logits.shape[0], TOP_K), 1)
    vals = jnp.zeros((logits.shape[0], TOP_K), F32)
    idxs = jnp.zeros((logits.shape[0], TOP_K), jnp.int32)
    l = logits
    for k in range(TOP_K):
        mk = jnp.max(l, axis=-1, keepdims=True)
        ik = jnp.min(jnp.where(l == mk, lane, N_EXPERTS), axis=-1, keepdims=True)
        vals = jnp.where(lane4 == k, mk, vals)
        idxs = jnp.where(lane4 == k, ik, idxs)
        l = jnp.where(lane == ik, -jnp.inf, l)
    ev = jnp.exp(vals - vals[:, 0:1])
    gate_ref[...] = ev / jnp.sum(ev, axis=-1, keepdims=True)
    idx_ref[...] = idxs


def _xa(x1, gxa, wq, kvmem, wo, gffn, wr, br, *, batch, seq, mem_len, tm):
    m, d = x1.shape
    tm = min(tm, seq)
    nt = seq // tm
    row = lambda i: (i, 0)
    full = lambda i: (0, 0)
    return pl.pallas_call(
        _xa_kernel,
        grid=(m // tm,),
        in_specs=[
            pl.BlockSpec((tm, d), row),
            pl.BlockSpec((1, d), full),
            pl.BlockSpec((d, d), full),
            pl.BlockSpec((mem_len, d), lambda i: (i // nt, 0)),
            pl.BlockSpec((mem_len, d), lambda i: (i // nt, 1)),
            pl.BlockSpec((d, d), full),
            pl.BlockSpec((1, d), full),
            pl.BlockSpec((d, N_EXPERTS), full),
            pl.BlockSpec((1, N_EXPERTS), full),
        ],
        out_specs=[
            pl.BlockSpec((tm, d), row),
            pl.BlockSpec((tm, d), row),
            pl.BlockSpec((tm, TOP_K), row),
            pl.BlockSpec((tm, TOP_K), row),
        ],
        out_shape=[
            jax.ShapeDtypeStruct((m, d), F32),
            jax.ShapeDtypeStruct((m, d), F32),
            jax.ShapeDtypeStruct((m, TOP_K), jnp.int32),
            jax.ShapeDtypeStruct((m, TOP_K), F32),
        ],
        compiler_params=_cparams(("parallel",)),
        name="xattn_router",
    )(x1, gxa, wq, kvmem, kvmem, wo, gffn, wr, br)


def _moe_kernel(bexp_ref, inv_hbm, hn_hbm, wgu_ref, bgu_ref, wd_ref, bd_ref, y_hbm,
                idx_smem, xbuf, ybuf, idx_sem, g_sem, s_sem, *, n_tok, n_blocks):
    i = pl.program_id(0)
    blk = MOE_BLOCK

    def idx_copy(b, slot):
        return pltpu.make_async_copy(inv_hbm.at[b], idx_smem.at[slot], idx_sem.at[slot])

    def gather_copy(b, slot, j):
        a = idx_smem[slot, 0, j]
        tok = jnp.maximum(a, 0) // TOP_K
        return pltpu.make_async_copy(hn_hbm.at[pl.ds(tok, 1)], xbuf.at[b % 2, pl.ds(j, 1)], g_sem.at[b % 2])

    def scatter_copy(b, slot, j):
        a = idx_smem[slot, 0, j]
        dst = jnp.where(a < 0, TOP_K * n_tok + (b % 2) * blk + j, (a % TOP_K) * n_tok + a // TOP_K)
        return pltpu.make_async_copy(ybuf.at[b % 2, pl.ds(j, 1)], y_hbm.at[pl.ds(dst, 1)], s_sem.at[b % 2])

    def start_gathers(b):
        slot = b % 3

        def body(j, carry):
            gather_copy(b, slot, j).start()
            return carry
        lax.fori_loop(0, blk, body, 0)

    def wait_gathers(b):
        def body(j, carry):
            pltpu.make_async_copy(hn_hbm.at[pl.ds(0, 1)], xbuf.at[b % 2, pl.ds(j, 1)], g_sem.at[b % 2]).wait()
            return carry
        lax.fori_loop(0, blk, body, 0)

    def start_scatters(b):
        slot = b % 3

        def body(j, carry):
            scatter_copy(b, slot, j).start()
            return carry
        lax.fori_loop(0, blk, body, 0)

    def wait_scatters(b):
        def body(j, carry):
            pltpu.make_async_copy(ybuf.at[b % 2, pl.ds(j, 1)], y_hbm.at[pl.ds(0, 1)], s_sem.at[b % 2]).wait()
            return carry
        lax.fori_loop(0, blk, body, 0)

    @pl.when(i == 0)
    def _():
        ybuf[1] = jnp.zeros((blk, ybuf.shape[2]), F32)
        for half in range(2):
            spare = pltpu.make_async_copy(ybuf.at[1], y_hbm.at[pl.ds(TOP_K * n_tok + half * blk, blk)], s_sem.at[1])
            spare.start()
            spare.wait()
        idx_copy(0, 0).start()
        idx_copy(0, 0).wait()
        start_gathers(0)
        if n_blocks > 1:
            idx_copy(1, 1).start()

    @pl.when(i >= 2)
    def _():
        wait_scatters(i - 2)

    @pl.when(i + 1 < n_blocks)
    def _():
        idx_copy(i + 1, (i + 1) % 3).wait()
        start_gathers(i + 1)

    @pl.when(i + 2 < n_blocks)
    def _():
        idx_copy(i + 2, (i + 2) % 3).start()

    wait_gathers(i)

    xb = xbuf[i % 2].astype(BF16)
    gu = _dot(xb, wgu_ref[0]) + bgu_ref[0]
    x_glu = jnp.minimum(gu[:, :D_FF], SWIGLU_LIMIT)
    x_lin = jnp.clip(gu[:, D_FF:], -SWIGLU_LIMIT, SWIGLU_LIMIT)
    act = x_glu / (1.0 + jnp.exp(-SWIGLU_ALPHA * x_glu)) * (x_lin + 1.0)
    ybuf[i % 2] = _dot(act.astype(BF16), wd_ref[0]) + bd_ref[0]

    start_scatters(i)

    @pl.when(i == n_blocks - 1)
    def _():
        if n_blocks > 1:
            wait_scatters(i - 1)
        wait_scatters(i)


def _moe(block_expert, inv, hn, wgu, bgu, wd, bd, *, n_tok):
    n_blocks = inv.shape[0]
    d = hn.shape[1]
    grid_spec = pltpu.PrefetchScalarGridSpec(
        num_scalar_prefetch=1,
        grid=(n_blocks,),
        in_specs=[
            pl.BlockSpec(memory_space=pl.ANY),
            pl.BlockSpec(memory_space=pl.ANY),
            pl.BlockSpec((1, d, 2 * D_FF), lambda i, be: (be[i], 0, 0)),
            pl.BlockSpec((1, 1, 2 * D_FF), lambda i, be: (be[i], 0, 0)),
            pl.BlockSpec((1, D_FF, d), lambda i, be: (be[i], 0, 0)),
            pl.BlockSpec((1, 1, d), lambda i, be: (be[i], 0, 0)),
        ],
        out_specs=pl.BlockSpec(memory_space=pl.ANY),
        scratch_shapes=[
            pltpu.SMEM((3, 1, MOE_BLOCK), jnp.int32),
            pltpu.VMEM((2, MOE_BLOCK, d), F32),
            pltpu.VMEM((2, MOE_BLOCK, d), F32),
            pltpu.SemaphoreType.DMA((3,)),
            pltpu.SemaphoreType.DMA((2,)),
            pltpu.SemaphoreType.DMA((2,)),
        ],
    )
    return pl.pallas_call(
        functools.partial(_moe_kernel, n_tok=n_tok, n_blocks=n_blocks),
        grid_spec=grid_spec,
        out_shape=jax.ShapeDtypeStruct((TOP_K * n_tok + 2 * MOE_BLOCK, d), F32),
        compiler_params=_cparams(("arbitrary",)),
        name="moe_experts",
    )(block_expert, inv, hn, wgu, bgu, wd, bd)


def _combine_kernel(x_ref, y0_ref, y1_ref, y2_ref, y3_ref, gate_ref, gain_ref, o_ref):
    g = gate_ref[...]
    acc = x_ref[...]
    for k, y_ref in enumerate((y0_ref, y1_ref, y2_ref, y3_ref)):
        acc = acc + g[:, k:k + 1] * y_ref[...]
    o_ref[...] = _rms(acc, gain_ref[...])


def _combine(x2, y, gates, gain, *, tm):
    m, d = x2.shape
    tm = min(tm, m)
    kb = m // tm
    row = lambda i: (i, 0)
    return pl.pallas_call(
        _combine_kernel,
        grid=(m // tm,),
        in_specs=[pl.BlockSpec((tm, d), row)]
        + [pl.BlockSpec((tm, d), functools.partial(lambda i, k: (k * kb + i, 0), k=k)) for k in range(TOP_K)]
        + [pl.BlockSpec((tm, TOP_K), row), pl.BlockSpec((1, d), lambda i: (0, 0))],
        out_specs=pl.BlockSpec((tm, d), row),
        out_shape=jax.ShapeDtypeStruct((m, d), F32),
        compiler_params=_cparams(("parallel",)),
        name="combine_norm",
    )(x2, y, y, y, y, gates, gain)


def _routing_tables(top_idx, n_tok):
    n_assign = n_tok * TOP_K
    e_flat = top_idx.reshape(-1)
    order = jnp.argsort(e_flat).astype(jnp.int32)
    e_sorted = e_flat[order]
    counts = jnp.zeros((N_EXPERTS,), jnp.int32).at[e_flat].add(1)
    starts = jnp.cumsum(counts) - counts
    padded = (counts + MOE_BLOCK - 1) // MOE_BLOCK * MOE_BLOCK
    pad_ends = jnp.cumsum(padded)
    pad_starts = pad_ends - padded
    rank = jnp.arange(n_assign, dtype=jnp.int32) - starts[e_sorted]
    dest = pad_starts[e_sorted] + rank
    cap = -(-n_assign // MOE_BLOCK) * MOE_BLOCK + N_EXPERTS * MOE_BLOCK
    n_blocks = cap // MOE_BLOCK
    inv = jnp.full((cap,), -1, jnp.int32).at[dest].set(order)
    block_start = jnp.arange(n_blocks, dtype=jnp.int32) * MOE_BLOCK
    block_expert = jnp.minimum(jnp.searchsorted(pad_ends, block_start, side='right'), N_EXPERTS - 1)
    return inv.reshape(n_blocks, 1, MOE_BLOCK), block_expert.astype(jnp.int32)


def kernel(x, mem, t5_table, lb_logits, norm_mix, w_in, b_gates, hg_norm_gain, swa_sinks, w_branch_a, w_branch_b, w_mix_out, norm_xa, xa_w_q, xa_w_k, xa_w_v, xa_w_o, norm_ffn, w_router, b_router, w_gate_up, b_gate_up, w_down, b_down, norm_final):
    batch, seq, d = x.shape
    mem_len = mem.shape[1]
    m = batch * seq
    depth = norm_mix.shape[0]
    assert depth == 1, "the final norm is fused into the layer's last kernel"
    lower_bounds = jnp.cumsum(jax.nn.softmax(lb_logits.astype(F32), axis=0), axis=0)
    bias = _t5_bias_table(t5_table)
    xt = x.reshape(m, d)
    for l in range(depth):
        w = w_in[l]
        cuts = np.cumsum([0, 1024, 1024, 1024, 1024, 1024, 256, 256, 1024, 1024])
        seg = [w[:, cuts[k]:cuts[k + 1]] for k in range(9)]
        w_perm = jnp.concatenate(seg[0:5] + seg[7:9] + seg[5:7], axis=1).astype(BF16)
        proj = _norm_matmul(xt, norm_mix[l][None], w_perm, use_norm=True, tm=1024, tn=1536, out_dtype=BF16)
        o_a = _hgrn(proj, lower_bounds[l][None], hg_norm_gain[l][None], batch=batch, seq=seq, t_rows=512)
        o_b = _swa(proj, swa_sinks[l], bias, batch=batch, seq=seq, q_blk=4, k_blk=28, v_blk=29)
        x1 = _merge(xt, o_a, o_b, proj, b_gates[l], w_branch_a[l].astype(BF16), w_branch_b[l].astype(BF16),
                    w_mix_out[l].astype(BF16), tm=512, ga_blk=5, gb_blk=6)
        w_kv = jnp.concatenate([xa_w_k[l], xa_w_v[l]], axis=1).astype(BF16)
        kvmem = _norm_matmul(mem.reshape(batch * mem_len, d), norm_xa[l][None], w_kv, use_norm=False,
                             tm=1024, tn=1024, out_dtype=BF16)
        x2, hn, top_idx, gates = _xa(x1, norm_xa[l][None], xa_w_q[l].astype(BF16), kvmem, xa_w_o[l].astype(BF16),
                                     norm_ffn[l][None], w_router[l], b_router[l][None],
                                     batch=batch, seq=seq, mem_len=mem_len, tm=512)
        inv, block_expert = _routing_tables(top_idx, m)
        y = _moe(block_expert, inv, hn, w_gate_up[l].astype(BF16), b_gate_up[l][:, None, :],
                 w_down[l].astype(BF16), b_down[l][:, None, :], n_tok=m)
        xt = _combine(x2, y, gates, norm_final[None], tm=256)
    return xt.reshape(batch, seq, d)
```

```python
import functools
import math

import jax
import jax.numpy as jnp
import numpy as np
from jax import lax
from jax.experimental import pallas as pl
from jax.experimental.pallas import tpu as pltpu

F32 = jnp.float32
BF16 = jnp.bfloat16

RMS_EPS = 1e-5
D_MODEL = 1024
HG_HEADS = 8
HG_DK = 128
HG_DV = 128
HG_CHUNK = 64
HG_LEVELS = (32, 16)
HG_DIAG = 16
SWA_HEADS = 16
SWA_KV_HEADS = 4
SWA_GROUP = 4
SWA_HEAD_DIM = 64
SWA_WINDOW = 128
SWA_BLOCK = 128
T5_BUCKETS = 32
T5_MAX_EXACT = 16
T5_MAX_DIST = 128
XA_HEADS = 4
XA_HEAD_DIM = 256
N_EXPERTS = 32
TOP_K = 4
D_FF = 1024
SWIGLU_ALPHA = 1.702
SWIGLU_LIMIT = 7.0
MOE_BLOCK = 128
NEG_BIG = -1e30

VMEM_LIMIT = 56 * 1024 * 1024


def _cparams(sem):
    return pltpu.CompilerParams(dimension_semantics=sem, vmem_limit_bytes=VMEM_LIMIT)


def _rms(xf, gain):
    return xf * lax.rsqrt(jnp.mean(xf * xf, axis=-1, keepdims=True) + RMS_EPS) * gain


def _dot(a, b):
    return jnp.dot(a, b, preferred_element_type=F32)


def _dot_nt(a, b):
    return lax.dot_general(a, b, (((1,), (1,)), ((), ())), preferred_element_type=F32)


def _dot_tn(a, b):
    return lax.dot_general(a, b, (((0,), (0,)), ((), ())), preferred_element_type=F32)


def _norm_matmul_kernel(x_ref, g_ref, w_ref, o_ref, h_ref, *, use_norm):
    @pl.when(pl.program_id(1) == 0)
    def _():
        xf = x_ref[...].astype(F32)
        if use_norm:
            xf = _rms(xf, g_ref[...])
        h_ref[...] = xf.astype(BF16)

    o_ref[...] = _dot(h_ref[...], w_ref[...]).astype(o_ref.dtype)


def _norm_matmul(x, gain, w, *, use_norm, tm, tn, out_dtype):
    m, k = x.shape
    n = w.shape[1]
    tm = min(tm, m)
    tn = min(tn, n)
    return pl.pallas_call(
        functools.partial(_norm_matmul_kernel, use_norm=use_norm),
        grid=(m // tm, n // tn),
        in_specs=[
            pl.BlockSpec((tm, k), lambda i, j: (i, 0)),
            pl.BlockSpec((1, k), lambda i, j: (0, 0)),
            pl.BlockSpec((k, tn), lambda i, j: (0, j)),
        ],
        out_specs=pl.BlockSpec((tm, tn), lambda i, j: (i, j)),
        out_shape=jax.ShapeDtypeStruct((m, n), out_dtype),
        scratch_shapes=[pltpu.VMEM((tm, k), BF16)],
        compiler_params=_cparams(("parallel", "arbitrary")),
        name="norm_matmul",
    )(x, gain, w)


def _group_row(x, group, row):
    t, d = x.shape
    x3 = x.reshape(t // group, group, d)
    return jnp.broadcast_to(x3[:, row:row + 1, :], (t // group, group, d)).reshape(t, d)


def _hgrn_kernel(q_ref, f_ref, i_ref, g_ref, lb_ref, gain_ref, o_ref,
                 st_ref, qf_ref, kk_ref, gc_ref, v_ref, a_ref, lq_ref, lk_ref, *, t_rows):
    c = HG_CHUNK
    nchunk = t_rows // c

    @pl.when(pl.program_id(2) == 0)
    def _():
        st_ref[...] = jnp.zeros_like(st_ref)

    lb = lb_ref[...]
    ff = f_ref[...].astype(F32)
    e = jnp.exp(-jnp.abs(ff))
    r = 1.0 / (1.0 + e)
    sig_pos = jnp.where(ff >= 0, r, e * r)
    sig_neg = jnp.where(ff >= 0, e * r, r)
    logf = jnp.log(lb + (1.0 - lb) * sig_pos)
    kk = (1.0 - lb) * sig_neg
    qv = q_ref[...].astype(F32)
    qf = qv * (1.0 / (1.0 + jnp.exp(-qv)))

    row = lax.broadcasted_iota(jnp.int32, (t_rows, 1), 0)
    rin = row % c
    gcum = logf
    sh = 1
    while sh < c:
        gcum = gcum + jnp.where(rin >= sh, pltpu.roll(gcum, sh, 0), 0.0)
        sh *= 2

    qf_ref[...] = qf
    kk_ref[...] = kk
    gc_ref[...] = gcum
    v_ref[...] = i_ref[...].astype(F32)

    col = lax.broadcasted_iota(jnp.int32, (t_rows, c), 1)
    d = HG_DIAG
    rin_d = row % d
    base_d = (rin // d) * d
    a_diag = jnp.zeros((t_rows, c), F32)
    for j in range(d):
        gj = _group_row(gcum, d, j)
        kj = _group_row(kk, d, j)
        dec = jnp.exp(jnp.minimum(gcum - gj, 0.0))
        p = jnp.where(rin_d >= j, qf * dec * kj, 0.0)
        colv = jnp.sum(p, axis=-1, keepdims=True)
        a_diag = jnp.where(col == base_d + j, colv, a_diag)
    a_ref[...] = a_diag

    for li, b in enumerate(HG_LEVELS):
        gref = _group_row(gcum, 2 * b, b - 1)
        upper = (rin % (2 * b)) >= b
        qa = jnp.where(upper, qf * jnp.exp(jnp.minimum(gcum - gref, 0.0)), 0.0)
        kb = jnp.where(upper, 0.0, kk * jnp.exp(jnp.minimum(gref - gcum, 0.0)))
        lq_ref[li] = qa.astype(BF16)
        lk_ref[li] = kb.astype(BF16)

    gain = gain_ref[...]
    ri = lax.broadcasted_iota(jnp.int32, (c, c), 0)
    ci = lax.broadcasted_iota(jnp.int32, (c, c), 1)
    lvl_mask = []
    for b in HG_LEVELS:
        lvl_mask.append(((ri // b - ci // b) * 2 + (ci // b) % 2) == 2)

    for ch in range(nchunk):
        sl = slice(ch * c, (ch + 1) * c)
        g_c = gc_ref[sl, :]
        g_last = gc_ref[(ch + 1) * c - 1:(ch + 1) * c, :]
        a = a_ref[sl, :]
        for li in range(len(HG_LEVELS)):
            s = _dot_nt(lq_ref[li, sl, :], lk_ref[li, sl, :])
            a = a + jnp.where(lvl_mask[li], s, 0.0)
        v_c = v_ref[sl, :].astype(BF16)
        st = st_ref[...]
        q_dec = (qf_ref[sl, :] * jnp.exp(g_c)).astype(BF16)
        o = _dot_nt(q_dec, st.astype(BF16)) + _dot(a.astype(BF16), v_c)
        k_dec = (kk_ref[sl, :] * jnp.exp(g_last - g_c)).astype(BF16)
        st_ref[...] = st * jnp.exp(g_last) + _dot_tn(v_c, k_dec)
        on = _rms(o, gain)
        gv = g_ref[sl, :].astype(F32)
        o_ref[sl, :] = (on * gv * (1.0 / (1.0 + jnp.exp(-gv)))).astype(o_ref.dtype)


def _hgrn(proj, lb, gain, *, batch, seq, t_rows):
    m = batch * seq
    t_rows = min(t_rows, seq)
    nt = seq // t_rows
    h = HG_HEADS

    def spec(off):
        return pl.BlockSpec((t_rows, 128), lambda b, hh, t: (b * nt + t, off + hh))

    return pl.pallas_call(
        functools.partial(_hgrn_kernel, t_rows=t_rows),
        grid=(batch, h, nt),
        in_specs=[spec(0), spec(h), spec(2 * h), spec(3 * h),
                  pl.BlockSpec((1, 128), lambda b, hh, t: (0, hh)),
                  pl.BlockSpec((1, 128), lambda b, hh, t: (0, 0))],
        out_specs=pl.BlockSpec((t_rows, 128), lambda b, hh, t: (b * nt + t, hh)),
        out_shape=jax.ShapeDtypeStruct((m, h * HG_DV), BF16),
        scratch_shapes=[pltpu.VMEM((HG_DV, HG_DK), F32)]
        + [pltpu.VMEM((t_rows, 128), F32) for _ in range(4)]
        + [pltpu.VMEM((t_rows, HG_CHUNK), F32)]
        + [pltpu.VMEM((len(HG_LEVELS), t_rows, 128), BF16) for _ in range(2)],
        compiler_params=_cparams(("parallel", "parallel", "arbitrary")),
        name="hgrn2",
    )(proj, proj, proj, proj, lb, gain)


def _swa_kernel(sink_ref, q_ref, kp_ref, kc_ref, vp_ref, vc_ref, bias_ref, o_ref):
    blk = SWA_BLOCK
    dh = SWA_HEAD_DIM
    first = pl.program_id(1) == 0
    colk = lax.broadcasted_iota(jnp.int32, (blk, 2 * blk), 1)
    no_prev = first & (colk < blk)
    scale = dh ** -0.5
    outs = []
    for h in range(SWA_KV_HEADS):
        ks = slice(h * dh, (h + 1) * dh)
        kcat = jnp.concatenate([kp_ref[:, ks], kc_ref[:, ks]], axis=0)
        vcat = jnp.concatenate([vp_ref[:, ks], vc_ref[:, ks]], axis=0)
        for g in range(SWA_GROUP):
            hq = h * SWA_GROUP + g
            q = q_ref[:, hq * dh:(hq + 1) * dh]
            s = _dot_nt(q, kcat) * scale + bias_ref[hq]
            s = jnp.where(no_prev, NEG_BIG, s)
            sink = sink_ref[hq]
            mx = jnp.maximum(jnp.max(s, axis=-1, keepdims=True), sink)
            p = jnp.exp(s - mx)
            den = jnp.sum(p, axis=-1, keepdims=True) + jnp.exp(sink - mx)
            o = _dot(p.astype(BF16), vcat) / den
            outs.append(o)
    o_ref[...] = jnp.concatenate(outs, axis=-1).astype(o_ref.dtype)


def _swa(proj, sinks, bias, *, batch, seq, q_blk, k_blk, v_blk):
    m = batch * seq
    nb = seq // SWA_BLOCK
    blk = SWA_BLOCK
    kw = SWA_KV_HEADS * SWA_HEAD_DIM
    qw = SWA_HEADS * SWA_HEAD_DIM

    def cur(cb):
        return lambda b, n, s: (b * nb + n, cb)

    def prev(cb):
        return lambda b, n, s: (b * nb + jnp.maximum(n - 1, 0), cb)

    grid_spec = pltpu.PrefetchScalarGridSpec(
        num_scalar_prefetch=1,
        grid=(batch, nb),
        in_specs=[
            pl.BlockSpec((blk, qw), cur(q_blk)),
            pl.BlockSpec((blk, kw), prev(k_blk)),
            pl.BlockSpec((blk, kw), cur(k_blk)),
            pl.BlockSpec((blk, kw), prev(v_blk)),
            pl.BlockSpec((blk, kw), cur(v_blk)),
            pl.BlockSpec((SWA_HEADS, blk, 2 * blk), lambda b, n, s: (0, 0, 0)),
        ],
        out_specs=pl.BlockSpec((blk, qw), lambda b, n, s: (b * nb + n, 0)),
    )
    return pl.pallas_call(
        _swa_kernel,
        grid_spec=grid_spec,
        out_shape=jax.ShapeDtypeStruct((m, qw), BF16),
        compiler_params=_cparams(("parallel", "arbitrary")),
        name="swa",
    )(sinks, proj, proj, proj, proj, proj, bias)


def _t5_bias_table(t5_table):
    t_loc = np.arange(SWA_BLOCK, dtype=np.int32)[:, None]
    s_loc = np.arange(2 * SWA_BLOCK, dtype=np.int32)[None, :]
    dist = t_loc + SWA_BLOCK - s_loc
    n = np.maximum(dist, 0)
    nf = np.maximum(n, 1).astype(np.float32)
    large = T5_MAX_EXACT + (np.log(nf / np.float32(T5_MAX_EXACT)) / np.float32(math.log(T5_MAX_DIST / T5_MAX_EXACT))
                            * np.float32(T5_BUCKETS - T5_MAX_EXACT)).astype(np.int32)
    large = np.minimum(large, T5_BUCKETS - 1)
    bucket = np.where(n < T5_MAX_EXACT, n, large).astype(np.int32)
    in_band = (dist >= 0) & (dist < SWA_WINDOW)
    bias = t5_table.astype(F32)[jnp.asarray(bucket)]
    bias = jnp.where(jnp.asarray(in_band)[..., None], bias, NEG_BIG)
    return bias.transpose(2, 0, 1)


def _merge_kernel(x_ref, oa_ref, ob_ref, ga_ref, gb_ref, bg_ref, wa_ref, wb_ref, wm_ref, o_ref):
    a = _dot(oa_ref[...], wa_ref[...])
    b = _dot(ob_ref[...], wb_ref[...])
    bg = bg_ref[...]
    za = ga_ref[...].astype(F32) + bg[0:1]
    zb = gb_ref[...].astype(F32) + bg[1:2]
    merged = a / (1.0 + jnp.exp(-za)) + b / (1.0 + jnp.exp(-zb))
    o_ref[...] = x_ref[...] + _dot(merged.astype(BF16), wm_ref[...])


def _merge(x, o_a, o_b, proj, b_gates, wa, wb, wm, *, tm, ga_blk, gb_blk):
    m, d = x.shape
    tm = min(tm, m)
    row = lambda i: (i, 0)
    full = lambda i: (0, 0)
    return pl.pallas_call(
        _merge_kernel,
        grid=(m // tm,),
        in_specs=[
            pl.BlockSpec((tm, d), row),
            pl.BlockSpec((tm, d), row),
            pl.BlockSpec((tm, d), row),
            pl.BlockSpec((tm, d), lambda i: (i, ga_blk)),
            pl.BlockSpec((tm, d), lambda i: (i, gb_blk)),
            pl.BlockSpec((2, d), full),
            pl.BlockSpec((d, d), full),
            pl.BlockSpec((d, d), full),
            pl.BlockSpec((d, d), full),
        ],
        out_specs=pl.BlockSpec((tm, d), row),
        out_shape=jax.ShapeDtypeStruct((m, d), F32),
        compiler_params=_cparams(("parallel",)),
        name="merge",
    )(x, o_a, o_b, proj, proj, b_gates, wa, wb, wm)


def _xa_kernel(x_ref, gxa_ref, wq_ref, km_ref, vm_ref, wo_ref, gffn_ref, wr_ref, br_ref,
               x2_ref, hn_ref, idx_ref, gate_ref):
    x1 = x_ref[...]
    hx = _rms(x1, gxa_ref[...]).astype(BF16)
    q = _dot(hx, wq_ref[...]).astype(BF16)
    scale = XA_HEAD_DIM ** -0.5
    outs = []
    for h in range(XA_HEADS):
        sl = slice(h * XA_HEAD_DIM, (h + 1) * XA_HEAD_DIM)
        s = _dot_nt(q[:, sl], km_ref[:, sl]) * scale
        mx = jnp.max(s, axis=-1, keepdims=True)
        p = jnp.exp(s - mx)
        den = jnp.sum(p, axis=-1, keepdims=True)
        outs.append((_dot(p.astype(BF16), vm_ref[:, sl]) / den).astype(BF16))
    o = jnp.concatenate(outs, axis=-1)
    x2 = x1 + _dot(o, wo_ref[...])
    x2_ref[...] = x2
    hn = _rms(x2, gffn_ref[...])
    hn_ref[...] = hn
    logits = jnp.dot(hn, wr_ref[...], preferred_element_type=F32,
                     precision=lax.Precision.HIGHEST) + br_ref[...]
    lane = lax.broadcasted_iota(jnp.int32, logits.shape, 1)
    lane4 = lax.broadcasted_iota(jnp.int32, (logits.shape[0], TOP_K), 1)
    vals = jnp.zeros((logits.shape[0], TOP_K), F32)
    idxs = jnp.zeros((logits.shape[0], TOP_K), jnp.int32)
    l = logits
    for k in range(TOP_K):
        mk = jnp.max(l, axis=-1, keepdims=True)
        ik = jnp.min(jnp.where(l == mk, lane, N_EXPERTS), axis=-1, keepdims=True)
        vals = jnp.where(lane4 == k, mk, vals)
        idxs = jnp.where(lane4 == k, ik, idxs)
        l = jnp.where(lane == ik, -jnp.inf, l)
    ev = jnp.exp(vals - vals[:, 0:1])
    gate_ref[...] = ev / jnp.sum(ev, axis=-1, keepdims=True)
    idx_ref[...] = idxs


def _xa(x1, gxa, wq, kvmem, wo, gffn, wr, br, *, batch, seq, mem_len, tm):
    m, d = x1.shape
    tm = min(tm, seq)
    nt = seq // tm
    row = lambda i: (i, 0)
    full = lambda i: (0, 0)
    return pl.pallas_call(
        _xa_kernel,
        grid=(m // tm,),
        in_specs=[
            pl.BlockSpec((tm, d), row),
            pl.BlockSpec((1, d), full),
            pl.BlockSpec((d, d), full),
            pl.BlockSpec((mem_len, d), lambda i: (i // nt, 0)),
            pl.BlockSpec((mem_len, d), lambda i: (i // nt, 1)),
            pl.BlockSpec((d, d), full),
            pl.BlockSpec((1, d), full),
            pl.BlockSpec((d, N_EXPERTS), full),
            pl.BlockSpec((1, N_EXPERTS), full),
        ],
        out_specs=[
            pl.BlockSpec((tm, d), row),
            pl.BlockSpec((tm, d), row),
            pl.BlockSpec((tm, TOP_K), row),
            pl.BlockSpec((tm, TOP_K), row),
        ],
        out_shape=[
            jax.ShapeDtypeStruct((m, d), F32),
            jax.ShapeDtypeStruct((m, d), F32),
            jax.ShapeDtypeStruct((m, TOP_K), jnp.int32),
            jax.ShapeDtypeStruct((m, TOP_K), F32),
        ],
        compiler_params=_cparams(("parallel",)),
        name="xattn_router",
    )(x1, gxa, wq, kvmem, kvmem, wo, gffn, wr, br)


def _moe_kernel(bexp_ref, inv_hbm, hn_hbm, wgu_ref, bgu_ref, wd_ref, bd_ref, y_hbm,
                idx_smem, xbuf, ybuf, idx_sem, g_sem, s_sem, *, n_tok, n_blocks):
    i = pl.program_id(0)
    blk = MOE_BLOCK

    last = n_blocks - 1
    nslot = idx_smem.shape[0]

    def idx_copy(b, slot):
        return pltpu.make_async_copy(inv_hbm.at[jnp.minimum(b, last)], idx_smem.at[slot], idx_sem.at[slot])

    def gather_copy(slot, buf, j):
        tok = idx_smem[slot, 0, j]
        return pltpu.make_async_copy(hn_hbm.at[pl.ds(tok, 1)], xbuf.at[buf, pl.ds(j, 1)], g_sem)

    def scatter_copy(slot, buf, j):
        dst = idx_smem[slot, 1, j]
        return pltpu.make_async_copy(ybuf.at[buf, pl.ds(j, 1)], y_hbm.at[pl.ds(dst, 1)], s_sem)

    def wait_gathers(buf):
        pltpu.make_async_copy(hn_hbm.at[pl.ds(0, blk)], xbuf.at[buf], g_sem).wait()

    def wait_scatters(buf):
        pltpu.make_async_copy(ybuf.at[buf], y_hbm.at[pl.ds(0, blk)], s_sem).wait()

    @pl.when(i == 0)
    def _():
        ybuf[1] = jnp.zeros((blk, ybuf.shape[2]), F32)
        for half in range(2):
            spare = pltpu.make_async_copy(ybuf.at[1], y_hbm.at[pl.ds(TOP_K * n_tok + half * blk, blk)], s_sem)
            spare.start()
            spare.wait()

        def fill(j, carry):
            idx_smem[nslot - 1, 1, j] = TOP_K * n_tok + blk + j
            return carry
        lax.fori_loop(0, blk, fill, 0)
        idx_copy(0, 0).start()
        idx_copy(0, 0).wait()

        def first(j, carry):
            gather_copy(0, 0, j).start()
            return carry
        lax.fori_loop(0, blk, first, 0)
        idx_copy(1, 1).start()

    def step(cur):
        nxt = 1 - cur
        idx_copy(i + 1, (i + 1) % nslot).wait()
        wait_gathers(cur)

        @pl.when(i >= 1)
        def _():
            wait_scatters(cur)

        idx_copy(i + 2, (i + 2) % nslot).start()
        s_slot = (i + nslot - 1) % nslot
        g_slot = (i + 1) % nslot
        for j in range(blk):
            scatter_copy(s_slot, nxt, j).start()
            gather_copy(g_slot, nxt, j).start()

        xb = xbuf[cur].astype(BF16)
        gu = _dot(xb, wgu_ref[0]) + bgu_ref[0]
        x_glu = jnp.minimum(gu[:, :D_FF], SWIGLU_LIMIT)
        x_lin = jnp.clip(gu[:, D_FF:], -SWIGLU_LIMIT, SWIGLU_LIMIT)
        act = x_glu / (1.0 + jnp.exp(-SWIGLU_ALPHA * x_glu)) * (x_lin + 1.0)
        ybuf[cur] = _dot(act.astype(BF16), wd_ref[0]) + bd_ref[0]

        @pl.when(i == last)
        def _():
            idx_copy(i + 2, (i + 2) % nslot).wait()
            wait_gathers(nxt)
            wait_scatters(nxt)

            def tail(j, carry):
                scatter_copy(i % nslot, cur, j).start()
                return carry
            lax.fori_loop(0, blk, tail, 0)
            wait_scatters(cur)

    @pl.when(i % 2 == 0)
    def _():
        step(0)

    @pl.when(i % 2 == 1)
    def _():
        step(1)


def _moe(block_expert, inv, hn, wgu, bgu, wd, bd, *, n_tok):
    n_blocks = inv.shape[0]
    d = hn.shape[1]
    grid_spec = pltpu.PrefetchScalarGridSpec(
        num_scalar_prefetch=1,
        grid=(n_blocks,),
        in_specs=[
            pl.BlockSpec(memory_space=pl.ANY),
            pl.BlockSpec(memory_space=pl.ANY),
            pl.BlockSpec((1, d, 2 * D_FF), lambda i, be: (be[i], 0, 0)),
            pl.BlockSpec((1, 1, 2 * D_FF), lambda i, be: (be[i], 0, 0)),
            pl.BlockSpec((1, D_FF, d), lambda i, be: (be[i], 0, 0)),
            pl.BlockSpec((1, 1, d), lambda i, be: (be[i], 0, 0)),
        ],
        out_specs=pl.BlockSpec(memory_space=pl.ANY),
        scratch_shapes=[
            pltpu.SMEM((4, 2, MOE_BLOCK), jnp.int32),
            pltpu.VMEM((2, MOE_BLOCK, d), F32),
            pltpu.VMEM((2, MOE_BLOCK, d), F32),
            pltpu.SemaphoreType.DMA((4,)),
            pltpu.SemaphoreType.DMA,
            pltpu.SemaphoreType.DMA,
        ],
    )
    return pl.pallas_call(
        functools.partial(_moe_kernel, n_tok=n_tok, n_blocks=n_blocks),
        grid_spec=grid_spec,
        out_shape=jax.ShapeDtypeStruct((TOP_K * n_tok + 2 * MOE_BLOCK, d), F32),
        compiler_params=_cparams(("arbitrary",)),
        name="moe_experts",
    )(block_expert, inv, hn, wgu, bgu, wd, bd)


def _combine_kernel(x_ref, y0_ref, y1_ref, y2_ref, y3_ref, gate_ref, gain_ref, o_ref):
    g = gate_ref[...]
    acc = x_ref[...]
    for k, y_ref in enumerate((y0_ref, y1_ref, y2_ref, y3_ref)):
        acc = acc + g[:, k:k + 1] * y_ref[...]
    o_ref[...] = _rms(acc, gain_ref[...])


def _combine(x2, y, gates, gain, *, tm):
    m, d = x2.shape
    tm = min(tm, m)
    kb = m // tm
    row = lambda i: (i, 0)
    return pl.pallas_call(
        _combine_kernel,
        grid=(m // tm,),
        in_specs=[pl.BlockSpec((tm, d), row)]
        + [pl.BlockSpec((tm, d), functools.partial(lambda i, k: (k * kb + i, 0), k=k)) for k in range(TOP_K)]
        + [pl.BlockSpec((tm, TOP_K), row), pl.BlockSpec((1, d), lambda i: (0, 0))],
        out_specs=pl.BlockSpec((tm, d), row),
        out_shape=jax.ShapeDtypeStruct((m, d), F32),
        compiler_params=_cparams(("parallel",)),
        name="combine_norm",
    )(x2, y, y, y, y, gates, gain)


def _routing_tables(top_idx, n_tok):
    n_assign = n_tok * TOP_K
    e_flat = top_idx.reshape(-1)
    order = jnp.argsort(e_flat).astype(jnp.int32)
    experts = jnp.arange(N_EXPERTS, dtype=jnp.int32)
    counts = jnp.sum((e_flat[:, None] == experts[None, :]).astype(jnp.int32), axis=0)
    starts = jnp.cumsum(counts) - counts
    padded = (counts + MOE_BLOCK - 1) // MOE_BLOCK * MOE_BLOCK
    pad_ends = jnp.cumsum(padded)
    pad_starts = pad_ends - padded
    cap = -(-n_assign // MOE_BLOCK) * MOE_BLOCK + N_EXPERTS * MOE_BLOCK
    n_blocks = cap // MOE_BLOCK
    block_start = jnp.arange(n_blocks, dtype=jnp.int32) * MOE_BLOCK
    block_expert = jnp.minimum(
        jnp.sum((pad_ends[None, :] <= block_start[:, None]).astype(jnp.int32), axis=1), N_EXPERTS - 1)
    r = block_start[:, None] + jnp.arange(MOE_BLOCK, dtype=jnp.int32)[None, :]
    pos = r - pad_starts[block_expert][:, None]
    valid = pos < counts[block_expert][:, None]
    src = jnp.clip(starts[block_expert][:, None] + pos, 0, n_assign - 1)
    inv = order[src]
    tok = jnp.where(valid, inv // TOP_K, 0)
    spare = TOP_K * n_tok + ((r // MOE_BLOCK) % 2) * MOE_BLOCK + r % MOE_BLOCK
    dst = jnp.where(valid, (inv % TOP_K) * n_tok + inv // TOP_K, spare)
    table = jnp.stack([tok, dst], axis=1)
    return table, block_expert.astype(jnp.int32)


def kernel(x, mem, t5_table, lb_logits, norm_mix, w_in, b_gates, hg_norm_gain, swa_sinks, w_branch_a, w_branch_b, w_mix_out, norm_xa, xa_w_q, xa_w_k, xa_w_v, xa_w_o, norm_ffn, w_router, b_router, w_gate_up, b_gate_up, w_down, b_down, norm_final):
    batch, seq, d = x.shape
    mem_len = mem.shape[1]
    m = batch * seq
    depth = norm_mix.shape[0]
    assert depth == 1, "the final norm is fused into the layer's last kernel"
    lower_bounds = jnp.cumsum(jax.nn.softmax(lb_logits.astype(F32), axis=0), axis=0)
    bias = _t5_bias_table(t5_table)
    xt = x.reshape(m, d)
    for l in range(depth):
        w = w_in[l]
        cuts = np.cumsum([0, 1024, 1024, 1024, 1024, 1024, 256, 256, 1024, 1024])
        seg = [w[:, cuts[k]:cuts[k + 1]] for k in range(9)]
        w_perm = jnp.concatenate(seg[0:5] + seg[7:9] + seg[5:7], axis=1).astype(BF16)
        proj = _norm_matmul(xt, norm_mix[l][None], w_perm, use_norm=True, tm=1024, tn=1536, out_dtype=BF16)
        o_a = _hgrn(proj, lower_bounds[l][None], hg_norm_gain[l][None], batch=batch, seq=seq, t_rows=512)
        o_b = _swa(proj, swa_sinks[l], bias, batch=batch, seq=seq, q_blk=4, k_blk=28, v_blk=29)
        x1 = _merge(xt, o_a, o_b, proj, b_gates[l], w_branch_a[l].astype(BF16), w_branch_b[l].astype(BF16),
                    w_mix_out[l].astype(BF16), tm=512, ga_blk=5, gb_blk=6)
        w_kv = jnp.concatenate([xa_w_k[l], xa_w_v[l]], axis=1).astype(BF16)
        kvmem = _norm_matmul(mem.reshape(batch * mem_len, d), norm_xa[l][None], w_kv, use_norm=False,
                             tm=1024, tn=1024, out_dtype=BF16)
        x2, hn, top_idx, gates = _xa(x1, norm_xa[l][None], xa_w_q[l].astype(BF16), kvmem, xa_w_o[l].astype(BF16),
                                     norm_ffn[l][None], w_router[l], b_router[l][None],
                                     batch=batch, seq=seq, mem_len=mem_len, tm=512)
        inv, block_expert = _routing_tables(top_idx, m)
        y = _moe(block_expert, inv, hn, w_gate_up[l].astype(BF16), b_gate_up[l][:, None, :],
                 w_down[l].astype(BF16), b_down[l][:, None, :], n_tok=m)
        xt = _combine(x2, y, gates, norm_final[None], tm=256)
    return xt.reshape(batch, seq, d)
```

```python
import functools
import math

import jax
import jax.numpy as jnp
import numpy as np
from jax import lax
from jax.experimental import pallas as pl
from jax.experimental.pallas import tpu as pltpu

F32 = jnp.float32
BF16 = jnp.bfloat16

RMS_EPS = 1e-5
D_MODEL = 1024
HG_HEADS = 8
HG_DK = 128
HG_DV = 128
HG_CHUNK = 64
HG_LEVELS = (32, 16, 8)
HG_DIAG = 8
LOG2E = 1.4426950408889634
SWA_HEADS = 16
SWA_KV_HEADS = 4
SWA_GROUP = 4
SWA_HEAD_DIM = 64
SWA_WINDOW = 128
SWA_BLOCK = 128
T5_BUCKETS = 32
T5_MAX_EXACT = 16
T5_MAX_DIST = 128
XA_HEADS = 4
XA_HEAD_DIM = 256
N_EXPERTS = 32
TOP_K = 4
D_FF = 1024
SWIGLU_ALPHA = 1.702
SWIGLU_LIMIT = 7.0
MOE_BLOCK = 128
MOE_DMA_THREADS = 2
NEG_BIG = -1e30

VMEM_LIMIT = 56 * 1024 * 1024


def _cparams(sem):
    return pltpu.CompilerParams(dimension_semantics=sem, vmem_limit_bytes=VMEM_LIMIT)


def _rms(xf, gain):
    return xf * lax.rsqrt(jnp.mean(xf * xf, axis=-1, keepdims=True) + RMS_EPS) * gain


def _dot(a, b):
    return jnp.dot(a, b, preferred_element_type=F32)


def _dot_nt(a, b):
    return lax.dot_general(a, b, (((1,), (1,)), ((), ())), preferred_element_type=F32)


def _dot_tn(a, b):
    return lax.dot_general(a, b, (((0,), (0,)), ((), ())), preferred_element_type=F32)


def _norm_matmul_kernel(x_ref, g_ref, w_ref, o_ref, h_ref, *, use_norm):
    @pl.when(pl.program_id(1) == 0)
    def _():
        xf = x_ref[...].astype(F32)
        if use_norm:
            xf = _rms(xf, g_ref[...])
        h_ref[...] = xf.astype(BF16)

    o_ref[...] = _dot(h_ref[...], w_ref[...]).astype(o_ref.dtype)


def _norm_matmul(x, gain, w, *, use_norm, tm, tn, out_dtype):
    m, k = x.shape
    n = w.shape[1]
    tm = min(tm, m)
    tn = min(tn, n)
    return pl.pallas_call(
        functools.partial(_norm_matmul_kernel, use_norm=use_norm),
        grid=(m // tm, n // tn),
        in_specs=[
            pl.BlockSpec((tm, k), lambda i, j: (i, 0)),
            pl.BlockSpec((1, k), lambda i, j: (0, 0)),
            pl.BlockSpec((k, tn), lambda i, j: (0, j)),
        ],
        out_specs=pl.BlockSpec((tm, tn), lambda i, j: (i, j)),
        out_shape=jax.ShapeDtypeStruct((m, n), out_dtype),
        scratch_shapes=[pltpu.VMEM((tm, k), BF16)],
        compiler_params=_cparams(("parallel", "arbitrary")),
        name="norm_matmul",
    )(x, gain, w)


def _rows_bcast(ref, base, group, row, n):
    parts = [jnp.broadcast_to(ref[base + g0 + row:base + g0 + row + 1, :], (group, ref.shape[1]))
             for g0 in range(0, n, group)]
    return jnp.concatenate(parts, axis=0)


def _hgrn_kernel(q_ref, f_ref, i_ref, g_ref, lb_ref, gain_ref, o_ref,
                 st_ref, kk_ref, gc_ref, *, t_rows):
    c = HG_CHUNK
    d = HG_DIAG
    nchunk = t_rows // c

    @pl.when(pl.program_id(2) == 0)
    def _():
        st_ref[...] = jnp.zeros_like(st_ref)

    lb = lb_ref[...]
    gain = gain_ref[...]
    row = lax.broadcasted_iota(jnp.int32, (c, 1), 0)
    col = lax.broadcasted_iota(jnp.int32, (c, c), 1)
    cd = col - (row // d) * d
    cd = jnp.where(cd <= row % d, cd, -1)
    lvl_mask = []
    for b in HG_LEVELS:
        rb = lax.broadcasted_iota(jnp.int32, (c, c), 0) // b
        lvl_mask.append(((rb - col // b) * 2 + (col // b) % 2) == 2)

    st = st_ref[...]
    for ch in range(nchunk):
        sl = slice(ch * c, (ch + 1) * c)
        ff = f_ref[sl, :].astype(F32)
        e = jnp.exp(-jnp.abs(ff))
        r = 1.0 / (1.0 + e)
        sig_pos = jnp.where(ff >= 0, r, e * r)
        sig_neg = jnp.where(ff >= 0, e * r, r)
        logf = jnp.log(lb + (1.0 - lb) * sig_pos)
        kk = (1.0 - lb) * sig_neg
        qv = q_ref[sl, :].astype(F32)
        qf = qv * (1.0 / (1.0 + jnp.exp(-qv)))

        gcum = logf * LOG2E
        sh = 1
        while sh < c:
            gcum = gcum + jnp.where(row >= sh, pltpu.roll(gcum, sh, 0), 0.0)
            sh *= 2
        kk_ref[sl, :] = kk
        gc_ref[sl, :] = gcum
        g_last = gcum[c - 1:c, :]

        a = jnp.zeros((c, c), F32)
        for j in range(d):
            gj = _rows_bcast(gc_ref, ch * c, d, j, c)
            kj = _rows_bcast(kk_ref, ch * c, d, j, c)
            p = qf * jnp.exp2(gcum - gj) * kj
            a = jnp.where(cd == j, jnp.sum(p, axis=-1, keepdims=True), a)

        for li, b in enumerate(HG_LEVELS):
            gref = _rows_bcast(gc_ref, ch * c, 2 * b, b - 1, c)
            upper = (row % (2 * b)) >= b
            qa = jnp.where(upper, qf * jnp.exp2(gcum - gref), 0.0).astype(BF16)
            kb = jnp.where(upper, 0.0, kk * jnp.exp2(gref - gcum)).astype(BF16)
            a = a + jnp.where(lvl_mask[li], _dot_nt(qa, kb), 0.0)

        v_c = i_ref[sl, :]
        q_dec = (qf * jnp.exp2(gcum)).astype(BF16)
        o = _dot_nt(q_dec, st.astype(BF16)) + _dot(a.astype(BF16), v_c)
        k_dec = (kk * jnp.exp2(g_last - gcum)).astype(BF16)
        st = st * jnp.exp2(g_last) + _dot_tn(v_c, k_dec)
        on = _rms(o, gain)
        gv = g_ref[sl, :].astype(F32)
        o_ref[sl, :] = (on * gv * (1.0 / (1.0 + jnp.exp(-gv)))).astype(o_ref.dtype)
    st_ref[...] = st


def _hgrn(proj, lb, gain, *, batch, seq, t_rows):
    m = batch * seq
    t_rows = min(t_rows, seq)
    nt = seq // t_rows
    h = HG_HEADS

    def spec(off):
        return pl.BlockSpec((t_rows, 128), lambda b, hh, t: (b * nt + t, off + hh))

    return pl.pallas_call(
        functools.partial(_hgrn_kernel, t_rows=t_rows),
        grid=(batch, h, nt),
        in_specs=[spec(0), spec(h), spec(2 * h), spec(3 * h),
                  pl.BlockSpec((1, 128), lambda b, hh, t: (0, hh)),
                  pl.BlockSpec((1, 128), lambda b, hh, t: (0, 0))],
        out_specs=pl.BlockSpec((t_rows, 128), lambda b, hh, t: (b * nt + t, hh)),
        out_shape=jax.ShapeDtypeStruct((m, h * HG_DV), BF16),
        scratch_shapes=[pltpu.VMEM((HG_DV, HG_DK), F32)]
        + [pltpu.VMEM((t_rows, 128), F32) for _ in range(2)],
        compiler_params=_cparams(("parallel", "parallel", "arbitrary")),
        name="hgrn2",
    )(proj, proj, proj, proj, lb, gain)


def _swa_kernel(sink_ref, q_ref, kp_ref, kc_ref, vp_ref, vc_ref, bias_ref, o_ref):
    blk = SWA_BLOCK
    dh = SWA_HEAD_DIM
    first = pl.program_id(1) == 0
    colk = lax.broadcasted_iota(jnp.int32, (blk, 2 * blk), 1)
    no_prev = first & (colk < blk)
    scale = dh ** -0.5
    outs = []
    for h in range(SWA_KV_HEADS):
        ks = slice(h * dh, (h + 1) * dh)
        kcat = jnp.concatenate([kp_ref[:, ks], kc_ref[:, ks]], axis=0)
        vcat = jnp.concatenate([vp_ref[:, ks], vc_ref[:, ks]], axis=0)
        for g in range(SWA_GROUP):
            hq = h * SWA_GROUP + g
            q = q_ref[:, hq * dh:(hq + 1) * dh]
            s = _dot_nt(q, kcat) * scale + bias_ref[hq]
            s = jnp.where(no_prev, NEG_BIG, s)
            sink = sink_ref[hq]
            mx = jnp.maximum(jnp.max(s, axis=-1, keepdims=True), sink)
            p = jnp.exp(s - mx)
            den = jnp.sum(p, axis=-1, keepdims=True) + jnp.exp(sink - mx)
            o = _dot(p.astype(BF16), vcat) / den
            outs.append(o)
    o_ref[...] = jnp.concatenate(outs, axis=-1).astype(o_ref.dtype)


def _swa(proj, sinks, bias, *, batch, seq, q_blk, k_blk, v_blk):
    m = batch * seq
    nb = seq // SWA_BLOCK
    blk = SWA_BLOCK
    kw = SWA_KV_HEADS * SWA_HEAD_DIM
    qw = SWA_HEADS * SWA_HEAD_DIM

    def cur(cb):
        return lambda b, n, s: (b * nb + n, cb)

    def prev(cb):
        return lambda b, n, s: (b * nb + jnp.maximum(n - 1, 0), cb)

    grid_spec = pltpu.PrefetchScalarGridSpec(
        num_scalar_prefetch=1,
        grid=(batch, nb),
        in_specs=[
            pl.BlockSpec((blk, qw), cur(q_blk)),
            pl.BlockSpec((blk, kw), prev(k_blk)),
            pl.BlockSpec((blk, kw), cur(k_blk)),
            pl.BlockSpec((blk, kw), prev(v_blk)),
            pl.BlockSpec((blk, kw), cur(v_blk)),
            pl.BlockSpec((SWA_HEADS, blk, 2 * blk), lambda b, n, s: (0, 0, 0)),
        ],
        out_specs=pl.BlockSpec((blk, qw), lambda b, n, s: (b * nb + n, 0)),
    )
    return pl.pallas_call(
        _swa_kernel,
        grid_spec=grid_spec,
        out_shape=jax.ShapeDtypeStruct((m, qw), BF16),
        compiler_params=_cparams(("parallel", "arbitrary")),
        name="swa",
    )(sinks, proj, proj, proj, proj, proj, bias)


def _t5_bias_table(t5_table):
    t_loc = np.arange(SWA_BLOCK, dtype=np.int32)[:, None]
    s_loc = np.arange(2 * SWA_BLOCK, dtype=np.int32)[None, :]
    dist = t_loc + SWA_BLOCK - s_loc
    n = np.maximum(dist, 0)
    nf = np.maximum(n, 1).astype(np.float32)
    large = T5_MAX_EXACT + (np.log(nf / np.float32(T5_MAX_EXACT)) / np.float32(math.log(T5_MAX_DIST / T5_MAX_EXACT))
                            * np.float32(T5_BUCKETS - T5_MAX_EXACT)).astype(np.int32)
    large = np.minimum(large, T5_BUCKETS - 1)
    bucket = np.where(n < T5_MAX_EXACT, n, large).astype(np.int32)
    in_band = (dist >= 0) & (dist < SWA_WINDOW)
    bias = t5_table.astype(F32)[jnp.asarray(bucket)]
    bias = jnp.where(jnp.asarray(in_band)[..., None], bias, NEG_BIG)
    return bias.transpose(2, 0, 1)


def _merge_kernel(x_ref, oa_ref, ob_ref, ga_ref, gb_ref, bg_ref, wa_ref, wb_ref, wm_ref, o_ref):
    a = _dot(oa_ref[...], wa_ref[...])
    b = _dot(ob_ref[...], wb_ref[...])
    bg = bg_ref[...]
    za = ga_ref[...].astype(F32) + bg[0:1]
    zb = gb_ref[...].astype(F32) + bg[1:2]
    merged = a / (1.0 + jnp.exp(-za)) + b / (1.0 + jnp.exp(-zb))
    o_ref[...] = x_ref[...] + _dot(merged.astype(BF16), wm_ref[...])


def _merge(x, o_a, o_b, proj, b_gates, wa, wb, wm, *, tm, ga_blk, gb_blk):
    m, d = x.shape
    tm = min(tm, m)
    row = lambda i: (i, 0)
    full = lambda i: (0, 0)
    return pl.pallas_call(
        _merge_kernel,
        grid=(m // tm,),
        in_specs=[
            pl.BlockSpec((tm, d), row),
            pl.BlockSpec((tm, d), row),
            pl.BlockSpec((tm, d), row),
            pl.BlockSpec((tm, d), lambda i: (i, ga_blk)),
            pl.BlockSpec((tm, d), lambda i: (i, gb_blk)),
            pl.BlockSpec((2, d), full),
            pl.BlockSpec((d, d), full),
            pl.BlockSpec((d, d), full),
            pl.BlockSpec((d, d), full),
        ],
        out_specs=pl.BlockSpec((tm, d), row),
        out_shape=jax.ShapeDtypeStruct((m, d), F32),
        compiler_params=_cparams(("parallel",)),
        name="merge",
    )(x, o_a, o_b, proj, proj, b_gates, wa, wb, wm)


def _xa_kernel(x_ref, gxa_ref, wq_ref, km_ref, vm_ref, wo_ref, gffn_ref, wr_ref, br_ref,
               x2_ref, hn_ref, idx_ref, gate_ref):
    x1 = x_ref[...]
    hx = _rms(x1, gxa_ref[...]).astype(BF16)
    q = _dot(hx, wq_ref[...]).astype(BF16)
    scale = XA_HEAD_DIM ** -0.5
    outs = []
    for h in range(XA_HEADS):
        sl = slice(h * XA_HEAD_DIM, (h + 1) * XA_HEAD_DIM)
        s = _dot_nt(q[:, sl], km_ref[:, sl]) * scale
        mx = jnp.max(s, axis=-1, keepdims=True)
        p = jnp.exp(s - mx)
        den = jnp.sum(p, axis=-1, keepdims=True)
        outs.append((_dot(p.astype(BF16), vm_ref[:, sl]) / den).astype(BF16))
    o = jnp.concatenate(outs, axis=-1)
    x2 = x1 + _dot(o, wo_ref[...])
    x2_ref[...] = x2
    hn = _rms(x2, gffn_ref[...])
    hn_ref[...] = hn
    hn_hi = hn.astype(BF16)
    hn_lo = (hn - hn_hi.astype(F32)).astype(BF16)
    big = _dot_nt(wr_ref[...], hn_hi)
    l = big[:N_EXPERTS] + big[N_EXPERTS:] + _dot_nt(wr_ref[:N_EXPERTS, :], hn_lo) + br_ref[...]
    eid = lax.broadcasted_iota(jnp.int32, l.shape, 0)
    vals = []
    idxs = []
    for k in range(TOP_K):
        mk = jnp.max(l, axis=0, keepdims=True)
        ik = jnp.min(jnp.where(l == mk, eid, N_EXPERTS), axis=0, keepdims=True)
        vals.append(mk)
        idxs.append(ik)
        l = jnp.where(eid == ik, -jnp.inf, l)
    ev = [jnp.exp(v - vals[0]) for v in vals]
    inv_den = 1.0 / (ev[0] + ev[1] + ev[2] + ev[3])
    gate_ref[...] = jnp.concatenate([e_k * inv_den for e_k in ev], axis=0)
    idx_ref[...] = jnp.concatenate(idxs, axis=0)


def _xa(x1, gxa, wq, kvmem, wo, gffn, wr, br, *, batch, seq, mem_len, tm):
    m, d = x1.shape
    tm = min(tm, seq)
    nt = seq // tm
    row = lambda i: (i, 0)
    full = lambda i: (0, 0)
    return pl.pallas_call(
        _xa_kernel,
        grid=(m // tm,),
        in_specs=[
            pl.BlockSpec((tm, d), row),
            pl.BlockSpec((1, d), full),
            pl.BlockSpec((d, d), full),
            pl.BlockSpec((mem_len, d), lambda i: (i // nt, 0)),
            pl.BlockSpec((mem_len, d), lambda i: (i // nt, 1)),
            pl.BlockSpec((d, d), full),
            pl.BlockSpec((1, d), full),
            pl.BlockSpec((2 * N_EXPERTS, d), full),
            pl.BlockSpec((N_EXPERTS, 1), full),
        ],
        out_specs=[
            pl.BlockSpec((tm, d), row),
            pl.BlockSpec((tm, d), row),
            pl.BlockSpec((TOP_K, tm), lambda i: (0, i)),
            pl.BlockSpec((TOP_K, tm), lambda i: (0, i)),
        ],
        out_shape=[
            jax.ShapeDtypeStruct((m, d), F32),
            jax.ShapeDtypeStruct((m, d), F32),
            jax.ShapeDtypeStruct((TOP_K, m), jnp.int32),
            jax.ShapeDtypeStruct((TOP_K, m), F32),
        ],
        compiler_params=_cparams(("parallel",)),
        name="xattn_router",
    )(x1, gxa, wq, kvmem, kvmem, wo, gffn, wr, br)


def _moe_kernel(bexp_ref, inv_hbm, hn_hbm, wgu_ref, bgu_ref, wd_ref, bd_ref, y_hbm,
                idx_smem, xbuf, ybuf, idx_sem, g_sem, s_sem, *, n_tok, n_blocks):
    i = pl.program_id(0)
    blk = MOE_BLOCK

    last = n_blocks - 1
    nslot = idx_smem.shape[0]

    def idx_copy(b, slot):
        return pltpu.make_async_copy(inv_hbm.at[jnp.minimum(b, last)], idx_smem.at[slot], idx_sem.at[slot])

    def gather_copy(slot, buf, j):
        tok = idx_smem[slot, 0, j]
        return pltpu.make_async_copy(hn_hbm.at[pl.ds(tok, 1)], xbuf.at[buf, pl.ds(j, 1)], g_sem)

    def scatter_copy(slot, buf, j):
        dst = idx_smem[slot, 1, j]
        return pltpu.make_async_copy(ybuf.at[buf, pl.ds(j, 1)], y_hbm.at[pl.ds(dst, 1)], s_sem)

    def wait_gathers(buf):
        pltpu.make_async_copy(hn_hbm.at[pl.ds(0, blk)], xbuf.at[buf], g_sem).wait()

    def wait_scatters(buf):
        pltpu.make_async_copy(ybuf.at[buf], y_hbm.at[pl.ds(0, blk)], s_sem).wait()

    @pl.when(i == 0)
    def _():
        ybuf[1] = jnp.zeros((blk, ybuf.shape[2]), F32)
        for half in range(2):
            spare = pltpu.make_async_copy(ybuf.at[1], y_hbm.at[pl.ds(TOP_K * n_tok + half * blk, blk)], s_sem)
            spare.start()
            spare.wait()

        def fill(j, carry):
            idx_smem[nslot - 1, 1, j] = TOP_K * n_tok + blk + j
            return carry
        lax.fori_loop(0, blk, fill, 0)
        idx_copy(0, 0).start()
        idx_copy(0, 0).wait()

        def first(j, carry):
            gather_copy(0, 0, j).start()
            return carry
        lax.fori_loop(0, blk, first, 0)
        idx_copy(1, 1).start()

    def step(cur):
        nxt = 1 - cur
        idx_copy(i + 1, (i + 1) % nslot).wait()
        wait_gathers(cur)

        @pl.when(i >= 1)
        def _():
            wait_scatters(cur)

        idx_copy(i + 2, (i + 2) % nslot).start()
        s_slot = (i + nslot - 1) % nslot
        g_slot = (i + 1) % nslot
        for j in range(blk):
            scatter_copy(s_slot, nxt, j).start(priority=j % MOE_DMA_THREADS)
            gather_copy(g_slot, nxt, j).start(priority=j % MOE_DMA_THREADS)

        xb = xbuf[cur].astype(BF16)
        gu = _dot(xb, wgu_ref[0]) + bgu_ref[0]
        x_glu = jnp.minimum(gu[:, :D_FF], SWIGLU_LIMIT)
        x_lin = jnp.clip(gu[:, D_FF:], -SWIGLU_LIMIT, SWIGLU_LIMIT)
        act = x_glu / (1.0 + jnp.exp(-SWIGLU_ALPHA * x_glu)) * (x_lin + 1.0)
        ybuf[cur] = _dot(act.astype(BF16), wd_ref[0]) + bd_ref[0]

        @pl.when(i == last)
        def _():
            idx_copy(i + 2, (i + 2) % nslot).wait()
            wait_gathers(nxt)
            wait_scatters(nxt)

            def tail(j, carry):
                scatter_copy(i % nslot, cur, j).start()
                return carry
            lax.fori_loop(0, blk, tail, 0)
            wait_scatters(cur)

    @pl.when(i % 2 == 0)
    def _():
        step(0)

    @pl.when(i % 2 == 1)
    def _():
        step(1)


def _moe(block_expert, inv, hn, wgu, bgu, wd, bd, *, n_tok):
    n_blocks = inv.shape[0]
    d = hn.shape[1]
    grid_spec = pltpu.PrefetchScalarGridSpec(
        num_scalar_prefetch=1,
        grid=(n_blocks,),
        in_specs=[
            pl.BlockSpec(memory_space=pl.ANY),
            pl.BlockSpec(memory_space=pl.ANY),
            pl.BlockSpec((1, d, 2 * D_FF), lambda i, be: (be[i], 0, 0)),
            pl.BlockSpec((1, 1, 2 * D_FF), lambda i, be: (be[i], 0, 0)),
            pl.BlockSpec((1, D_FF, d), lambda i, be: (be[i], 0, 0)),
            pl.BlockSpec((1, 1, d), lambda i, be: (be[i], 0, 0)),
        ],
        out_specs=pl.BlockSpec(memory_space=pl.ANY),
        scratch_shapes=[
            pltpu.SMEM((4, 2, MOE_BLOCK), jnp.int32),
            pltpu.VMEM((2, MOE_BLOCK, d), F32),
            pltpu.VMEM((2, MOE_BLOCK, d), F32),
            pltpu.SemaphoreType.DMA((4,)),
            pltpu.SemaphoreType.DMA,
            pltpu.SemaphoreType.DMA,
        ],
    )
    return pl.pallas_call(
        functools.partial(_moe_kernel, n_tok=n_tok, n_blocks=n_blocks),
        grid_spec=grid_spec,
        out_shape=jax.ShapeDtypeStruct((TOP_K * n_tok + 2 * MOE_BLOCK, d), F32),
        compiler_params=_cparams(("arbitrary",)),
        name="moe_experts",
    )(block_expert, inv, hn, wgu, bgu, wd, bd)


def _combine_kernel(x_ref, y0_ref, y1_ref, y2_ref, y3_ref, gate_ref, gain_ref, o_ref):
    g = gate_ref[...]
    acc = x_ref[...]
    for k, y_ref in enumerate((y0_ref, y1_ref, y2_ref, y3_ref)):
        acc = acc + g[:, k:k + 1] * y_ref[...]
    o_ref[...] = _rms(acc, gain_ref[...])


def _combine(x2, y, gates, gain, *, tm):
    m, d = x2.shape
    tm = min(tm, m)
    kb = m // tm
    row = lambda i: (i, 0)
    return pl.pallas_call(
        _combine_kernel,
        grid=(m // tm,),
        in_specs=[pl.BlockSpec((tm, d), row)]
        + [pl.BlockSpec((tm, d), functools.partial(lambda i, k: (k * kb + i, 0), k=k)) for k in range(TOP_K)]
        + [pl.BlockSpec((tm, TOP_K), row), pl.BlockSpec((1, d), lambda i: (0, 0))],
        out_specs=pl.BlockSpec((tm, d), row),
        out_shape=jax.ShapeDtypeStruct((m, d), F32),
        compiler_params=_cparams(("parallel",)),
        name="combine_norm",
    )(x2, y, y, y, y, gates, gain)


def _routing_tables(top_idx, n_tok):
    n_assign = n_tok * TOP_K
    e_flat = top_idx.reshape(-1)
    order = jnp.argsort(e_flat).astype(jnp.int32)
    experts = jnp.arange(N_EXPERTS, dtype=jnp.int32)
    counts = jnp.sum((e_flat[:, None] == experts[None, :]).astype(jnp.int32), axis=0)
    starts = jnp.cumsum(counts) - counts
    padded = (counts + MOE_BLOCK - 1) // MOE_BLOCK * MOE_BLOCK
    pad_ends = jnp.cumsum(padded)
    pad_starts = pad_ends - padded
    cap = -(-n_assign // MOE_BLOCK) * MOE_BLOCK + N_EXPERTS * MOE_BLOCK
    n_blocks = cap // MOE_BLOCK
    block_start = jnp.arange(n_blocks, dtype=jnp.int32) * MOE_BLOCK
    block_expert = jnp.minimum(
        jnp.sum((pad_ends[None, :] <= block_start[:, None]).astype(jnp.int32), axis=1), N_EXPERTS - 1)
    r = block_start[:, None] + jnp.arange(MOE_BLOCK, dtype=jnp.int32)[None, :]
    pos = r - pad_starts[block_expert][:, None]
    valid = pos < counts[block_expert][:, None]
    src = jnp.clip(starts[block_expert][:, None] + pos, 0, n_assign - 1)
    inv = order[src]
    tok = jnp.where(valid, inv % n_tok, 0)
    spare = TOP_K * n_tok + ((r // MOE_BLOCK) % 2) * MOE_BLOCK + r % MOE_BLOCK
    dst = jnp.where(valid, inv, spare)
    table = jnp.stack([tok, dst], axis=1)
    return table, block_expert.astype(jnp.int32)


def kernel(x, mem, t5_table, lb_logits, norm_mix, w_in, b_gates, hg_norm_gain, swa_sinks, w_branch_a, w_branch_b, w_mix_out, norm_xa, xa_w_q, xa_w_k, xa_w_v, xa_w_o, norm_ffn, w_router, b_router, w_gate_up, b_gate_up, w_down, b_down, norm_final):
    batch, seq, d = x.shape
    mem_len = mem.shape[1]
    m = batch * seq
    depth = norm_mix.shape[0]
    assert depth == 1, "the final norm is fused into the layer's last kernel"
    lower_bounds = jnp.cumsum(jax.nn.softmax(lb_logits.astype(F32), axis=0), axis=0)
    bias = _t5_bias_table(t5_table)
    xt = x.reshape(m, d)
    for l in range(depth):
        w = w_in[l]
        cuts = np.cumsum([0, 1024, 1024, 1024, 1024, 1024, 256, 256, 1024, 1024])
        seg = [w[:, cuts[k]:cuts[k + 1]] for k in range(9)]
        w_perm = jnp.concatenate(seg[0:5] + seg[7:9] + seg[5:7], axis=1).astype(BF16)
        proj = _norm_matmul(xt, norm_mix[l][None], w_perm, use_norm=True, tm=1024, tn=1536, out_dtype=BF16)
        o_a = _hgrn(proj, lower_bounds[l][None], hg_norm_gain[l][None], batch=batch, seq=seq, t_rows=512)
        o_b = _swa(proj, swa_sinks[l], bias, batch=batch, seq=seq, q_blk=4, k_blk=28, v_blk=29)
        x1 = _merge(xt, o_a, o_b, proj, b_gates[l], w_branch_a[l].astype(BF16), w_branch_b[l].astype(BF16),
                    w_mix_out[l].astype(BF16), tm=512, ga_blk=5, gb_blk=6)
        w_kv = jnp.concatenate([xa_w_k[l], xa_w_v[l]], axis=1).astype(BF16)
        kvmem = _norm_matmul(mem.reshape(batch * mem_len, d), norm_xa[l][None], w_kv, use_norm=False,
                             tm=1024, tn=1024, out_dtype=BF16)
        wr_hi = w_router[l].astype(BF16)
        wr_lo = (w_router[l] - wr_hi.astype(F32)).astype(BF16)
        x2, hn, top_idx, gates = _xa(x1, norm_xa[l][None], xa_w_q[l].astype(BF16), kvmem, xa_w_o[l].astype(BF16),
                                     norm_ffn[l][None], jnp.concatenate([wr_hi, wr_lo], axis=1).T,
                                     b_router[l][:, None], batch=batch, seq=seq, mem_len=mem_len, tm=512)
        gates = gates.T
        inv, block_expert = _routing_tables(top_idx, m)
        y = _moe(block_expert, inv, hn, w_gate_up[l].astype(BF16), b_gate_up[l][:, None, :],
                 w_down[l].astype(BF16), b_down[l][:, None, :], n_tok=m)
        xt = _combine(x2, y, gates, norm_final[None], tm=256)
    return xt.reshape(batch, seq, d)
```

```python
import functools
import math

import jax
import jax.numpy as jnp
import numpy as np
from jax import lax
from jax.experimental import pallas as pl
from jax.experimental.pallas import tpu as pltpu

F32 = jnp.float32
BF16 = jnp.bfloat16

RMS_EPS = 1e-5
D_MODEL = 1024
ROW_TILES = D_MODEL // 128
HG_HEADS = 8
HG_DK = 128
HG_DV = 128
HG_CHUNK = 64
HG_LEVELS = (32, 16, 8)
HG_DIAG = 8
LOG2E = 1.4426950408889634
SWA_HEADS = 16
SWA_KV_HEADS = 4
SWA_GROUP = 4
SWA_HEAD_DIM = 64
SWA_WINDOW = 128
SWA_BLOCK = 128
T5_BUCKETS = 32
T5_MAX_EXACT = 16
T5_MAX_DIST = 128
XA_HEADS = 4
XA_HEAD_DIM = 256
N_EXPERTS = 32
TOP_K = 4
D_FF = 1024
SWIGLU_ALPHA = 1.702
SWIGLU_LIMIT = 7.0
MOE_BLOCK = 128
MOE_DMA_THREADS = 2
NEG_BIG = -1e30

VMEM_LIMIT = 56 * 1024 * 1024


def _cparams(sem):
    return pltpu.CompilerParams(dimension_semantics=sem, vmem_limit_bytes=VMEM_LIMIT)


def _rms(xf, gain):
    return xf * lax.rsqrt(jnp.mean(xf * xf, axis=-1, keepdims=True) + RMS_EPS) * gain


def _dot(a, b):
    return jnp.dot(a, b, preferred_element_type=F32)


def _dot_nt(a, b):
    return lax.dot_general(a, b, (((1,), (1,)), ((), ())), preferred_element_type=F32)


def _dot_tn(a, b):
    return lax.dot_general(a, b, (((0,), (0,)), ((), ())), preferred_element_type=F32)


def _norm_matmul_kernel(x_ref, g_ref, w_ref, o_ref, h_ref, *, use_norm):
    @pl.when(pl.program_id(1) == 0)
    def _():
        xf = x_ref[...].astype(F32)
        if use_norm:
            xf = _rms(xf, g_ref[...])
        h_ref[...] = xf.astype(BF16)

    o_ref[...] = _dot(h_ref[...], w_ref[...]).astype(o_ref.dtype)


def _norm_matmul(x, gain, w, *, use_norm, tm, tn, out_dtype):
    m, k = x.shape
    n = w.shape[1]
    tm = min(tm, m)
    tn = min(tn, n)
    return pl.pallas_call(
        functools.partial(_norm_matmul_kernel, use_norm=use_norm),
        grid=(m // tm, n // tn),
        in_specs=[
            pl.BlockSpec((tm, k), lambda i, j: (i, 0)),
            pl.BlockSpec((1, k), lambda i, j: (0, 0)),
            pl.BlockSpec((k, tn), lambda i, j: (0, j)),
        ],
        out_specs=pl.BlockSpec((tm, tn), lambda i, j: (i, j)),
        out_shape=jax.ShapeDtypeStruct((m, n), out_dtype),
        scratch_shapes=[pltpu.VMEM((tm, k), BF16)],
        compiler_params=_cparams(("parallel", "arbitrary")),
        name="norm_matmul",
    )(x, gain, w)


def _rows_bcast(ref, base, group, row, n):
    parts = [jnp.broadcast_to(ref[base + g0 + row:base + g0 + row + 1, :], (group, ref.shape[1]))
             for g0 in range(0, n, group)]
    return jnp.concatenate(parts, axis=0)


def _hgrn_kernel(q_ref, f_ref, i_ref, g_ref, lb_ref, gain_ref, o_ref,
                 st_ref, kk_ref, gc_ref, *, t_rows):
    c = HG_CHUNK
    d = HG_DIAG
    nchunk = t_rows // c

    @pl.when(pl.program_id(2) == 0)
    def _():
        st_ref[...] = jnp.zeros_like(st_ref)

    lb = lb_ref[...]
    gain = gain_ref[...]
    row = lax.broadcasted_iota(jnp.int32, (c, 1), 0)
    col = lax.broadcasted_iota(jnp.int32, (c, c), 1)
    cd = col - (row // d) * d
    cd = jnp.where(cd <= row % d, cd, -1)
    lvl_mask = []
    for b in HG_LEVELS:
        rb = lax.broadcasted_iota(jnp.int32, (c, c), 0) // b
        lvl_mask.append(((rb - col // b) * 2 + (col // b) % 2) == 2)

    st = st_ref[...]
    for ch in range(nchunk):
        sl = slice(ch * c, (ch + 1) * c)
        ff = f_ref[sl, :].astype(F32)
        e = jnp.exp(-jnp.abs(ff))
        r = 1.0 / (1.0 + e)
        sig_pos = jnp.where(ff >= 0, r, e * r)
        sig_neg = jnp.where(ff >= 0, e * r, r)
        logf = jnp.log(lb + (1.0 - lb) * sig_pos)
        kk = (1.0 - lb) * sig_neg
        qv = q_ref[sl, :].astype(F32)
        qf = qv * (1.0 / (1.0 + jnp.exp(-qv)))

        gcum = logf * LOG2E
        sh = 1
        while sh < c:
            gcum = gcum + jnp.where(row >= sh, pltpu.roll(gcum, sh, 0), 0.0)
            sh *= 2
        kk_ref[sl, :] = kk
        gc_ref[sl, :] = gcum
        g_last = gcum[c - 1:c, :]

        a = jnp.zeros((c, c), F32)
        for j in range(d):
            gj = _rows_bcast(gc_ref, ch * c, d, j, c)
            kj = _rows_bcast(kk_ref, ch * c, d, j, c)
            p = qf * jnp.exp2(gcum - gj) * kj
            a = jnp.where(cd == j, jnp.sum(p, axis=-1, keepdims=True), a)

        for li, b in enumerate(HG_LEVELS):
            gref = _rows_bcast(gc_ref, ch * c, 2 * b, b - 1, c)
            upper = (row % (2 * b)) >= b
            qa = jnp.where(upper, qf * jnp.exp2(gcum - gref), 0.0).astype(BF16)
            kb = jnp.where(upper, 0.0, kk * jnp.exp2(gref - gcum)).astype(BF16)
            a = a + jnp.where(lvl_mask[li], _dot_nt(qa, kb), 0.0)

        v_c = i_ref[sl, :]
        q_dec = (qf * jnp.exp2(gcum)).astype(BF16)
        o = _dot_nt(q_dec, st.astype(BF16)) + _dot(a.astype(BF16), v_c)
        k_dec = (kk * jnp.exp2(g_last - gcum)).astype(BF16)
        st = st * jnp.exp2(g_last) + _dot_tn(v_c, k_dec)
        on = _rms(o, gain)
        gv = g_ref[sl, :].astype(F32)
        o_ref[sl, :] = (on * gv * (1.0 / (1.0 + jnp.exp(-gv)))).astype(o_ref.dtype)
    st_ref[...] = st


def _hgrn(proj, lb, gain, *, batch, seq, t_rows):
    m = batch * seq
    t_rows = min(t_rows, seq)
    nt = seq // t_rows
    h = HG_HEADS

    def spec(off):
        return pl.BlockSpec((t_rows, 128), lambda b, hh, t: (b * nt + t, off + hh))

    return pl.pallas_call(
        functools.partial(_hgrn_kernel, t_rows=t_rows),
        grid=(batch, h, nt),
        in_specs=[spec(0), spec(h), spec(2 * h), spec(3 * h),
                  pl.BlockSpec((1, 128), lambda b, hh, t: (0, hh)),
                  pl.BlockSpec((1, 128), lambda b, hh, t: (0, 0))],
        out_specs=pl.BlockSpec((t_rows, 128), lambda b, hh, t: (b * nt + t, hh)),
        out_shape=jax.ShapeDtypeStruct((m, h * HG_DV), BF16),
        scratch_shapes=[pltpu.VMEM((HG_DV, HG_DK), F32)]
        + [pltpu.VMEM((t_rows, 128), F32) for _ in range(2)],
        compiler_params=_cparams(("parallel", "parallel", "arbitrary")),
        name="hgrn2",
    )(proj, proj, proj, proj, lb, gain)


def _swa_kernel(sink_ref, q_ref, kp_ref, kc_ref, vp_ref, vc_ref, bias_ref, o_ref):
    blk = SWA_BLOCK
    dh = SWA_HEAD_DIM
    first = pl.program_id(1) == 0
    colk = lax.broadcasted_iota(jnp.int32, (blk, 2 * blk), 1)
    no_prev = first & (colk < blk)
    scale = dh ** -0.5
    outs = []
    for h in range(SWA_KV_HEADS):
        ks = slice(h * dh, (h + 1) * dh)
        kcat = jnp.concatenate([kp_ref[:, ks], kc_ref[:, ks]], axis=0)
        vcat = jnp.concatenate([vp_ref[:, ks], vc_ref[:, ks]], axis=0)
        for g in range(SWA_GROUP):
            hq = h * SWA_GROUP + g
            q = q_ref[:, hq * dh:(hq + 1) * dh]
            s = _dot_nt(q, kcat) * scale + bias_ref[hq]
            s = jnp.where(no_prev, NEG_BIG, s)
            sink = sink_ref[hq]
            mx = jnp.maximum(jnp.max(s, axis=-1, keepdims=True), sink)
            p = jnp.exp(s - mx)
            den = jnp.sum(p, axis=-1, keepdims=True) + jnp.exp(sink - mx)
            o = _dot(p.astype(BF16), vcat) / den
            outs.append(o)
    o_ref[...] = jnp.concatenate(outs, axis=-1).astype(o_ref.dtype)


def _swa(proj, sinks, bias, *, batch, seq, q_blk, k_blk, v_blk):
    m = batch * seq
    nb = seq // SWA_BLOCK
    blk = SWA_BLOCK
    kw = SWA_KV_HEADS * SWA_HEAD_DIM
    qw = SWA_HEADS * SWA_HEAD_DIM

    def cur(cb):
        return lambda b, n, s: (b * nb + n, cb)

    def prev(cb):
        return lambda b, n, s: (b * nb + jnp.maximum(n - 1, 0), cb)

    grid_spec = pltpu.PrefetchScalarGridSpec(
        num_scalar_prefetch=1,
        grid=(batch, nb),
        in_specs=[
            pl.BlockSpec((blk, qw), cur(q_blk)),
            pl.BlockSpec((blk, kw), prev(k_blk)),
            pl.BlockSpec((blk, kw), cur(k_blk)),
            pl.BlockSpec((blk, kw), prev(v_blk)),
            pl.BlockSpec((blk, kw), cur(v_blk)),
            pl.BlockSpec((SWA_HEADS, blk, 2 * blk), lambda b, n, s: (0, 0, 0)),
        ],
        out_specs=pl.BlockSpec((blk, qw), lambda b, n, s: (b * nb + n, 0)),
    )
    return pl.pallas_call(
        _swa_kernel,
        grid_spec=grid_spec,
        out_shape=jax.ShapeDtypeStruct((m, qw), BF16),
        compiler_params=_cparams(("parallel", "arbitrary")),
        name="swa",
    )(sinks, proj, proj, proj, proj, proj, bias)


def _t5_bias_table(t5_table):
    t_loc = np.arange(SWA_BLOCK, dtype=np.int32)[:, None]
    s_loc = np.arange(2 * SWA_BLOCK, dtype=np.int32)[None, :]
    dist = t_loc + SWA_BLOCK - s_loc
    n = np.maximum(dist, 0)
    nf = np.maximum(n, 1).astype(np.float32)
    large = T5_MAX_EXACT + (np.log(nf / np.float32(T5_MAX_EXACT)) / np.float32(math.log(T5_MAX_DIST / T5_MAX_EXACT))
                            * np.float32(T5_BUCKETS - T5_MAX_EXACT)).astype(np.int32)
    large = np.minimum(large, T5_BUCKETS - 1)
    bucket = np.where(n < T5_MAX_EXACT, n, large).astype(np.int32)
    in_band = (dist >= 0) & (dist < SWA_WINDOW)
    bias = t5_table.astype(F32)[jnp.asarray(bucket)]
    bias = jnp.where(jnp.asarray(in_band)[..., None], bias, NEG_BIG)
    return bias.transpose(2, 0, 1)


def _merge_kernel(x_ref, oa_ref, ob_ref, ga_ref, gb_ref, bg_ref, wa_ref, wb_ref, wm_ref, o_ref):
    a = _dot(oa_ref[...], wa_ref[...])
    b = _dot(ob_ref[...], wb_ref[...])
    bg = bg_ref[...]
    za = ga_ref[...].astype(F32) + bg[0:1]
    zb = gb_ref[...].astype(F32) + bg[1:2]
    merged = a / (1.0 + jnp.exp(-za)) + b / (1.0 + jnp.exp(-zb))
    o_ref[...] = x_ref[...] + _dot(merged.astype(BF16), wm_ref[...])


def _merge(x, o_a, o_b, proj, b_gates, wa, wb, wm, *, tm, ga_blk, gb_blk):
    m, d = x.shape
    tm = min(tm, m)
    row = lambda i: (i, 0)
    full = lambda i: (0, 0)
    return pl.pallas_call(
        _merge_kernel,
        grid=(m // tm,),
        in_specs=[
            pl.BlockSpec((tm, d), row),
            pl.BlockSpec((tm, d), row),
            pl.BlockSpec((tm, d), row),
            pl.BlockSpec((tm, d), lambda i: (i, ga_blk)),
            pl.BlockSpec((tm, d), lambda i: (i, gb_blk)),
            pl.BlockSpec((2, d), full),
            pl.BlockSpec((d, d), full),
            pl.BlockSpec((d, d), full),
            pl.BlockSpec((d, d), full),
        ],
        out_specs=pl.BlockSpec((tm, d), row),
        out_shape=jax.ShapeDtypeStruct((m, d), F32),
        compiler_params=_cparams(("parallel",)),
        name="merge",
    )(x, o_a, o_b, proj, proj, b_gates, wa, wb, wm)


def _xa_kernel(x_ref, gxa_ref, wq_ref, km_ref, vm_ref, wo_ref, gffn_ref, wr_ref, br_ref,
               x2_ref, hn_ref, idx_ref, gate_ref):
    x1 = x_ref[...]
    hx = _rms(x1, gxa_ref[...]).astype(BF16)
    q = _dot(hx, wq_ref[...]).astype(BF16)
    scale = XA_HEAD_DIM ** -0.5
    outs = []
    for h in range(XA_HEADS):
        sl = slice(h * XA_HEAD_DIM, (h + 1) * XA_HEAD_DIM)
        s = _dot_nt(q[:, sl], km_ref[:, sl]) * scale
        mx = jnp.max(s, axis=-1, keepdims=True)
        p = jnp.exp(s - mx)
        den = jnp.sum(p, axis=-1, keepdims=True)
        outs.append((_dot(p.astype(BF16), vm_ref[:, sl]) / den).astype(BF16))
    o = jnp.concatenate(outs, axis=-1)
    x2 = x1 + _dot(o, wo_ref[...])
    x2_ref[...] = x2
    hn = _rms(x2, gffn_ref[...])
    for s in range(ROW_TILES):
        hn_ref[:, s, :] = hn[:, s * 128:(s + 1) * 128]
    hn_hi = hn.astype(BF16)
    hn_lo = (hn - hn_hi.astype(F32)).astype(BF16)
    big = _dot_nt(wr_ref[...], hn_hi)
    l = big[:N_EXPERTS] + big[N_EXPERTS:] + _dot_nt(wr_ref[:N_EXPERTS, :], hn_lo) + br_ref[...]
    eid = lax.broadcasted_iota(jnp.int32, l.shape, 0)
    vals = []
    idxs = []
    for k in range(TOP_K):
        mk = jnp.max(l, axis=0, keepdims=True)
        ik = jnp.min(jnp.where(l == mk, eid, N_EXPERTS), axis=0, keepdims=True)
        vals.append(mk)
        idxs.append(ik)
        l = jnp.where(eid == ik, -jnp.inf, l)
    ev = [jnp.exp(v - vals[0]) for v in vals]
    inv_den = 1.0 / (ev[0] + ev[1] + ev[2] + ev[3])
    gate_ref[...] = jnp.concatenate([e_k * inv_den for e_k in ev], axis=0)
    idx_ref[...] = jnp.concatenate(idxs, axis=0)


def _xa(x1, gxa, wq, kvmem, wo, gffn, wr, br, *, batch, seq, mem_len, tm):
    m, d = x1.shape
    tm = min(tm, seq)
    nt = seq // tm
    row = lambda i: (i, 0)
    full = lambda i: (0, 0)
    return pl.pallas_call(
        _xa_kernel,
        grid=(m // tm,),
        in_specs=[
            pl.BlockSpec((tm, d), row),
            pl.BlockSpec((1, d), full),
            pl.BlockSpec((d, d), full),
            pl.BlockSpec((mem_len, d), lambda i: (i // nt, 0)),
            pl.BlockSpec((mem_len, d), lambda i: (i // nt, 1)),
            pl.BlockSpec((d, d), full),
            pl.BlockSpec((1, d), full),
            pl.BlockSpec((2 * N_EXPERTS, d), full),
            pl.BlockSpec((N_EXPERTS, 1), full),
        ],
        out_specs=[
            pl.BlockSpec((tm, d), row),
            pl.BlockSpec((tm, ROW_TILES, 128), lambda i: (i, 0, 0)),
            pl.BlockSpec((TOP_K, tm), lambda i: (0, i)),
            pl.BlockSpec((TOP_K, tm), lambda i: (0, i)),
        ],
        out_shape=[
            jax.ShapeDtypeStruct((m, d), F32),
            jax.ShapeDtypeStruct((m, ROW_TILES, 128), F32),
            jax.ShapeDtypeStruct((TOP_K, m), jnp.int32),
            jax.ShapeDtypeStruct((TOP_K, m), F32),
        ],
        compiler_params=_cparams(("parallel",)),
        name="xattn_router",
    )(x1, gxa, wq, kvmem, kvmem, wo, gffn, wr, br)


def _moe_kernel(bexp_ref, inv_hbm, hn_hbm, wgu_ref, bgu_ref, wd_ref, bd_ref, y_hbm,
                idx_smem, xbuf, ybuf, idx_sem, g_sem, s_sem, *, n_tok, n_blocks):
    i = pl.program_id(0)
    blk = MOE_BLOCK

    last = n_blocks - 1
    nslot = idx_smem.shape[0]

    def idx_copy(b, slot):
        return pltpu.make_async_copy(inv_hbm.at[jnp.minimum(b, last)], idx_smem.at[slot], idx_sem.at[slot])

    def gather_copy(slot, buf, j):
        tok = idx_smem[slot, 0, j]
        return pltpu.make_async_copy(hn_hbm.at[tok], xbuf.at[buf, j], g_sem)

    def scatter_copy(slot, buf, j):
        dst = idx_smem[slot, 1, j]
        return pltpu.make_async_copy(ybuf.at[buf, j], y_hbm.at[dst], s_sem)

    def wait_gathers(buf):
        pltpu.make_async_copy(hn_hbm.at[pl.ds(0, blk)], xbuf.at[buf], g_sem).wait()

    def wait_scatters(buf):
        pltpu.make_async_copy(ybuf.at[buf], y_hbm.at[pl.ds(0, blk)], s_sem).wait()

    @pl.when(i == 0)
    def _():
        ybuf[1] = jnp.zeros(ybuf.shape[1:], F32)
        for half in range(2):
            spare = pltpu.make_async_copy(ybuf.at[1], y_hbm.at[pl.ds(TOP_K * n_tok + half * blk, blk)], s_sem)
            spare.start()
            spare.wait()

        def fill(j, carry):
            idx_smem[nslot - 1, 1, j] = TOP_K * n_tok + blk + j
            return carry
        lax.fori_loop(0, blk, fill, 0)
        idx_copy(0, 0).start()
        idx_copy(0, 0).wait()

        def first(j, carry):
            gather_copy(0, 0, j).start()
            return carry
        lax.fori_loop(0, blk, first, 0)
        idx_copy(1, 1).start()

    def step(cur):
        nxt = 1 - cur
        idx_copy(i + 1, (i + 1) % nslot).wait()
        wait_gathers(cur)

        @pl.when(i >= 1)
        def _():
            wait_scatters(cur)

        idx_copy(i + 2, (i + 2) % nslot).start()
        s_slot = (i + nslot - 1) % nslot
        g_slot = (i + 1) % nslot
        for j in range(blk):
            scatter_copy(s_slot, nxt, j).start(priority=j % MOE_DMA_THREADS)
            gather_copy(g_slot, nxt, j).start(priority=j % MOE_DMA_THREADS)

        xb = jnp.concatenate([xbuf[cur, :, s, :] for s in range(ROW_TILES)], axis=-1).astype(BF16)
        gu = _dot(xb, wgu_ref[0]) + bgu_ref[0]
        x_glu = jnp.minimum(gu[:, :D_FF], SWIGLU_LIMIT)
        x_lin = jnp.clip(gu[:, D_FF:], -SWIGLU_LIMIT, SWIGLU_LIMIT)
        act = x_glu / (1.0 + jnp.exp(-SWIGLU_ALPHA * x_glu)) * (x_lin + 1.0)
        y = _dot(act.astype(BF16), wd_ref[0]) + bd_ref[0]
        for s in range(ROW_TILES):
            ybuf[cur, :, s, :] = y[:, s * 128:(s + 1) * 128]

        @pl.when(i == last)
        def _():
            idx_copy(i + 2, (i + 2) % nslot).wait()
            wait_gathers(nxt)
            wait_scatters(nxt)

            def tail(j, carry):
                scatter_copy(i % nslot, cur, j).start()
                return carry
            lax.fori_loop(0, blk, tail, 0)
            wait_scatters(cur)

    @pl.when(i % 2 == 0)
    def _():
        step(0)

    @pl.when(i % 2 == 1)
    def _():
        step(1)


def _moe(block_expert, inv, hn, wgu, bgu, wd, bd, *, n_tok):
    n_blocks = inv.shape[0]
    d = D_MODEL
    grid_spec = pltpu.PrefetchScalarGridSpec(
        num_scalar_prefetch=1,
        grid=(n_blocks,),
        in_specs=[
            pl.BlockSpec(memory_space=pl.ANY),
            pl.BlockSpec(memory_space=pl.ANY),
            pl.BlockSpec((1, d, 2 * D_FF), lambda i, be: (be[i], 0, 0)),
            pl.BlockSpec((1, 1, 2 * D_FF), lambda i, be: (be[i], 0, 0)),
            pl.BlockSpec((1, D_FF, d), lambda i, be: (be[i], 0, 0)),
            pl.BlockSpec((1, 1, d), lambda i, be: (be[i], 0, 0)),
        ],
        out_specs=pl.BlockSpec(memory_space=pl.ANY),
        scratch_shapes=[
            pltpu.SMEM((4, 2, MOE_BLOCK), jnp.int32),
            pltpu.VMEM((2, MOE_BLOCK, ROW_TILES, 128), F32),
            pltpu.VMEM((2, MOE_BLOCK, ROW_TILES, 128), F32),
            pltpu.SemaphoreType.DMA((4,)),
            pltpu.SemaphoreType.DMA,
            pltpu.SemaphoreType.DMA,
        ],
    )
    return pl.pallas_call(
        functools.partial(_moe_kernel, n_tok=n_tok, n_blocks=n_blocks),
        grid_spec=grid_spec,
        out_shape=jax.ShapeDtypeStruct((TOP_K * n_tok + 2 * MOE_BLOCK, ROW_TILES, 128), F32),
        compiler_params=_cparams(("arbitrary",)),
        name="moe_experts",
    )(block_expert, inv, hn, wgu, bgu, wd, bd)


def _combine_kernel(x_ref, y0_ref, y1_ref, y2_ref, y3_ref, gate_ref, gain_ref, o_ref):
    g = gate_ref[...]
    acc = x_ref[...]
    for k, y_ref in enumerate((y0_ref, y1_ref, y2_ref, y3_ref)):
        yk = jnp.concatenate([y_ref[:, s, :] for s in range(ROW_TILES)], axis=-1)
        acc = acc + g[:, k:k + 1] * yk
    o_ref[...] = _rms(acc, gain_ref[...])


def _combine(x2, y, gates, gain, *, tm):
    m, d = x2.shape
    tm = min(tm, m)
    kb = m // tm
    row = lambda i: (i, 0)
    return pl.pallas_call(
        _combine_kernel,
        grid=(m // tm,),
        in_specs=[pl.BlockSpec((tm, d), row)]
        + [pl.BlockSpec((tm, ROW_TILES, 128), functools.partial(lambda i, k: (k * kb + i, 0, 0), k=k))
           for k in range(TOP_K)]
        + [pl.BlockSpec((tm, TOP_K), row), pl.BlockSpec((1, d), lambda i: (0, 0))],
        out_specs=pl.BlockSpec((tm, d), row),
        out_shape=jax.ShapeDtypeStruct((m, d), F32),
        compiler_params=_cparams(("parallel",)),
        name="combine_norm",
    )(x2, y, y, y, y, gates, gain)


def _routing_tables(top_idx, n_tok):
    n_assign = n_tok * TOP_K
    e_flat = top_idx.reshape(-1)
    order = jnp.argsort(e_flat).astype(jnp.int32)
    experts = jnp.arange(N_EXPERTS, dtype=jnp.int32)
    counts = jnp.sum((e_flat[:, None] == experts[None, :]).astype(jnp.int32), axis=0)
    starts = jnp.cumsum(counts) - counts
    padded = (counts + MOE_BLOCK - 1) // MOE_BLOCK * MOE_BLOCK
    pad_ends = jnp.cumsum(padded)
    pad_starts = pad_ends - padded
    cap = -(-n_assign // MOE_BLOCK) * MOE_BLOCK + N_EXPERTS * MOE_BLOCK
    n_blocks = cap // MOE_BLOCK
    block_start = jnp.arange(n_blocks, dtype=jnp.int32) * MOE_BLOCK
    block_expert = jnp.minimum(
        jnp.sum((pad_ends[None, :] <= block_start[:, None]).astype(jnp.int32), axis=1), N_EXPERTS - 1)
    r = block_start[:, None] + jnp.arange(MOE_BLOCK, dtype=jnp.int32)[None, :]
    pos = r - pad_starts[block_expert][:, None]
    valid = pos < counts[block_expert][:, None]
    src = jnp.clip(starts[block_expert][:, None] + pos, 0, n_assign - 1)
    inv = order[src]
    tok = jnp.where(valid, inv % n_tok, 0)
    spare = TOP_K * n_tok + ((r // MOE_BLOCK) % 2) * MOE_BLOCK + r % MOE_BLOCK
    dst = jnp.where(valid, inv, spare)
    table = jnp.stack([tok, dst], axis=1)
    return table, block_expert.astype(jnp.int32)


def kernel(x, mem, t5_table, lb_logits, norm_mix, w_in, b_gates, hg_norm_gain, swa_sinks, w_branch_a, w_branch_b, w_mix_out, norm_xa, xa_w_q, xa_w_k, xa_w_v, xa_w_o, norm_ffn, w_router, b_router, w_gate_up, b_gate_up, w_down, b_down, norm_final):
    batch, seq, d = x.shape
    mem_len = mem.shape[1]
    m = batch * seq
    depth = norm_mix.shape[0]
    assert depth == 1, "the final norm is fused into the layer's last kernel"
    lower_bounds = jnp.cumsum(jax.nn.softmax(lb_logits.astype(F32), axis=0), axis=0)
    bias = _t5_bias_table(t5_table)
    xt = x.reshape(m, d)
    for l in range(depth):
        w = w_in[l]
        cuts = np.cumsum([0, 1024, 1024, 1024, 1024, 1024, 256, 256, 1024, 1024])
        seg = [w[:, cuts[k]:cuts[k + 1]] for k in range(9)]
        w_perm = jnp.concatenate(seg[0:5] + seg[7:9] + seg[5:7], axis=1).astype(BF16)
        proj = _norm_matmul(xt, norm_mix[l][None], w_perm, use_norm=True, tm=1024, tn=1536, out_dtype=BF16)
        o_a = _hgrn(proj, lower_bounds[l][None], hg_norm_gain[l][None], batch=batch, seq=seq, t_rows=512)
        o_b = _swa(proj, swa_sinks[l], bias, batch=batch, seq=seq, q_blk=4, k_blk=28, v_blk=29)
        x1 = _merge(xt, o_a, o_b, proj, b_gates[l], w_branch_a[l].astype(BF16), w_branch_b[l].astype(BF16),
                    w_mix_out[l].astype(BF16), tm=512, ga_blk=5, gb_blk=6)
        w_kv = jnp.concatenate([xa_w_k[l], xa_w_v[l]], axis=1).astype(BF16)
        kvmem = _norm_matmul(mem.reshape(batch * mem_len, d), norm_xa[l][None], w_kv, use_norm=False,
                             tm=1024, tn=1024, out_dtype=BF16)
        wr_hi = w_router[l].astype(BF16)
        wr_lo = (w_router[l] - wr_hi.astype(F32)).astype(BF16)
        x2, hn, top_idx, gates = _xa(x1, norm_xa[l][None], xa_w_q[l].astype(BF16), kvmem, xa_w_o[l].astype(BF16),
                                     norm_ffn[l][None], jnp.concatenate([wr_hi, wr_lo], axis=1).T,
                                     b_router[l][:, None], batch=batch, seq=seq, mem_len=mem_len, tm=512)
        gates = gates.T
        inv, block_expert = _routing_tables(top_idx, m)
        y = _moe(block_expert, inv, hn, w_gate_up[l].astype(BF16), b_gate_up[l][:, None, :],
                 w_down[l].astype(BF16), b_down[l][:, None, :], n_tok=m)
        xt = _combine(x2, y, gates, norm_final[None], tm=256)
    return xt.reshape(batch, seq, d)
```

```python
import functools
import math

import jax
import jax.numpy as jnp
import numpy as np
from jax import lax
from jax.experimental import pallas as pl
from jax.experimental.pallas import tpu as pltpu

F32 = jnp.float32
BF16 = jnp.bfloat16

RMS_EPS = 1e-5
D_MODEL = 1024
HG_HEADS = 8
HG_DK = 128
HG_DV = 128
HG_CHUNK = 64
HG_LEVELS = (32, 16, 8)
HG_DIAG = 8
HG_GROUP = 4
LOG2E = 1.4426950408889634
SWA_HEADS = 16
SWA_KV_HEADS = 4
SWA_GROUP = 4
SWA_HEAD_DIM = 64
SWA_WINDOW = 128
SWA_BLOCK = 128
T5_BUCKETS = 32
T5_MAX_EXACT = 16
T5_MAX_DIST = 128
XA_HEADS = 4
XA_HEAD_DIM = 256
N_EXPERTS = 32
TOP_K = 4
D_FF = 1024
SWIGLU_ALPHA = 1.702
SWIGLU_LIMIT = 7.0
MOE_BLOCK = 128
MOE_DMA_THREADS = 2
NEG_BIG = -1e30

VMEM_LIMIT = 56 * 1024 * 1024


def _cparams(sem):
    return pltpu.CompilerParams(dimension_semantics=sem, vmem_limit_bytes=VMEM_LIMIT)


def _rms(xf, gain):
    return xf * lax.rsqrt(jnp.mean(xf * xf, axis=-1, keepdims=True) + RMS_EPS) * gain


def _dot(a, b):
    return jnp.dot(a, b, preferred_element_type=F32)


def _dot_nt(a, b):
    return lax.dot_general(a, b, (((1,), (1,)), ((), ())), preferred_element_type=F32)


def _dot_tn(a, b):
    return lax.dot_general(a, b, (((0,), (0,)), ((), ())), preferred_element_type=F32)


def _norm_matmul_kernel(x_ref, g_ref, w_ref, o_ref, h_ref, *, use_norm):
    @pl.when(pl.program_id(1) == 0)
    def _():
        xf = x_ref[...].astype(F32)
        if use_norm:
            xf = _rms(xf, g_ref[...])
        h_ref[...] = xf.astype(BF16)

    o_ref[...] = _dot(h_ref[...], w_ref[...]).astype(o_ref.dtype)


def _norm_matmul(x, gain, w, *, use_norm, tm, tn, out_dtype):
    m, k = x.shape
    n = w.shape[1]
    tm = min(tm, m)
    tn = min(tn, n)
    return pl.pallas_call(
        functools.partial(_norm_matmul_kernel, use_norm=use_norm),
        grid=(m // tm, n // tn),
        in_specs=[
            pl.BlockSpec((tm, k), lambda i, j: (i, 0)),
            pl.BlockSpec((1, k), lambda i, j: (0, 0)),
            pl.BlockSpec((k, tn), lambda i, j: (0, j)),
        ],
        out_specs=pl.BlockSpec((tm, tn), lambda i, j: (i, j)),
        out_shape=jax.ShapeDtypeStruct((m, n), out_dtype),
        scratch_shapes=[pltpu.VMEM((tm, k), BF16)],
        compiler_params=_cparams(("parallel", "arbitrary")),
        name="norm_matmul",
    )(x, gain, w)


def _rows_bcast(ref, base, group, row, n):
    parts = [jnp.broadcast_to(ref[pl.ds(base + g0 + row, 1), :], (group, ref.shape[1]))
             for g0 in range(0, n, group)]
    return jnp.concatenate(parts, axis=0)


def _hgrn_kernel(q_ref, f_ref, i_ref, g_ref, lb_ref, gain_ref, o_ref,
                 st_ref, kk_ref, gc_ref, a_ref, qd_ref, kv_ref, dec_ref, *, t_rows):
    c = HG_CHUNK
    d = HG_DIAG
    grows = HG_GROUP * c
    ngroups = t_rows // grows

    @pl.when(pl.program_id(2) == 0)
    def _():
        st_ref[...] = jnp.zeros_like(st_ref)

    lb = lb_ref[...]
    gain = gain_ref[...]
    row = lax.broadcasted_iota(jnp.int32, (c, 1), 0)
    col = lax.broadcasted_iota(jnp.int32, (c, c), 1)
    cd = col - (row // d) * d
    cd = jnp.where(cd <= row % d, cd, -1)
    lvl_mask = []
    for b in HG_LEVELS:
        rb = lax.broadcasted_iota(jnp.int32, (c, c), 0) // b
        lvl_mask.append(((rb - col // b) * 2 + (col // b) % 2) == 2)

    def rows(ref_row0):
        return pl.ds(pl.multiple_of(ref_row0, c), c)

    def phase_a(gi, buf):
        for ch in range(HG_GROUP):
            src = rows(gi * grows + ch * c)
            dst = buf * grows + ch * c
            _hgrn_front(q_ref, f_ref, i_ref, lb, row, cd, lvl_mask, kk_ref, gc_ref, a_ref, qd_ref, kv_ref,
                        dec_ref, src, dst, buf * HG_GROUP + ch)

    def phase_b(gi, buf, st):
        sts = []
        for ch in range(HG_GROUP):
            sts.append(st.astype(BF16))
            slot = buf * HG_GROUP + ch
            st = st * dec_ref[pl.ds(slot, 1), :] + kv_ref[slot]
        for ch in range(HG_GROUP):
            src = rows(gi * grows + ch * c)
            dst = rows(buf * grows + ch * c)
            o = _dot_nt(qd_ref[dst, :], sts[ch]) + _dot(a_ref[dst, :], i_ref[src, :])
            on = _rms(o, gain)
            gv = g_ref[src, :].astype(F32)
            o_ref[src, :] = (on * gv * (1.0 / (1.0 + jnp.exp(-gv)))).astype(o_ref.dtype)
        return st

    phase_a(0, 0)

    def body(k, st):
        st = phase_b(k, k % 2, st)
        phase_a(k + 1, (k + 1) % 2)
        return st

    st = lax.fori_loop(0, ngroups - 1, body, st_ref[...])
    st_ref[...] = phase_b(ngroups - 1, (ngroups - 1) % 2, st)


def _hgrn_front(q_ref, f_ref, i_ref, lb, row, cd, lvl_mask, kk_ref, gc_ref, a_ref, qd_ref, kv_ref, dec_ref,
                sl, dst, slot):
    c = HG_CHUNK
    d = HG_DIAG
    ds = pl.ds(pl.multiple_of(dst, c), c)
    ff = f_ref[sl, :].astype(F32)
    e = jnp.exp(-jnp.abs(ff))
    r = 1.0 / (1.0 + e)
    sig_pos = jnp.where(ff >= 0, r, e * r)
    sig_neg = jnp.where(ff >= 0, e * r, r)
    logf = jnp.log(lb + (1.0 - lb) * sig_pos)
    kk = (1.0 - lb) * sig_neg
    qv = q_ref[sl, :].astype(F32)
    qf = qv * (1.0 / (1.0 + jnp.exp(-qv)))

    gcum = logf * LOG2E
    sh = 1
    while sh < c:
        gcum = gcum + jnp.where(row >= sh, pltpu.roll(gcum, sh, 0), 0.0)
        sh *= 2
    kk_ref[ds, :] = kk
    gc_ref[ds, :] = gcum
    g_last = gcum[c - 1:c, :]

    a = jnp.zeros((c, c), F32)
    for j in range(d):
        gj = _rows_bcast(gc_ref, dst, d, j, c)
        kj = _rows_bcast(kk_ref, dst, d, j, c)
        p = qf * jnp.exp2(gcum - gj) * kj
        a = jnp.where(cd == j, jnp.sum(p, axis=-1, keepdims=True), a)

    for li, b in enumerate(HG_LEVELS):
        gref = _rows_bcast(gc_ref, dst, 2 * b, b - 1, c)
        upper = (row % (2 * b)) >= b
        qa = jnp.where(upper, qf * jnp.exp2(gcum - gref), 0.0).astype(BF16)
        kb = jnp.where(upper, 0.0, kk * jnp.exp2(gref - gcum)).astype(BF16)
        a = a + jnp.where(lvl_mask[li], _dot_nt(qa, kb), 0.0)
    a_ref[ds, :] = a.astype(BF16)

    qd_ref[ds, :] = (qf * jnp.exp2(gcum)).astype(BF16)
    k_dec = (kk * jnp.exp2(g_last - gcum)).astype(BF16)
    kv_ref[slot] = _dot_tn(i_ref[sl, :], k_dec)
    dec_ref[pl.ds(slot, 1), :] = jnp.exp2(g_last)


def _hgrn(proj, lb, gain, *, batch, seq, t_rows):
    m = batch * seq
    t_rows = min(t_rows, seq)
    nt = seq // t_rows
    h = HG_HEADS
    grows = HG_GROUP * HG_CHUNK
    assert t_rows % grows == 0

    def spec(off):
        return pl.BlockSpec((t_rows, 128), lambda b, hh, t: (b * nt + t, off + hh))

    return pl.pallas_call(
        functools.partial(_hgrn_kernel, t_rows=t_rows),
        grid=(batch, h, nt),
        in_specs=[spec(0), spec(h), spec(2 * h), spec(3 * h),
                  pl.BlockSpec((1, 128), lambda b, hh, t: (0, hh)),
                  pl.BlockSpec((1, 128), lambda b, hh, t: (0, 0))],
        out_specs=pl.BlockSpec((t_rows, 128), lambda b, hh, t: (b * nt + t, hh)),
        out_shape=jax.ShapeDtypeStruct((m, h * HG_DV), BF16),
        scratch_shapes=[
            pltpu.VMEM((HG_DV, HG_DK), F32),
            pltpu.VMEM((2 * grows, 128), F32),
            pltpu.VMEM((2 * grows, 128), F32),
            pltpu.VMEM((2 * grows, HG_CHUNK), BF16),
            pltpu.VMEM((2 * grows, 128), BF16),
            pltpu.VMEM((2 * HG_GROUP, HG_DV, HG_DK), F32),
            pltpu.VMEM((2 * HG_GROUP, 128), F32),
        ],
        compiler_params=_cparams(("parallel", "parallel", "arbitrary")),
        name="hgrn2",
    )(proj, proj, proj, proj, lb, gain)


def _swa_kernel(sink_ref, q_ref, kp_ref, kc_ref, vp_ref, vc_ref, bias_ref, o_ref):
    blk = SWA_BLOCK
    dh = SWA_HEAD_DIM
    first = pl.program_id(1) == 0
    colk = lax.broadcasted_iota(jnp.int32, (blk, 2 * blk), 1)
    no_prev = first & (colk < blk)
    scale = dh ** -0.5
    outs = []
    for h in range(SWA_KV_HEADS):
        ks = slice(h * dh, (h + 1) * dh)
        kcat = jnp.concatenate([kp_ref[:, ks], kc_ref[:, ks]], axis=0)
        vcat = jnp.concatenate([vp_ref[:, ks], vc_ref[:, ks]], axis=0)
        for g in range(SWA_GROUP):
            hq = h * SWA_GROUP + g
            q = q_ref[:, hq * dh:(hq + 1) * dh]
            s = _dot_nt(q, kcat) * scale + bias_ref[hq]
            s = jnp.where(no_prev, NEG_BIG, s)
            sink = sink_ref[hq]
            mx = jnp.maximum(jnp.max(s, axis=-1, keepdims=True), sink)
            p = jnp.exp(s - mx)
            den = jnp.sum(p, axis=-1, keepdims=True) + jnp.exp(sink - mx)
            o = _dot(p.astype(BF16), vcat) / den
            outs.append(o)
    o_ref[...] = jnp.concatenate(outs, axis=-1).astype(o_ref.dtype)


def _swa(proj, sinks, bias, *, batch, seq, q_blk, k_blk, v_blk):
    m = batch * seq
    nb = seq // SWA_BLOCK
    blk = SWA_BLOCK
    kw = SWA_KV_HEADS * SWA_HEAD_DIM
    qw = SWA_HEADS * SWA_HEAD_DIM

    def cur(cb):
        return lambda b, n, s: (b * nb + n, cb)

    def prev(cb):
        return lambda b, n, s: (b * nb + jnp.maximum(n - 1, 0), cb)

    grid_spec = pltpu.PrefetchScalarGridSpec(
        num_scalar_prefetch=1,
        grid=(batch, nb),
        in_specs=[
            pl.BlockSpec((blk, qw), cur(q_blk)),
            pl.BlockSpec((blk, kw), prev(k_blk)),
            pl.BlockSpec((blk, kw), cur(k_blk)),
            pl.BlockSpec((blk, kw), prev(v_blk)),
            pl.BlockSpec((blk, kw), cur(v_blk)),
            pl.BlockSpec((SWA_HEADS, blk, 2 * blk), lambda b, n, s: (0, 0, 0)),
        ],
        out_specs=pl.BlockSpec((blk, qw), lambda b, n, s: (b * nb + n, 0)),
    )
    return pl.pallas_call(
        _swa_kernel,
        grid_spec=grid_spec,
        out_shape=jax.ShapeDtypeStruct((m, qw), BF16),
        compiler_params=_cparams(("parallel", "arbitrary")),
        name="swa",
    )(sinks, proj, proj, proj, proj, proj, bias)


def _t5_bias_table(t5_table):
    t_loc = np.arange(SWA_BLOCK, dtype=np.int32)[:, None]
    s_loc = np.arange(2 * SWA_BLOCK, dtype=np.int32)[None, :]
    dist = t_loc + SWA_BLOCK - s_loc
    n = np.maximum(dist, 0)
    nf = np.maximum(n, 1).astype(np.float32)
    large = T5_MAX_EXACT + (np.log(nf / np.float32(T5_MAX_EXACT)) / np.float32(math.log(T5_MAX_DIST / T5_MAX_EXACT))
                            * np.float32(T5_BUCKETS - T5_MAX_EXACT)).astype(np.int32)
    large = np.minimum(large, T5_BUCKETS - 1)
    bucket = np.where(n < T5_MAX_EXACT, n, large).astype(np.int32)
    in_band = (dist >= 0) & (dist < SWA_WINDOW)
    bias = t5_table.astype(F32)[jnp.asarray(bucket)]
    bias = jnp.where(jnp.asarray(in_band)[..., None], bias, NEG_BIG)
    return bias.transpose(2, 0, 1)


def _merge_kernel(x_ref, oa_ref, ob_ref, ga_ref, gb_ref, bg_ref, wa_ref, wb_ref, wm_ref, o_ref):
    a = _dot(oa_ref[...], wa_ref[...])
    b = _dot(ob_ref[...], wb_ref[...])
    bg = bg_ref[...]
    za = ga_ref[...].astype(F32) + bg[0:1]
    zb = gb_ref[...].astype(F32) + bg[1:2]
    merged = a / (1.0 + jnp.exp(-za)) + b / (1.0 + jnp.exp(-zb))
    o_ref[...] = x_ref[...] + _dot(merged.astype(BF16), wm_ref[...])


def _merge(x, o_a, o_b, proj, b_gates, wa, wb, wm, *, tm, ga_blk, gb_blk):
    m, d = x.shape
    tm = min(tm, m)
    row = lambda i: (i, 0)
    full = lambda i: (0, 0)
    return pl.pallas_call(
        _merge_kernel,
        grid=(m // tm,),
        in_specs=[
            pl.BlockSpec((tm, d), row),
            pl.BlockSpec((tm, d), row),
            pl.BlockSpec((tm, d), row),
            pl.BlockSpec((tm, d), lambda i: (i, ga_blk)),
            pl.BlockSpec((tm, d), lambda i: (i, gb_blk)),
            pl.BlockSpec((2, d), full),
            pl.BlockSpec((d, d), full),
            pl.BlockSpec((d, d), full),
            pl.BlockSpec((d, d), full),
        ],
        out_specs=pl.BlockSpec((tm, d), row),
        out_shape=jax.ShapeDtypeStruct((m, d), F32),
        compiler_params=_cparams(("parallel",)),
        name="merge",
    )(x, o_a, o_b, proj, proj, b_gates, wa, wb, wm)


def _xa_kernel(x_ref, gxa_ref, wq_ref, km_ref, vm_ref, wo_ref, gffn_ref, wr_ref, br_ref,
               x2_ref, hn_ref, idx_ref, gate_ref):
    x1 = x_ref[...]
    hx = _rms(x1, gxa_ref[...]).astype(BF16)
    q = _dot(hx, wq_ref[...]).astype(BF16)
    scale = XA_HEAD_DIM ** -0.5
    outs = []
    for h in range(XA_HEADS):
        sl = slice(h * XA_HEAD_DIM, (h + 1) * XA_HEAD_DIM)
        s = _dot_nt(q[:, sl], km_ref[:, sl]) * scale
        mx = jnp.max(s, axis=-1, keepdims=True)
        p = jnp.exp(s - mx)
        den = jnp.sum(p, axis=-1, keepdims=True)
        outs.append((_dot(p.astype(BF16), vm_ref[:, sl]) / den).astype(BF16))
    o = jnp.concatenate(outs, axis=-1)
    x2 = x1 + _dot(o, wo_ref[...])
    x2_ref[...] = x2
    hn = _rms(x2, gffn_ref[...])
    hn_ref[...] = hn
    hn_hi = hn.astype(BF16)
    hn_lo = (hn - hn_hi.astype(F32)).astype(BF16)
    big = _dot_nt(wr_ref[...], hn_hi)
    l = big[:N_EXPERTS] + big[N_EXPERTS:] + _dot_nt(wr_ref[:N_EXPERTS, :], hn_lo) + br_ref[...]
    eid = lax.broadcasted_iota(jnp.int32, l.shape, 0)
    vals = []
    idxs = []
    for k in range(TOP_K):
        mk = jnp.max(l, axis=0, keepdims=True)
        ik = jnp.min(jnp.where(l == mk, eid, N_EXPERTS), axis=0, keepdims=True)
        vals.append(mk)
        idxs.append(ik)
        l = jnp.where(eid == ik, -jnp.inf, l)
    ev = [jnp.exp(v - vals[0]) for v in vals]
    inv_den = 1.0 / (ev[0] + ev[1] + ev[2] + ev[3])
    gate_ref[...] = jnp.concatenate([e_k * inv_den for e_k in ev], axis=0)
    idx_ref[...] = jnp.concatenate(idxs, axis=0)


def _xa(x1, gxa, wq, kvmem, wo, gffn, wr, br, *, batch, seq, mem_len, tm):
    m, d = x1.shape
    tm = min(tm, seq)
    nt = seq // tm
    row = lambda i: (i, 0)
    full = lambda i: (0, 0)
    return pl.pallas_call(
        _xa_kernel,
        grid=(m // tm,),
        in_specs=[
            pl.BlockSpec((tm, d), row),
            pl.BlockSpec((1, d), full),
            pl.BlockSpec((d, d), full),
            pl.BlockSpec((mem_len, d), lambda i: (i // nt, 0)),
            pl.BlockSpec((mem_len, d), lambda i: (i // nt, 1)),
            pl.BlockSpec((d, d), full),
            pl.BlockSpec((1, d), full),
            pl.BlockSpec((2 * N_EXPERTS, d), full),
            pl.BlockSpec((N_EXPERTS, 1), full),
        ],
        out_specs=[
            pl.BlockSpec((tm, d), row),
            pl.BlockSpec((tm, d), row),
            pl.BlockSpec((TOP_K, tm), lambda i: (0, i)),
            pl.BlockSpec((TOP_K, tm), lambda i: (0, i)),
        ],
        out_shape=[
            jax.ShapeDtypeStruct((m, d), F32),
            jax.ShapeDtypeStruct((m, d), F32),
            jax.ShapeDtypeStruct((TOP_K, m), jnp.int32),
            jax.ShapeDtypeStruct((TOP_K, m), F32),
        ],
        compiler_params=_cparams(("parallel",)),
        name="xattn_router",
    )(x1, gxa, wq, kvmem, kvmem, wo, gffn, wr, br)


def _moe_kernel(bexp_ref, inv_hbm, hn_hbm, wgu_ref, bgu_ref, wd_ref, bd_ref, y_hbm,
                idx_smem, xbuf, ybuf, wgu_bf, wd_bf, idx_sem, g_sem, s_sem, *, n_tok, n_blocks):
    i = pl.program_id(0)
    blk = MOE_BLOCK

    @pl.when((i == 0) | (bexp_ref[i] != bexp_ref[jnp.maximum(i - 1, 0)]))
    def _():
        wgu_bf[...] = wgu_ref[0].astype(BF16)
        wd_bf[...] = wd_ref[0].astype(BF16)

    last = n_blocks - 1
    nslot = idx_smem.shape[0]

    def idx_copy(b, slot):
        return pltpu.make_async_copy(inv_hbm.at[jnp.minimum(b, last)], idx_smem.at[slot], idx_sem.at[slot])

    def gather_copy(slot, buf, j):
        tok = idx_smem[slot, 0, j]
        return pltpu.make_async_copy(hn_hbm.at[pl.ds(tok, 1)], xbuf.at[buf, pl.ds(j, 1)], g_sem)

    def scatter_copy(slot, buf, j):
        dst = idx_smem[slot, 1, j]
        return pltpu.make_async_copy(ybuf.at[buf, pl.ds(j, 1)], y_hbm.at[pl.ds(dst, 1)], s_sem)

    def wait_gathers(buf):
        pltpu.make_async_copy(hn_hbm.at[pl.ds(0, blk)], xbuf.at[buf], g_sem).wait()

    def wait_scatters(buf):
        pltpu.make_async_copy(ybuf.at[buf], y_hbm.at[pl.ds(0, blk)], s_sem).wait()

    @pl.when(i == 0)
    def _():
        ybuf[1] = jnp.zeros(ybuf.shape[1:], F32)
        for half in range(2):
            spare = pltpu.make_async_copy(ybuf.at[1], y_hbm.at[pl.ds(TOP_K * n_tok + half * blk, blk)], s_sem)
            spare.start()
            spare.wait()

        def fill(j, carry):
            idx_smem[nslot - 1, 1, j] = TOP_K * n_tok + blk + j
            return carry
        lax.fori_loop(0, blk, fill, 0)
        idx_copy(0, 0).start()
        idx_copy(0, 0).wait()

        def first(j, carry):
            gather_copy(0, 0, j).start()
            return carry
        lax.fori_loop(0, blk, first, 0)
        idx_copy(1, 1).start()

    def step(cur):
        nxt = 1 - cur
        idx_copy(i + 1, (i + 1) % nslot).wait()
        wait_gathers(cur)

        @pl.when(i >= 1)
        def _():
            wait_scatters(cur)

        idx_copy(i + 2, (i + 2) % nslot).start()
        s_slot = (i + nslot - 1) % nslot
        g_slot = (i + 1) % nslot
        for j in range(blk):
            scatter_copy(s_slot, nxt, j).start(priority=j % MOE_DMA_THREADS)
            gather_copy(g_slot, nxt, j).start(priority=j % MOE_DMA_THREADS)

        xb = xbuf[cur].astype(BF16)
        gu = _dot(xb, wgu_bf[...]) + bgu_ref[0]
        x_glu = jnp.minimum(gu[:, :D_FF], SWIGLU_LIMIT)
        x_lin = jnp.clip(gu[:, D_FF:], -SWIGLU_LIMIT, SWIGLU_LIMIT)
        act = x_glu / (1.0 + jnp.exp(-SWIGLU_ALPHA * x_glu)) * (x_lin + 1.0)
        ybuf[cur] = _dot(act.astype(BF16), wd_bf[...]) + bd_ref[0]

        @pl.when(i == last)
        def _():
            idx_copy(i + 2, (i + 2) % nslot).wait()
            wait_gathers(nxt)
            wait_scatters(nxt)

            def tail(j, carry):
                scatter_copy(i % nslot, cur, j).start()
                return carry
            lax.fori_loop(0, blk, tail, 0)
            wait_scatters(cur)

    @pl.when(i % 2 == 0)
    def _():
        step(0)

    @pl.when(i % 2 == 1)
    def _():
        step(1)


def _moe(block_expert, inv, hn, wgu, bgu, wd, bd, *, n_tok):
    n_blocks = inv.shape[0]
    d = D_MODEL
    grid_spec = pltpu.PrefetchScalarGridSpec(
        num_scalar_prefetch=1,
        grid=(n_blocks,),
        in_specs=[
            pl.BlockSpec(memory_space=pl.ANY),
            pl.BlockSpec(memory_space=pl.ANY),
            pl.BlockSpec((1, d, 2 * D_FF), lambda i, be: (be[i], 0, 0)),
            pl.BlockSpec((1, 1, 2 * D_FF), lambda i, be: (be[i], 0, 0)),
            pl.BlockSpec((1, D_FF, d), lambda i, be: (be[i], 0, 0)),
            pl.BlockSpec((1, 1, d), lambda i, be: (be[i], 0, 0)),
        ],
        out_specs=pl.BlockSpec(memory_space=pl.ANY),
        scratch_shapes=[
            pltpu.SMEM((4, 2, MOE_BLOCK), jnp.int32),
            pltpu.VMEM((2, MOE_BLOCK, d), F32),
            pltpu.VMEM((2, MOE_BLOCK, d), F32),
            pltpu.VMEM((d, 2 * D_FF), BF16),
            pltpu.VMEM((D_FF, d), BF16),
            pltpu.SemaphoreType.DMA((4,)),
            pltpu.SemaphoreType.DMA,
            pltpu.SemaphoreType.DMA,
        ],
    )
    return pl.pallas_call(
        functools.partial(_moe_kernel, n_tok=n_tok, n_blocks=n_blocks),
        grid_spec=grid_spec,
        out_shape=jax.ShapeDtypeStruct((TOP_K * n_tok + 2 * MOE_BLOCK, d), F32),
        compiler_params=_cparams(("arbitrary",)),
        name="moe_experts",
    )(block_expert, inv, hn, wgu, bgu, wd, bd)


def _combine_kernel(x_ref, y0_ref, y1_ref, y2_ref, y3_ref, gate_ref, gain_ref, o_ref):
    g = gate_ref[...]
    acc = x_ref[...]
    for k, y_ref in enumerate((y0_ref, y1_ref, y2_ref, y3_ref)):
        acc = acc + g[:, k:k + 1] * y_ref[...]
    o_ref[...] = _rms(acc, gain_ref[...])


def _combine(x2, y, gates, gain, *, tm):
    m, d = x2.shape
    tm = min(tm, m)
    kb = m // tm
    row = lambda i: (i, 0)
    return pl.pallas_call(
        _combine_kernel,
        grid=(m // tm,),
        in_specs=[pl.BlockSpec((tm, d), row)]
        + [pl.BlockSpec((tm, d), functools.partial(lambda i, k: (k * kb + i, 0), k=k)) for k in range(TOP_K)]
        + [pl.BlockSpec((tm, TOP_K), row), pl.BlockSpec((1, d), lambda i: (0, 0))],
        out_specs=pl.BlockSpec((tm, d), row),
        out_shape=jax.ShapeDtypeStruct((m, d), F32),
        compiler_params=_cparams(("parallel",)),
        name="combine_norm",
    )(x2, y, y, y, y, gates, gain)


def _routing_tables(top_idx, n_tok):
    n_assign = n_tok * TOP_K
    e_flat = top_idx.reshape(-1)
    order = jnp.argsort(e_flat).astype(jnp.int32)
    experts = jnp.arange(N_EXPERTS, dtype=jnp.int32)
    counts = jnp.sum((e_flat[:, None] == experts[None, :]).astype(jnp.int32), axis=0)
    starts = jnp.cumsum(counts) - counts
    padded = (counts + MOE_BLOCK - 1) // MOE_BLOCK * MOE_BLOCK
    pad_ends = jnp.cumsum(padded)
    pad_starts = pad_ends - padded
    cap = -(-n_assign // MOE_BLOCK) * MOE_BLOCK + N_EXPERTS * MOE_BLOCK
    n_blocks = cap // MOE_BLOCK
    block_start = jnp.arange(n_blocks, dtype=jnp.int32) * MOE_BLOCK
    block_expert = jnp.minimum(
        jnp.sum((pad_ends[None, :] <= block_start[:, None]).astype(jnp.int32), axis=1), N_EXPERTS - 1)
    r = block_start[:, None] + jnp.arange(MOE_BLOCK, dtype=jnp.int32)[None, :]
    pos = r - pad_starts[block_expert][:, None]
    valid = pos < counts[block_expert][:, None]
    src = jnp.clip(starts[block_expert][:, None] + pos, 0, n_assign - 1)
    inv = order[src]
    tok = jnp.where(valid, inv % n_tok, 0)
    spare = TOP_K * n_tok + ((r // MOE_BLOCK) % 2) * MOE_BLOCK + r % MOE_BLOCK
    dst = jnp.where(valid, inv, spare)
    table = jnp.stack([tok, dst], axis=1)
    return table, block_expert.astype(jnp.int32)


def kernel(x, mem, t5_table, lb_logits, norm_mix, w_in, b_gates, hg_norm_gain, swa_sinks, w_branch_a, w_branch_b, w_mix_out, norm_xa, xa_w_q, xa_w_k, xa_w_v, xa_w_o, norm_ffn, w_router, b_router, w_gate_up, b_gate_up, w_down, b_down, norm_final):
    batch, seq, d = x.shape
    mem_len = mem.shape[1]
    m = batch * seq
    depth = norm_mix.shape[0]
    assert depth == 1, "the final norm is fused into the layer's last kernel"
    lower_bounds = jnp.cumsum(jax.nn.softmax(lb_logits.astype(F32), axis=0), axis=0)
    bias = _t5_bias_table(t5_table)
    xt = x.reshape(m, d)
    for l in range(depth):
        w = w_in[l]
        cuts = np.cumsum([0, 1024, 1024, 1024, 1024, 1024, 256, 256, 1024, 1024])
        seg = [w[:, cuts[k]:cuts[k + 1]] for k in range(9)]
        w_perm = jnp.concatenate(seg[0:5] + seg[7:9] + seg[5:7], axis=1).astype(BF16)
        proj = _norm_matmul(xt, norm_mix[l][None], w_perm, use_norm=True, tm=1024, tn=1536, out_dtype=BF16)
        o_a = _hgrn(proj, lower_bounds[l][None], hg_norm_gain[l][None], batch=batch, seq=seq, t_rows=4096)
        o_b = _swa(proj, swa_sinks[l], bias, batch=batch, seq=seq, q_blk=4, k_blk=28, v_blk=29)
        x1 = _merge(xt, o_a, o_b, proj, b_gates[l], w_branch_a[l].astype(BF16), w_branch_b[l].astype(BF16),
                    w_mix_out[l].astype(BF16), tm=512, ga_blk=5, gb_blk=6)
        w_kv = jnp.concatenate([xa_w_k[l], xa_w_v[l]], axis=1).astype(BF16)
        kvmem = _norm_matmul(mem.reshape(batch * mem_len, d), norm_xa[l][None], w_kv, use_norm=False,
                             tm=1024, tn=1024, out_dtype=BF16)
        wr_hi = w_router[l].astype(BF16)
        wr_lo = (w_router[l] - wr_hi.astype(F32)).astype(BF16)
        x2, hn, top_idx, gates = _xa(x1, norm_xa[l][None], xa_w_q[l].astype(BF16), kvmem, xa_w_o[l].astype(BF16),
                                     norm_ffn[l][None], jnp.concatenate([wr_hi, wr_lo], axis=1).T,
                                     b_router[l][:, None], batch=batch, seq=seq, mem_len=mem_len, tm=512)
        gates = gates.T
        inv, block_expert = _routing_tables(top_idx, m)
        y = _moe(block_expert, inv, hn, w_gate_up[l], b_gate_up[l][:, None, :],
                 w_down[l], b_down[l][:, None, :], n_tok=m)
        xt = _combine(x2, y, gates, norm_final[None], tm=256)
    return xt.reshape(batch, seq, d)
```

```python
import functools
import math

import jax
import jax.numpy as jnp
import numpy as np
from jax import lax
from jax.experimental import pallas as pl
from jax.experimental.pallas import tpu as pltpu
from jax.experimental.pallas import tpu_sc as plsc

F32 = jnp.float32
BF16 = jnp.bfloat16

RMS_EPS = 1e-5
D_MODEL = 1024
HG_HEADS = 8
HG_DK = 128
HG_DV = 128
HG_CHUNK = 64
HG_LEVELS = (32, 16, 8)
HG_DIAG = 8
HG_GROUP = 4
LOG2E = 1.4426950408889634
SWA_HEADS = 16
SWA_KV_HEADS = 4
SWA_GROUP = 4
SWA_HEAD_DIM = 64
SWA_WINDOW = 128
SWA_BLOCK = 128
T5_BUCKETS = 32
T5_MAX_EXACT = 16
T5_MAX_DIST = 128
XA_HEADS = 4
XA_HEAD_DIM = 256
N_EXPERTS = 32
TOP_K = 4
D_FF = 1024
SWIGLU_ALPHA = 1.702
SWIGLU_LIMIT = 7.0
MOE_BLOCK = 128
SC_CORES = 2
SC_WORKERS = 32
SC_CHUNK = 32
NEG_BIG = -1e30

VMEM_LIMIT = 56 * 1024 * 1024


def _cparams(sem):
    return pltpu.CompilerParams(dimension_semantics=sem, vmem_limit_bytes=VMEM_LIMIT)


def _rms(xf, gain):
    return xf * lax.rsqrt(jnp.mean(xf * xf, axis=-1, keepdims=True) + RMS_EPS) * gain


def _dot(a, b):
    return jnp.dot(a, b, preferred_element_type=F32)


def _dot_nt(a, b):
    return lax.dot_general(a, b, (((1,), (1,)), ((), ())), preferred_element_type=F32)


def _dot_tn(a, b):
    return lax.dot_general(a, b, (((0,), (0,)), ((), ())), preferred_element_type=F32)


def _norm_matmul_kernel(x_ref, g_ref, w_ref, o_ref, h_ref, *, use_norm):
    @pl.when(pl.program_id(1) == 0)
    def _():
        xf = x_ref[...].astype(F32)
        if use_norm:
            xf = _rms(xf, g_ref[...])
        h_ref[...] = xf.astype(BF16)

    o_ref[...] = _dot(h_ref[...], w_ref[...]).astype(o_ref.dtype)


def _norm_matmul(x, gain, w, *, use_norm, tm, tn, out_dtype):
    m, k = x.shape
    n = w.shape[1]
    tm = min(tm, m)
    tn = min(tn, n)
    return pl.pallas_call(
        functools.partial(_norm_matmul_kernel, use_norm=use_norm),
        grid=(m // tm, n // tn),
        in_specs=[
            pl.BlockSpec((tm, k), lambda i, j: (i, 0)),
            pl.BlockSpec((1, k), lambda i, j: (0, 0)),
            pl.BlockSpec((k, tn), lambda i, j: (0, j)),
        ],
        out_specs=pl.BlockSpec((tm, tn), lambda i, j: (i, j)),
        out_shape=jax.ShapeDtypeStruct((m, n), out_dtype),
        scratch_shapes=[pltpu.VMEM((tm, k), BF16)],
        compiler_params=_cparams(("parallel", "arbitrary")),
        name="norm_matmul",
    )(x, gain, w)


def _rows_bcast(ref, base, group, row, n):
    parts = [jnp.broadcast_to(ref[pl.ds(base + g0 + row, 1), :], (group, ref.shape[1]))
             for g0 in range(0, n, group)]
    return jnp.concatenate(parts, axis=0)


def _hgrn_kernel(q_ref, f_ref, i_ref, g_ref, lb_ref, gain_ref, o_ref,
                 st_ref, kk_ref, gc_ref, a_ref, qd_ref, kv_ref, dec_ref, *, t_rows):
    c = HG_CHUNK
    d = HG_DIAG
    grows = HG_GROUP * c
    ngroups = t_rows // grows

    @pl.when(pl.program_id(2) == 0)
    def _():
        st_ref[...] = jnp.zeros_like(st_ref)

    lb = lb_ref[...]
    gain = gain_ref[...]
    row = lax.broadcasted_iota(jnp.int32, (c, 1), 0)
    col = lax.broadcasted_iota(jnp.int32, (c, c), 1)
    cd = col - (row // d) * d
    cd = jnp.where(cd <= row % d, cd, -1)
    lvl_mask = []
    for b in HG_LEVELS:
        rb = lax.broadcasted_iota(jnp.int32, (c, c), 0) // b
        lvl_mask.append(((rb - col // b) * 2 + (col // b) % 2) == 2)

    def rows(ref_row0):
        return pl.ds(pl.multiple_of(ref_row0, c), c)

    def phase_a(gi, buf):
        for ch in range(HG_GROUP):
            src = rows(gi * grows + ch * c)
            dst = buf * grows + ch * c
            _hgrn_front(q_ref, f_ref, i_ref, lb, row, cd, lvl_mask, kk_ref, gc_ref, a_ref, qd_ref, kv_ref,
                        dec_ref, src, dst, buf * HG_GROUP + ch)

    def phase_b(gi, buf, st):
        sts = []
        for ch in range(HG_GROUP):
            sts.append(st.astype(BF16))
            slot = buf * HG_GROUP + ch
            st = st * dec_ref[pl.ds(slot, 1), :] + kv_ref[slot]
        for ch in range(HG_GROUP):
            src = rows(gi * grows + ch * c)
            dst = rows(buf * grows + ch * c)
            o = _dot_nt(qd_ref[dst, :], sts[ch]) + _dot(a_ref[dst, :], i_ref[src, :])
            on = _rms(o, gain)
            gv = g_ref[src, :].astype(F32)
            o_ref[src, :] = (on * gv * (1.0 / (1.0 + jnp.exp(-gv)))).astype(o_ref.dtype)
        return st

    phase_a(0, 0)

    def body(k, st):
        st = phase_b(k, k % 2, st)
        phase_a(k + 1, (k + 1) % 2)
        return st

    st = lax.fori_loop(0, ngroups - 1, body, st_ref[...])
    st_ref[...] = phase_b(ngroups - 1, (ngroups - 1) % 2, st)


def _hgrn_front(q_ref, f_ref, i_ref, lb, row, cd, lvl_mask, kk_ref, gc_ref, a_ref, qd_ref, kv_ref, dec_ref,
                sl, dst, slot):
    c = HG_CHUNK
    d = HG_DIAG
    ds = pl.ds(pl.multiple_of(dst, c), c)
    ff = f_ref[sl, :].astype(F32)
    e = jnp.exp(-jnp.abs(ff))
    r = 1.0 / (1.0 + e)
    sig_pos = jnp.where(ff >= 0, r, e * r)
    sig_neg = jnp.where(ff >= 0, e * r, r)
    logf = jnp.log(lb + (1.0 - lb) * sig_pos)
    kk = (1.0 - lb) * sig_neg
    qv = q_ref[sl, :].astype(F32)
    qf = qv * (1.0 / (1.0 + jnp.exp(-qv)))

    gcum = logf * LOG2E
    sh = 1
    while sh < c:
        gcum = gcum + jnp.where(row >= sh, pltpu.roll(gcum, sh, 0), 0.0)
        sh *= 2
    kk_ref[ds, :] = kk
    gc_ref[ds, :] = gcum
    g_last = gcum[c - 1:c, :]

    a = jnp.zeros((c, c), F32)
    for j in range(d):
        gj = _rows_bcast(gc_ref, dst, d, j, c)
        kj = _rows_bcast(kk_ref, dst, d, j, c)
        p = qf * jnp.exp2(gcum - gj) * kj
        a = jnp.where(cd == j, jnp.sum(p, axis=-1, keepdims=True), a)

    for li, b in enumerate(HG_LEVELS):
        gref = _rows_bcast(gc_ref, dst, 2 * b, b - 1, c)
        upper = (row % (2 * b)) >= b
        qa = jnp.where(upper, qf * jnp.exp2(gcum - gref), 0.0).astype(BF16)
        kb = jnp.where(upper, 0.0, kk * jnp.exp2(gref - gcum)).astype(BF16)
        a = a + jnp.where(lvl_mask[li], _dot_nt(qa, kb), 0.0)
    a_ref[ds, :] = a.astype(BF16)

    qd_ref[ds, :] = (qf * jnp.exp2(gcum)).astype(BF16)
    k_dec = (kk * jnp.exp2(g_last - gcum)).astype(BF16)
    kv_ref[slot] = _dot_tn(i_ref[sl, :], k_dec)
    dec_ref[pl.ds(slot, 1), :] = jnp.exp2(g_last)


def _hgrn(proj, lb, gain, *, batch, seq, t_rows):
    m = batch * seq
    t_rows = min(t_rows, seq)
    nt = seq // t_rows
    h = HG_HEADS
    grows = HG_GROUP * HG_CHUNK
    assert t_rows % grows == 0

    def spec(off):
        return pl.BlockSpec((t_rows, 128), lambda b, hh, t: (b * nt + t, off + hh))

    return pl.pallas_call(
        functools.partial(_hgrn_kernel, t_rows=t_rows),
        grid=(batch, h, nt),
        in_specs=[spec(0), spec(h), spec(2 * h), spec(3 * h),
                  pl.BlockSpec((1, 128), lambda b, hh, t: (0, hh)),
                  pl.BlockSpec((1, 128), lambda b, hh, t: (0, 0))],
        out_specs=pl.BlockSpec((t_rows, 128), lambda b, hh, t: (b * nt + t, hh)),
        out_shape=jax.ShapeDtypeStruct((m, h * HG_DV), BF16),
        scratch_shapes=[
            pltpu.VMEM((HG_DV, HG_DK), F32),
            pltpu.VMEM((2 * grows, 128), F32),
            pltpu.VMEM((2 * grows, 128), F32),
            pltpu.VMEM((2 * grows, HG_CHUNK), BF16),
            pltpu.VMEM((2 * grows, 128), BF16),
            pltpu.VMEM((2 * HG_GROUP, HG_DV, HG_DK), F32),
            pltpu.VMEM((2 * HG_GROUP, 128), F32),
        ],
        compiler_params=_cparams(("parallel", "parallel", "arbitrary")),
        name="hgrn2",
    )(proj, proj, proj, proj, lb, gain)


def _swa_kernel(sink_ref, q_ref, kp_ref, kc_ref, vp_ref, vc_ref, bias_ref, o_ref):
    blk = SWA_BLOCK
    dh = SWA_HEAD_DIM
    first = pl.program_id(1) == 0
    colk = lax.broadcasted_iota(jnp.int32, (blk, 2 * blk), 1)
    no_prev = first & (colk < blk)
    scale = dh ** -0.5
    outs = []
    for h in range(SWA_KV_HEADS):
        ks = slice(h * dh, (h + 1) * dh)
        kcat = jnp.concatenate([kp_ref[:, ks], kc_ref[:, ks]], axis=0)
        vcat = jnp.concatenate([vp_ref[:, ks], vc_ref[:, ks]], axis=0)
        for g in range(SWA_GROUP):
            hq = h * SWA_GROUP + g
            q = q_ref[:, hq * dh:(hq + 1) * dh]
            s = _dot_nt(q, kcat) * scale + bias_ref[hq]
            s = jnp.where(no_prev, NEG_BIG, s)
            sink = sink_ref[hq]
            mx = jnp.maximum(jnp.max(s, axis=-1, keepdims=True), sink)
            p = jnp.exp(s - mx)
            den = jnp.sum(p, axis=-1, keepdims=True) + jnp.exp(sink - mx)
            o = _dot(p.astype(BF16), vcat) / den
            outs.append(o)
    o_ref[...] = jnp.concatenate(outs, axis=-1).astype(o_ref.dtype)


def _swa(proj, sinks, bias, *, batch, seq, q_blk, k_blk, v_blk):
    m = batch * seq
    nb = seq // SWA_BLOCK
    blk = SWA_BLOCK
    kw = SWA_KV_HEADS * SWA_HEAD_DIM
    qw = SWA_HEADS * SWA_HEAD_DIM

    def cur(cb):
        return lambda b, n, s: (b * nb + n, cb)

    def prev(cb):
        return lambda b, n, s: (b * nb + jnp.maximum(n - 1, 0), cb)

    grid_spec = pltpu.PrefetchScalarGridSpec(
        num_scalar_prefetch=1,
        grid=(batch, nb),
        in_specs=[
            pl.BlockSpec((blk, qw), cur(q_blk)),
            pl.BlockSpec((blk, kw), prev(k_blk)),
            pl.BlockSpec((blk, kw), cur(k_blk)),
            pl.BlockSpec((blk, kw), prev(v_blk)),
            pl.BlockSpec((blk, kw), cur(v_blk)),
            pl.BlockSpec((SWA_HEADS, blk, 2 * blk), lambda b, n, s: (0, 0, 0)),
        ],
        out_specs=pl.BlockSpec((blk, qw), lambda b, n, s: (b * nb + n, 0)),
    )
    return pl.pallas_call(
        _swa_kernel,
        grid_spec=grid_spec,
        out_shape=jax.ShapeDtypeStruct((m, qw), BF16),
        compiler_params=_cparams(("parallel", "arbitrary")),
        name="swa",
    )(sinks, proj, proj, proj, proj, proj, bias)


def _t5_bias_table(t5_table):
    t_loc = np.arange(SWA_BLOCK, dtype=np.int32)[:, None]
    s_loc = np.arange(2 * SWA_BLOCK, dtype=np.int32)[None, :]
    dist = t_loc + SWA_BLOCK - s_loc
    n = np.maximum(dist, 0)
    nf = np.maximum(n, 1).astype(np.float32)
    large = T5_MAX_EXACT + (np.log(nf / np.float32(T5_MAX_EXACT)) / np.float32(math.log(T5_MAX_DIST / T5_MAX_EXACT))
                            * np.float32(T5_BUCKETS - T5_MAX_EXACT)).astype(np.int32)
    large = np.minimum(large, T5_BUCKETS - 1)
    bucket = np.where(n < T5_MAX_EXACT, n, large).astype(np.int32)
    in_band = (dist >= 0) & (dist < SWA_WINDOW)
    bias = t5_table.astype(F32)[jnp.asarray(bucket)]
    bias = jnp.where(jnp.asarray(in_band)[..., None], bias, NEG_BIG)
    return bias.transpose(2, 0, 1)


def _merge_kernel(x_ref, oa_ref, ob_ref, ga_ref, gb_ref, bg_ref, wa_ref, wb_ref, wm_ref, o_ref):
    a = _dot(oa_ref[...], wa_ref[...])
    b = _dot(ob_ref[...], wb_ref[...])
    bg = bg_ref[...]
    za = ga_ref[...].astype(F32) + bg[0:1]
    zb = gb_ref[...].astype(F32) + bg[1:2]
    merged = a / (1.0 + jnp.exp(-za)) + b / (1.0 + jnp.exp(-zb))
    o_ref[...] = x_ref[...] + _dot(merged.astype(BF16), wm_ref[...])


def _merge(x, o_a, o_b, proj, b_gates, wa, wb, wm, *, tm, ga_blk, gb_blk):
    m, d = x.shape
    tm = min(tm, m)
    row = lambda i: (i, 0)
    full = lambda i: (0, 0)
    return pl.pallas_call(
        _merge_kernel,
        grid=(m // tm,),
        in_specs=[
            pl.BlockSpec((tm, d), row),
            pl.BlockSpec((tm, d), row),
            pl.BlockSpec((tm, d), row),
            pl.BlockSpec((tm, d), lambda i: (i, ga_blk)),
            pl.BlockSpec((tm, d), lambda i: (i, gb_blk)),
            pl.BlockSpec((2, d), full),
            pl.BlockSpec((d, d), full),
            pl.BlockSpec((d, d), full),
            pl.BlockSpec((d, d), full),
        ],
        out_specs=pl.BlockSpec((tm, d), row),
        out_shape=jax.ShapeDtypeStruct((m, d), F32),
        compiler_params=_cparams(("parallel",)),
        name="merge",
    )(x, o_a, o_b, proj, proj, b_gates, wa, wb, wm)


def _xa_kernel(x_ref, gxa_ref, wq_ref, km_ref, vm_ref, wo_ref, gffn_ref, wr_ref, br_ref,
               x2_ref, hn_ref, idx_ref, gate_ref):
    x1 = x_ref[...]
    hx = _rms(x1, gxa_ref[...]).astype(BF16)
    q = _dot(hx, wq_ref[...]).astype(BF16)
    scale = XA_HEAD_DIM ** -0.5
    outs = []
    for h in range(XA_HEADS):
        sl = slice(h * XA_HEAD_DIM, (h + 1) * XA_HEAD_DIM)
        s = _dot_nt(q[:, sl], km_ref[:, sl]) * scale
        mx = jnp.max(s, axis=-1, keepdims=True)
        p = jnp.exp(s - mx)
        den = jnp.sum(p, axis=-1, keepdims=True)
        outs.append((_dot(p.astype(BF16), vm_ref[:, sl]) / den).astype(BF16))
    o = jnp.concatenate(outs, axis=-1)
    x2 = x1 + _dot(o, wo_ref[...])
    x2_ref[...] = x2
    hn = _rms(x2, gffn_ref[...])
    hn_ref[...] = hn
    hn_hi = hn.astype(BF16)
    hn_lo = (hn - hn_hi.astype(F32)).astype(BF16)
    big = _dot_nt(wr_ref[...], hn_hi)
    l = big[:N_EXPERTS] + big[N_EXPERTS:] + _dot_nt(wr_ref[:N_EXPERTS, :], hn_lo) + br_ref[...]
    eid = lax.broadcasted_iota(jnp.int32, l.shape, 0)
    vals = []
    idxs = []
    for k in range(TOP_K):
        mk = jnp.max(l, axis=0, keepdims=True)
        ik = jnp.min(jnp.where(l == mk, eid, N_EXPERTS), axis=0, keepdims=True)
        vals.append(mk)
        idxs.append(ik)
        l = jnp.where(eid == ik, -jnp.inf, l)
    ev = [jnp.exp(v - vals[0]) for v in vals]
    inv_den = 1.0 / (ev[0] + ev[1] + ev[2] + ev[3])
    gate_ref[...] = jnp.concatenate([e_k * inv_den for e_k in ev], axis=0)
    idx_ref[...] = jnp.concatenate(idxs, axis=0)


def _xa(x1, gxa, wq, kvmem, wo, gffn, wr, br, *, batch, seq, mem_len, tm):
    m, d = x1.shape
    tm = min(tm, seq)
    nt = seq // tm
    row = lambda i: (i, 0)
    full = lambda i: (0, 0)
    return pl.pallas_call(
        _xa_kernel,
        grid=(m // tm,),
        in_specs=[
            pl.BlockSpec((tm, d), row),
            pl.BlockSpec((1, d), full),
            pl.BlockSpec((d, d), full),
            pl.BlockSpec((mem_len, d), lambda i: (i // nt, 0)),
            pl.BlockSpec((mem_len, d), lambda i: (i // nt, 1)),
            pl.BlockSpec((d, d), full),
            pl.BlockSpec((1, d), full),
            pl.BlockSpec((2 * N_EXPERTS, d), full),
            pl.BlockSpec((N_EXPERTS, 1), full),
        ],
        out_specs=[
            pl.BlockSpec((tm, d), row),
            pl.BlockSpec((tm, d), row),
            pl.BlockSpec((TOP_K, tm), lambda i: (0, i)),
            pl.BlockSpec((TOP_K, tm), lambda i: (0, i)),
        ],
        out_shape=[
            jax.ShapeDtypeStruct((m, d), F32),
            jax.ShapeDtypeStruct((m, d), F32),
            jax.ShapeDtypeStruct((TOP_K, m), jnp.int32),
            jax.ShapeDtypeStruct((TOP_K, m), F32),
        ],
        compiler_params=_cparams(("parallel",)),
        name="xattn_router",
    )(x1, gxa, wq, kvmem, kvmem, wo, gffn, wr, br)


def _moe_kernel(bexp_ref, x_ref, wgu_ref, bgu_ref, wd_ref, bd_ref, y_ref, wgu_bf, wd_bf):
    i = pl.program_id(0)

    @pl.when((i == 0) | (bexp_ref[i] != bexp_ref[jnp.maximum(i - 1, 0)]))
    def _():
        wgu_bf[...] = wgu_ref[0].astype(BF16)
        wd_bf[...] = wd_ref[0].astype(BF16)

    xb = x_ref[...].astype(BF16)
    gu = _dot(xb, wgu_bf[...]) + bgu_ref[0]
    x_glu = jnp.minimum(gu[:, :D_FF], SWIGLU_LIMIT)
    x_lin = jnp.clip(gu[:, D_FF:], -SWIGLU_LIMIT, SWIGLU_LIMIT)
    act = x_glu / (1.0 + jnp.exp(-SWIGLU_ALPHA * x_glu)) * (x_lin + 1.0)
    y_ref[...] = _dot(act.astype(BF16), wd_bf[...]) + bd_ref[0]


def _moe(block_expert, x_sorted, wgu, bgu, wd, bd):
    rows, d = x_sorted.shape
    n_blocks = rows // MOE_BLOCK
    grid_spec = pltpu.PrefetchScalarGridSpec(
        num_scalar_prefetch=1,
        grid=(n_blocks,),
        in_specs=[
            pl.BlockSpec((MOE_BLOCK, d), lambda i, be: (i, 0)),
            pl.BlockSpec((1, d, 2 * D_FF), lambda i, be: (be[i], 0, 0)),
            pl.BlockSpec((1, 1, 2 * D_FF), lambda i, be: (be[i], 0, 0)),
            pl.BlockSpec((1, D_FF, d), lambda i, be: (be[i], 0, 0)),
            pl.BlockSpec((1, 1, d), lambda i, be: (be[i], 0, 0)),
        ],
        out_specs=pl.BlockSpec((MOE_BLOCK, d), lambda i, be: (i, 0)),
        scratch_shapes=[
            pltpu.VMEM((d, 2 * D_FF), BF16),
            pltpu.VMEM((D_FF, d), BF16),
        ],
    )
    return pl.pallas_call(
        _moe_kernel,
        grid_spec=grid_spec,
        out_shape=jax.ShapeDtypeStruct((rows, d), F32),
        compiler_params=_cparams(("arbitrary",)),
        name="moe_experts",
    )(block_expert, x_sorted, wgu, bgu, wd, bd)


def _sc_gather(table, idx):
    b = idx.shape[0]
    d = table.shape[1]
    per_worker = b // SC_WORKERS
    n_chunks = per_worker // SC_CHUNK
    assert b % SC_WORKERS == 0 and per_worker % (2 * SC_CHUNK) == 0, (b, SC_WORKERS, SC_CHUNK)
    mesh = plsc.VectorSubcoreMesh(core_axis_name="c", subcore_axis_name="s")

    @functools.partial(
        pl.kernel, mesh=mesh,
        out_type=jax.ShapeDtypeStruct((b, d), table.dtype),
        scratch_types=[pltpu.VMEM((per_worker,), jnp.int32),
                       pltpu.VMEM((2, SC_CHUNK, d), table.dtype),
                       pltpu.SemaphoreType.DMA((2,)),
                       pltpu.SemaphoreType.DMA((2,))],
        name="sc_row_gather",
    )
    def gather_kernel(table_hbm, idx_hbm, out_hbm, idx_v, rows_v, g_sem, w_sem):
        worker = lax.axis_index("s") * SC_CORES + lax.axis_index("c")
        base = worker * per_worker
        pltpu.sync_copy(idx_hbm.at[pl.ds(base, per_worker)], idx_v)

        def gather(t, slot):
            return pltpu.make_async_copy(table_hbm.at[idx_v.at[pl.ds(t * SC_CHUNK, SC_CHUNK)]],
                                         rows_v.at[slot], g_sem.at[slot])

        def write(t, slot):
            return pltpu.make_async_copy(rows_v.at[slot], out_hbm.at[pl.ds(base + t * SC_CHUNK, SC_CHUNK)],
                                         w_sem.at[slot])

        gather(0, 0).start()

        def body(t2, carry):
            for slot in range(2):
                t = t2 * 2 + slot
                gather(t, slot).wait()

                @pl.when(t + 1 < n_chunks)
                def _():
                    @pl.when(t >= 1)
                    def _():
                        write(t - 1, 1 - slot).wait()
                    gather(t + 1, 1 - slot).start()
                write(t, slot).start()
            return carry
        lax.fori_loop(0, n_chunks // 2, body, 0)
        write(n_chunks - 2, 0).wait()
        write(n_chunks - 1, 1).wait()

    return gather_kernel(table, idx)


def _combine_kernel(x_ref, y0_ref, y1_ref, y2_ref, y3_ref, gate_ref, gain_ref, o_ref):
    g = gate_ref[...]
    acc = x_ref[...]
    for k, y_ref in enumerate((y0_ref, y1_ref, y2_ref, y3_ref)):
        acc = acc + g[:, k:k + 1] * y_ref[...]
    o_ref[...] = _rms(acc, gain_ref[...])


def _combine(x2, y, gates, gain, *, tm):
    m, d = x2.shape
    tm = min(tm, m)
    kb = m // tm
    row = lambda i: (i, 0)
    return pl.pallas_call(
        _combine_kernel,
        grid=(m // tm,),
        in_specs=[pl.BlockSpec((tm, d), row)]
        + [pl.BlockSpec((tm, d), functools.partial(lambda i, k: (k * kb + i, 0), k=k)) for k in range(TOP_K)]
        + [pl.BlockSpec((tm, TOP_K), row), pl.BlockSpec((1, d), lambda i: (0, 0))],
        out_specs=pl.BlockSpec((tm, d), row),
        out_shape=jax.ShapeDtypeStruct((m, d), F32),
        compiler_params=_cparams(("parallel",)),
        name="combine_norm",
    )(x2, y, y, y, y, gates, gain)


def _routing_tables(top_idx, n_tok):
    n_assign = n_tok * TOP_K
    e_flat = top_idx.reshape(-1)
    order = jnp.argsort(e_flat).astype(jnp.int32)
    experts = jnp.arange(N_EXPERTS, dtype=jnp.int32)
    counts = jnp.sum((e_flat[:, None] == experts[None, :]).astype(jnp.int32), axis=0)
    starts = jnp.cumsum(counts) - counts
    padded = (counts + MOE_BLOCK - 1) // MOE_BLOCK * MOE_BLOCK
    pad_ends = jnp.cumsum(padded)
    pad_starts = pad_ends - padded
    cap = -(-n_assign // MOE_BLOCK) * MOE_BLOCK + N_EXPERTS * MOE_BLOCK
    n_blocks = cap // MOE_BLOCK
    block_start = jnp.arange(n_blocks, dtype=jnp.int32) * MOE_BLOCK
    block_expert = jnp.minimum(
        jnp.sum((pad_ends[None, :] <= block_start[:, None]).astype(jnp.int32), axis=1), N_EXPERTS - 1)
    r = block_start[:, None] + jnp.arange(MOE_BLOCK, dtype=jnp.int32)[None, :]
    pos = r - pad_starts[block_expert][:, None]
    valid = pos < counts[block_expert][:, None]
    src = jnp.clip(starts[block_expert][:, None] + pos, 0, n_assign - 1)
    tok = jnp.where(valid, order[src] % n_tok, 0).reshape(-1)
    sorted_pos = jnp.argsort(order).astype(jnp.int32)
    row_of = sorted_pos + (pad_starts - starts)[e_flat]
    return tok, row_of, block_expert.astype(jnp.int32)


def kernel(x, mem, t5_table, lb_logits, norm_mix, w_in, b_gates, hg_norm_gain, swa_sinks, w_branch_a, w_branch_b, w_mix_out, norm_xa, xa_w_q, xa_w_k, xa_w_v, xa_w_o, norm_ffn, w_router, b_router, w_gate_up, b_gate_up, w_down, b_down, norm_final):
    batch, seq, d = x.shape
    mem_len = mem.shape[1]
    m = batch * seq
    depth = norm_mix.shape[0]
    assert depth == 1, "the final norm is fused into the layer's last kernel"
    lower_bounds = jnp.cumsum(jax.nn.softmax(lb_logits.astype(F32), axis=0), axis=0)
    bias = _t5_bias_table(t5_table)
    xt = x.reshape(m, d)
    for l in range(depth):
        w = w_in[l]
        cuts = np.cumsum([0, 1024, 1024, 1024, 1024, 1024, 256, 256, 1024, 1024])
        seg = [w[:, cuts[k]:cuts[k + 1]] for k in range(9)]
        w_perm = jnp.concatenate(seg[0:5] + seg[7:9] + seg[5:7], axis=1).astype(BF16)
        proj = _norm_matmul(xt, norm_mix[l][None], w_perm, use_norm=True, tm=1024, tn=1536, out_dtype=BF16)
        o_a = _hgrn(proj, lower_bounds[l][None], hg_norm_gain[l][None], batch=batch, seq=seq, t_rows=4096)
        o_b = _swa(proj, swa_sinks[l], bias, batch=batch, seq=seq, q_blk=4, k_blk=28, v_blk=29)
        x1 = _merge(xt, o_a, o_b, proj, b_gates[l], w_branch_a[l].astype(BF16), w_branch_b[l].astype(BF16),
                    w_mix_out[l].astype(BF16), tm=512, ga_blk=5, gb_blk=6)
        w_kv = jnp.concatenate([xa_w_k[l], xa_w_v[l]], axis=1).astype(BF16)
        kvmem = _norm_matmul(mem.reshape(batch * mem_len, d), norm_xa[l][None], w_kv, use_norm=False,
                             tm=1024, tn=1024, out_dtype=BF16)
        wr_hi = w_router[l].astype(BF16)
        wr_lo = (w_router[l] - wr_hi.astype(F32)).astype(BF16)
        x2, hn, top_idx, gates = _xa(x1, norm_xa[l][None], xa_w_q[l].astype(BF16), kvmem, xa_w_o[l].astype(BF16),
                                     norm_ffn[l][None], jnp.concatenate([wr_hi, wr_lo], axis=1).T,
                                     b_router[l][:, None], batch=batch, seq=seq, mem_len=mem_len, tm=512)
        gates = gates.T
        tok, row_of, block_expert = _routing_tables(top_idx, m)
        x_sorted = _sc_gather(hn, tok)
        y_sorted = _moe(block_expert, x_sorted, w_gate_up[l], b_gate_up[l][:, None, :],
                        w_down[l], b_down[l][:, None, :])
        y = _sc_gather(y_sorted, row_of)
        xt = _combine(x2, y, gates, norm_final[None], tm=256)
    return xt.reshape(batch, seq, d)
```

```python
import functools
import math

import jax
import jax.numpy as jnp
import numpy as np
from jax import lax
from jax.experimental import pallas as pl
from jax.experimental.pallas import tpu as pltpu
from jax.experimental.pallas import tpu_sc as plsc

F32 = jnp.float32
BF16 = jnp.bfloat16

RMS_EPS = 1e-5
D_MODEL = 1024
HG_HEADS = 8
HG_DK = 128
HG_DV = 128
HG_CHUNK = 64
HG_LEVELS = (32, 16, 8)
HG_DIAG = 8
HG_GROUP = 4
LOG2E = 1.4426950408889634
SWA_HEADS = 16
SWA_KV_HEADS = 4
SWA_GROUP = 4
SWA_HEAD_DIM = 64
SWA_WINDOW = 128
SWA_BLOCK = 128
T5_BUCKETS = 32
T5_MAX_EXACT = 16
T5_MAX_DIST = 128
XA_HEADS = 4
XA_HEAD_DIM = 256
N_EXPERTS = 32
TOP_K = 4
D_FF = 1024
SWIGLU_ALPHA = 1.702
SWIGLU_LIMIT = 7.0
MOE_BLOCK = 512
SC_CORES = 2
SC_WORKERS = 32
SC_CHUNK = 64
HALF = D_MODEL // 2
NEG_BIG = -1e30

VMEM_LIMIT = 56 * 1024 * 1024


def _cparams(sem):
    return pltpu.CompilerParams(dimension_semantics=sem, vmem_limit_bytes=VMEM_LIMIT)


def _rms(xf, gain):
    return xf * lax.rsqrt(jnp.mean(xf * xf, axis=-1, keepdims=True) + RMS_EPS) * gain


def _dot(a, b):
    return jnp.dot(a, b, preferred_element_type=F32)


def _dot_nt(a, b):
    return lax.dot_general(a, b, (((1,), (1,)), ((), ())), preferred_element_type=F32)


def _dot_tn(a, b):
    return lax.dot_general(a, b, (((0,), (0,)), ((), ())), preferred_element_type=F32)


def _pack_rows(x):
    hi = lax.bitcast_convert_type(x[:, :HALF].astype(BF16).astype(F32), jnp.uint32)
    lo = lax.bitcast_convert_type(x[:, HALF:].astype(BF16).astype(F32), jnp.uint32)
    return hi | (lo >> 16)


def _unpack_rows(p):
    hi = lax.bitcast_convert_type(p & jnp.uint32(0xFFFF0000), F32)
    lo = lax.bitcast_convert_type(p << 16, F32)
    return jnp.concatenate([hi, lo], axis=-1)


def _norm_matmul_kernel(x_ref, g_ref, w_ref, o_ref, h_ref, *, use_norm):
    @pl.when(pl.program_id(1) == 0)
    def _():
        xf = x_ref[...].astype(F32)
        if use_norm:
            xf = _rms(xf, g_ref[...])
        h_ref[...] = xf.astype(BF16)

    o_ref[...] = _dot(h_ref[...], w_ref[...]).astype(o_ref.dtype)


def _norm_matmul(x, gain, w, *, use_norm, tm, tn, out_dtype):
    m, k = x.shape
    n = w.shape[1]
    tm = min(tm, m)
    tn = min(tn, n)
    return pl.pallas_call(
        functools.partial(_norm_matmul_kernel, use_norm=use_norm),
        grid=(m // tm, n // tn),
        in_specs=[
            pl.BlockSpec((tm, k), lambda i, j: (i, 0)),
            pl.BlockSpec((1, k), lambda i, j: (0, 0)),
            pl.BlockSpec((k, tn), lambda i, j: (0, j)),
        ],
        out_specs=pl.BlockSpec((tm, tn), lambda i, j: (i, j)),
        out_shape=jax.ShapeDtypeStruct((m, n), out_dtype),
        scratch_shapes=[pltpu.VMEM((tm, k), BF16)],
        compiler_params=_cparams(("parallel", "arbitrary")),
        name="norm_matmul",
    )(x, gain, w)


def _rows_bcast(ref, base, group, row, n):
    parts = [jnp.broadcast_to(ref[pl.ds(base + g0 + row, 1), :], (group, ref.shape[1]))
             for g0 in range(0, n, group)]
    return jnp.concatenate(parts, axis=0)


def _hgrn_kernel(q_ref, f_ref, i_ref, g_ref, lb_ref, gain_ref, o_ref,
                 st_ref, kk_ref, gc_ref, a_ref, qd_ref, kv_ref, dec_ref, *, t_rows):
    c = HG_CHUNK
    d = HG_DIAG
    grows = HG_GROUP * c
    ngroups = t_rows // grows

    @pl.when(pl.program_id(2) == 0)
    def _():
        st_ref[...] = jnp.zeros_like(st_ref)

    lb = lb_ref[...]
    gain = gain_ref[...]
    row = lax.broadcasted_iota(jnp.int32, (c, 1), 0)
    col = lax.broadcasted_iota(jnp.int32, (c, c), 1)
    cd = col - (row // d) * d
    cd = jnp.where(cd <= row % d, cd, -1)
    lvl_mask = []
    for b in HG_LEVELS:
        rb = lax.broadcasted_iota(jnp.int32, (c, c), 0) // b
        lvl_mask.append(((rb - col // b) * 2 + (col // b) % 2) == 2)

    def rows(ref_row0):
        return pl.ds(pl.multiple_of(ref_row0, c), c)

    def phase_a(gi, buf):
        for ch in range(HG_GROUP):
            src = rows(gi * grows + ch * c)
            dst = buf * grows + ch * c
            _hgrn_front(q_ref, f_ref, i_ref, lb, row, cd, lvl_mask, kk_ref, gc_ref, a_ref, qd_ref, kv_ref,
                        dec_ref, src, dst, buf * HG_GROUP + ch)

    def phase_b(gi, buf, st):
        sts = []
        for ch in range(HG_GROUP):
            sts.append(st.astype(BF16))
            slot = buf * HG_GROUP + ch
            st = st * dec_ref[pl.ds(slot, 1), :] + kv_ref[slot]
        for ch in range(HG_GROUP):
            src = rows(gi * grows + ch * c)
            dst = rows(buf * grows + ch * c)
            o = _dot_nt(qd_ref[dst, :], sts[ch]) + _dot(a_ref[dst, :], i_ref[src, :])
            on = _rms(o, gain)
            gv = g_ref[src, :].astype(F32)
            o_ref[src, :] = (on * gv * (1.0 / (1.0 + jnp.exp(-gv)))).astype(o_ref.dtype)
        return st

    phase_a(0, 0)

    def body(k, st):
        st = phase_b(k, k % 2, st)
        phase_a(k + 1, (k + 1) % 2)
        return st

    st = lax.fori_loop(0, ngroups - 1, body, st_ref[...])
    st_ref[...] = phase_b(ngroups - 1, (ngroups - 1) % 2, st)


def _hgrn_front(q_ref, f_ref, i_ref, lb, row, cd, lvl_mask, kk_ref, gc_ref, a_ref, qd_ref, kv_ref, dec_ref,
                sl, dst, slot):
    c = HG_CHUNK
    d = HG_DIAG
    ds = pl.ds(pl.multiple_of(dst, c), c)
    ff = f_ref[sl, :].astype(F32)
    e = jnp.exp(-jnp.abs(ff))
    r = 1.0 / (1.0 + e)
    sig_pos = jnp.where(ff >= 0, r, e * r)
    sig_neg = jnp.where(ff >= 0, e * r, r)
    logf = jnp.log(lb + (1.0 - lb) * sig_pos)
    kk = (1.0 - lb) * sig_neg
    qv = q_ref[sl, :].astype(F32)
    qf = qv * (1.0 / (1.0 + jnp.exp(-qv)))

    gcum = logf * LOG2E
    sh = 1
    while sh < c:
        gcum = gcum + jnp.where(row >= sh, pltpu.roll(gcum, sh, 0), 0.0)
        sh *= 2
    kk_ref[ds, :] = kk
    gc_ref[ds, :] = gcum
    g_last = gcum[c - 1:c, :]

    a = jnp.zeros((c, c), F32)
    for j in range(d):
        gj = _rows_bcast(gc_ref, dst, d, j, c)
        kj = _rows_bcast(kk_ref, dst, d, j, c)
        p = qf * jnp.exp2(gcum - gj) * kj
        a = jnp.where(cd == j, jnp.sum(p, axis=-1, keepdims=True), a)

    for li, b in enumerate(HG_LEVELS):
        gref = _rows_bcast(gc_ref, dst, 2 * b, b - 1, c)
        upper = (row % (2 * b)) >= b
        qa = jnp.where(upper, qf * jnp.exp2(gcum - gref), 0.0).astype(BF16)
        kb = jnp.where(upper, 0.0, kk * jnp.exp2(gref - gcum)).astype(BF16)
        a = a + jnp.where(lvl_mask[li], _dot_nt(qa, kb), 0.0)
    a_ref[ds, :] = a.astype(BF16)

    qd_ref[ds, :] = (qf * jnp.exp2(gcum)).astype(BF16)
    k_dec = (kk * jnp.exp2(g_last - gcum)).astype(BF16)
    kv_ref[slot] = _dot_tn(i_ref[sl, :], k_dec)
    dec_ref[pl.ds(slot, 1), :] = jnp.exp2(g_last)


def _hgrn(proj, lb, gain, *, batch, seq, t_rows):
    m = batch * seq
    t_rows = min(t_rows, seq)
    nt = seq // t_rows
    h = HG_HEADS
    grows = HG_GROUP * HG_CHUNK
    assert t_rows % grows == 0

    def spec(off):
        return pl.BlockSpec((t_rows, 128), lambda b, hh, t: (b * nt + t, off + hh))

    return pl.pallas_call(
        functools.partial(_hgrn_kernel, t_rows=t_rows),
        grid=(batch, h, nt),
        in_specs=[spec(0), spec(h), spec(2 * h), spec(3 * h),
                  pl.BlockSpec((1, 128), lambda b, hh, t: (0, hh)),
                  pl.BlockSpec((1, 128), lambda b, hh, t: (0, 0))],
        out_specs=pl.BlockSpec((t_rows, 128), lambda b, hh, t: (b * nt + t, hh)),
        out_shape=jax.ShapeDtypeStruct((m, h * HG_DV), BF16),
        scratch_shapes=[
            pltpu.VMEM((HG_DV, HG_DK), F32),
            pltpu.VMEM((2 * grows, 128), F32),
            pltpu.VMEM((2 * grows, 128), F32),
            pltpu.VMEM((2 * grows, HG_CHUNK), BF16),
            pltpu.VMEM((2 * grows, 128), BF16),
            pltpu.VMEM((2 * HG_GROUP, HG_DV, HG_DK), F32),
            pltpu.VMEM((2 * HG_GROUP, 128), F32),
        ],
        compiler_params=_cparams(("parallel", "parallel", "arbitrary")),
        name="hgrn2",
    )(proj, proj, proj, proj, lb, gain)


def _swa_kernel(sink_ref, q_ref, kp_ref, kc_ref, vp_ref, vc_ref, bias_ref, o_ref):
    blk = SWA_BLOCK
    dh = SWA_HEAD_DIM
    first = pl.program_id(1) == 0
    colk = lax.broadcasted_iota(jnp.int32, (blk, 2 * blk), 1)
    no_prev = first & (colk < blk)
    scale = dh ** -0.5
    outs = []
    for h in range(SWA_KV_HEADS):
        ks = slice(h * dh, (h + 1) * dh)
        kcat = jnp.concatenate([kp_ref[:, ks], kc_ref[:, ks]], axis=0)
        vcat = jnp.concatenate([vp_ref[:, ks], vc_ref[:, ks]], axis=0)
        for g in range(SWA_GROUP):
            hq = h * SWA_GROUP + g
            q = q_ref[:, hq * dh:(hq + 1) * dh]
            s = _dot_nt(q, kcat) * scale + bias_ref[hq]
            s = jnp.where(no_prev, NEG_BIG, s)
            sink = sink_ref[hq]
            mx = jnp.maximum(jnp.max(s, axis=-1, keepdims=True), sink)
            p = jnp.exp(s - mx)
            den = jnp.sum(p, axis=-1, keepdims=True) + jnp.exp(sink - mx)
            o = _dot(p.astype(BF16), vcat) / den
            outs.append(o)
    o_ref[...] = jnp.concatenate(outs, axis=-1).astype(o_ref.dtype)


def _swa(proj, sinks, bias, *, batch, seq, q_blk, k_blk, v_blk):
    m = batch * seq
    nb = seq // SWA_BLOCK
    blk = SWA_BLOCK
    kw = SWA_KV_HEADS * SWA_HEAD_DIM
    qw = SWA_HEADS * SWA_HEAD_DIM

    def cur(cb):
        return lambda b, n, s: (b * nb + n, cb)

    def prev(cb):
        return lambda b, n, s: (b * nb + jnp.maximum(n - 1, 0), cb)

    grid_spec = pltpu.PrefetchScalarGridSpec(
        num_scalar_prefetch=1,
        grid=(batch, nb),
        in_specs=[
            pl.BlockSpec((blk, qw), cur(q_blk)),
            pl.BlockSpec((blk, kw), prev(k_blk)),
            pl.BlockSpec((blk, kw), cur(k_blk)),
            pl.BlockSpec((blk, kw), prev(v_blk)),
            pl.BlockSpec((blk, kw), cur(v_blk)),
            pl.BlockSpec((SWA_HEADS, blk, 2 * blk), lambda b, n, s: (0, 0, 0)),
        ],
        out_specs=pl.BlockSpec((blk, qw), lambda b, n, s: (b * nb + n, 0)),
    )
    return pl.pallas_call(
        _swa_kernel,
        grid_spec=grid_spec,
        out_shape=jax.ShapeDtypeStruct((m, qw), BF16),
        compiler_params=_cparams(("parallel", "arbitrary")),
        name="swa",
    )(sinks, proj, proj, proj, proj, proj, bias)


def _t5_bias_table(t5_table):
    t_loc = np.arange(SWA_BLOCK, dtype=np.int32)[:, None]
    s_loc = np.arange(2 * SWA_BLOCK, dtype=np.int32)[None, :]
    dist = t_loc + SWA_BLOCK - s_loc
    n = np.maximum(dist, 0)
    nf = np.maximum(n, 1).astype(np.float32)
    large = T5_MAX_EXACT + (np.log(nf / np.float32(T5_MAX_EXACT)) / np.float32(math.log(T5_MAX_DIST / T5_MAX_EXACT))
                            * np.float32(T5_BUCKETS - T5_MAX_EXACT)).astype(np.int32)
    large = np.minimum(large, T5_BUCKETS - 1)
    bucket = np.where(n < T5_MAX_EXACT, n, large).astype(np.int32)
    in_band = (dist >= 0) & (dist < SWA_WINDOW)
    bias = t5_table.astype(F32)[jnp.asarray(bucket)]
    bias = jnp.where(jnp.asarray(in_band)[..., None], bias, NEG_BIG)
    return bias.transpose(2, 0, 1)


def _merge_kernel(x_ref, oa_ref, ob_ref, ga_ref, gb_ref, bg_ref, wa_ref, wb_ref, wm_ref, o_ref):
    a = _dot(oa_ref[...], wa_ref[...])
    b = _dot(ob_ref[...], wb_ref[...])
    bg = bg_ref[...]
    za = ga_ref[...].astype(F32) + bg[0:1]
    zb = gb_ref[...].astype(F32) + bg[1:2]
    merged = a / (1.0 + jnp.exp(-za)) + b / (1.0 + jnp.exp(-zb))
    o_ref[...] = x_ref[...] + _dot(merged.astype(BF16), wm_ref[...])


def _merge(x, o_a, o_b, proj, b_gates, wa, wb, wm, *, tm, ga_blk, gb_blk):
    m, d = x.shape
    tm = min(tm, m)
    row = lambda i: (i, 0)
    full = lambda i: (0, 0)
    return pl.pallas_call(
        _merge_kernel,
        grid=(m // tm,),
        in_specs=[
            pl.BlockSpec((tm, d), row),
            pl.BlockSpec((tm, d), row),
            pl.BlockSpec((tm, d), row),
            pl.BlockSpec((tm, d), lambda i: (i, ga_blk)),
            pl.BlockSpec((tm, d), lambda i: (i, gb_blk)),
            pl.BlockSpec((2, d), full),
            pl.BlockSpec((d, d), full),
            pl.BlockSpec((d, d), full),
            pl.BlockSpec((d, d), full),
        ],
        out_specs=pl.BlockSpec((tm, d), row),
        out_shape=jax.ShapeDtypeStruct((m, d), F32),
        compiler_params=_cparams(("parallel",)),
        name="merge",
    )(x, o_a, o_b, proj, proj, b_gates, wa, wb, wm)


def _xa_kernel(x_ref, gxa_ref, wq_ref, km_ref, vm_ref, wo_ref, gffn_ref, wr_ref, br_ref,
               x2_ref, hn_ref, idx_ref, gate_ref):
    x1 = x_ref[...]
    hx = _rms(x1, gxa_ref[...]).astype(BF16)
    q = _dot(hx, wq_ref[...]).astype(BF16)
    scale = XA_HEAD_DIM ** -0.5
    outs = []
    for h in range(XA_HEADS):
        sl = slice(h * XA_HEAD_DIM, (h + 1) * XA_HEAD_DIM)
        s = _dot_nt(q[:, sl], km_ref[:, sl]) * scale
        mx = jnp.max(s, axis=-1, keepdims=True)
        p = jnp.exp(s - mx)
        den = jnp.sum(p, axis=-1, keepdims=True)
        outs.append((_dot(p.astype(BF16), vm_ref[:, sl]) / den).astype(BF16))
    o = jnp.concatenate(outs, axis=-1)
    x2 = x1 + _dot(o, wo_ref[...])
    x2_ref[...] = x2
    hn = _rms(x2, gffn_ref[...])
    hn_ref[...] = _pack_rows(hn)
    hn_hi = hn.astype(BF16)
    hn_lo = (hn - hn_hi.astype(F32)).astype(BF16)
    big = _dot_nt(wr_ref[...], hn_hi)
    l = big[:N_EXPERTS] + big[N_EXPERTS:] + _dot_nt(wr_ref[:N_EXPERTS, :], hn_lo) + br_ref[...]
    eid = lax.broadcasted_iota(jnp.int32, l.shape, 0)
    vals = []
    idxs = []
    for k in range(TOP_K):
        mk = jnp.max(l, axis=0, keepdims=True)
        ik = jnp.min(jnp.where(l == mk, eid, N_EXPERTS), axis=0, keepdims=True)
        vals.append(mk)
        idxs.append(ik)
        l = jnp.where(eid == ik, -jnp.inf, l)
    ev = [jnp.exp(v - vals[0]) for v in vals]
    inv_den = 1.0 / (ev[0] + ev[1] + ev[2] + ev[3])
    gate_ref[...] = jnp.concatenate([e_k * inv_den for e_k in ev], axis=0)
    idx_ref[...] = jnp.concatenate(idxs, axis=0)


def _xa(x1, gxa, wq, kvmem, wo, gffn, wr, br, *, batch, seq, mem_len, tm):
    m, d = x1.shape
    tm = min(tm, seq)
    nt = seq // tm
    row = lambda i: (i, 0)
    full = lambda i: (0, 0)
    return pl.pallas_call(
        _xa_kernel,
        grid=(m // tm,),
        in_specs=[
            pl.BlockSpec((tm, d), row),
            pl.BlockSpec((1, d), full),
            pl.BlockSpec((d, d), full),
            pl.BlockSpec((mem_len, d), lambda i: (i // nt, 0)),
            pl.BlockSpec((mem_len, d), lambda i: (i // nt, 1)),
            pl.BlockSpec((d, d), full),
            pl.BlockSpec((1, d), full),
            pl.BlockSpec((2 * N_EXPERTS, d), full),
            pl.BlockSpec((N_EXPERTS, 1), full),
        ],
        out_specs=[
            pl.BlockSpec((tm, d), row),
            pl.BlockSpec((tm, HALF), row),
            pl.BlockSpec((TOP_K, tm), lambda i: (0, i)),
            pl.BlockSpec((TOP_K, tm), lambda i: (0, i)),
        ],
        out_shape=[
            jax.ShapeDtypeStruct((m, d), F32),
            jax.ShapeDtypeStruct((m, HALF), jnp.uint32),
            jax.ShapeDtypeStruct((TOP_K, m), jnp.int32),
            jax.ShapeDtypeStruct((TOP_K, m), F32),
        ],
        compiler_params=_cparams(("parallel",)),
        name="xattn_router",
    )(x1, gxa, wq, kvmem, kvmem, wo, gffn, wr, br)


def _moe_kernel(bexp_ref, x_ref, wgu_ref, bgu_ref, wd_ref, bd_ref, y_ref, wgu_bf, wd_bf):
    i = pl.program_id(0)

    @pl.when((i == 0) | (bexp_ref[i] != bexp_ref[jnp.maximum(i - 1, 0)]))
    def _():
        wgu_bf[...] = wgu_ref[0].astype(BF16)
        wd_bf[...] = wd_ref[0].astype(BF16)

    n_used = bexp_ref[pl.num_programs(0)]

    @pl.when(i < n_used)
    def _():
        xb = _unpack_rows(x_ref[...]).astype(BF16)
        gu = _dot(xb, wgu_bf[...]) + bgu_ref[0]
        x_glu = jnp.minimum(gu[:, :D_FF], SWIGLU_LIMIT)
        x_lin = jnp.clip(gu[:, D_FF:], -SWIGLU_LIMIT, SWIGLU_LIMIT)
        act = x_glu / (1.0 + jnp.exp(-SWIGLU_ALPHA * x_glu)) * (x_lin + 1.0)
        y_ref[...] = _pack_rows(_dot(act.astype(BF16), wd_bf[...]) + bd_ref[0])

    @pl.when(i >= n_used)
    def _():
        y_ref[...] = jnp.zeros_like(y_ref)


def _moe(block_expert, x_sorted, wgu, bgu, wd, bd):
    rows = x_sorted.shape[0]
    d = D_MODEL
    n_blocks = rows // MOE_BLOCK
    grid_spec = pltpu.PrefetchScalarGridSpec(
        num_scalar_prefetch=1,
        grid=(n_blocks,),
        in_specs=[
            pl.BlockSpec((MOE_BLOCK, HALF), lambda i, be: (i, 0)),
            pl.BlockSpec((1, d, 2 * D_FF), lambda i, be: (be[i], 0, 0)),
            pl.BlockSpec((1, 1, 2 * D_FF), lambda i, be: (be[i], 0, 0)),
            pl.BlockSpec((1, D_FF, d), lambda i, be: (be[i], 0, 0)),
            pl.BlockSpec((1, 1, d), lambda i, be: (be[i], 0, 0)),
        ],
        out_specs=pl.BlockSpec((MOE_BLOCK, HALF), lambda i, be: (i, 0)),
        scratch_shapes=[
            pltpu.VMEM((d, 2 * D_FF), BF16),
            pltpu.VMEM((D_FF, d), BF16),
        ],
    )
    return pl.pallas_call(
        _moe_kernel,
        grid_spec=grid_spec,
        out_shape=jax.ShapeDtypeStruct((rows, HALF), jnp.uint32),
        compiler_params=_cparams(("arbitrary",)),
        name="moe_experts",
    )(block_expert, x_sorted, wgu, bgu, wd, bd)


def _sc_gather(table, idx):
    b = idx.shape[0]
    d = table.shape[1]
    per_worker = b // SC_WORKERS
    n_chunks = per_worker // SC_CHUNK
    assert b % SC_WORKERS == 0 and per_worker % (2 * SC_CHUNK) == 0, (b, SC_WORKERS, SC_CHUNK)
    mesh = plsc.VectorSubcoreMesh(core_axis_name="c", subcore_axis_name="s")

    @functools.partial(
        pl.kernel, mesh=mesh,
        out_type=jax.ShapeDtypeStruct((b, d), table.dtype),
        scratch_types=[pltpu.VMEM((per_worker,), jnp.int32),
                       pltpu.VMEM((2, SC_CHUNK, d), table.dtype),
                       pltpu.SemaphoreType.DMA((2,)),
                       pltpu.SemaphoreType.DMA((2,))],
        name="sc_row_gather",
    )
    def gather_kernel(table_hbm, idx_hbm, out_hbm, idx_v, rows_v, g_sem, w_sem):
        worker = lax.axis_index("s") * SC_CORES + lax.axis_index("c")
        base = worker * per_worker
        pltpu.sync_copy(idx_hbm.at[pl.ds(base, per_worker)], idx_v)

        def gather(t, slot):
            return pltpu.make_async_copy(table_hbm.at[idx_v.at[pl.ds(t * SC_CHUNK, SC_CHUNK)]],
                                         rows_v.at[slot], g_sem.at[slot])

        def write(t, slot):
            return pltpu.make_async_copy(rows_v.at[slot], out_hbm.at[pl.ds(base + t * SC_CHUNK, SC_CHUNK)],
                                         w_sem.at[slot])

        gather(0, 0).start()

        def body(t2, carry):
            for slot in range(2):
                t = t2 * 2 + slot
                gather(t, slot).wait()

                @pl.when(t + 1 < n_chunks)
                def _():
                    @pl.when(t >= 1)
                    def _():
                        write(t - 1, 1 - slot).wait()
                    gather(t + 1, 1 - slot).start()
                write(t, slot).start()
            return carry
        lax.fori_loop(0, n_chunks // 2, body, 0)
        write(n_chunks - 2, 0).wait()
        write(n_chunks - 1, 1).wait()

    return gather_kernel(table, idx)


def _combine_kernel(x_ref, y0_ref, y1_ref, y2_ref, y3_ref, gate_ref, gain_ref, o_ref):
    g = gate_ref[...]
    acc = x_ref[...]
    for k, y_ref in enumerate((y0_ref, y1_ref, y2_ref, y3_ref)):
        acc = acc + g[:, k:k + 1] * _unpack_rows(y_ref[...])
    o_ref[...] = _rms(acc, gain_ref[...])


def _combine(x2, y, gates, gain, *, tm):
    m, d = x2.shape
    tm = min(tm, m)
    kb = m // tm
    row = lambda i: (i, 0)
    return pl.pallas_call(
        _combine_kernel,
        grid=(m // tm,),
        in_specs=[pl.BlockSpec((tm, d), row)]
        + [pl.BlockSpec((tm, HALF), functools.partial(lambda i, k: (k * kb + i, 0), k=k)) for k in range(TOP_K)]
        + [pl.BlockSpec((tm, TOP_K), row), pl.BlockSpec((1, d), lambda i: (0, 0))],
        out_specs=pl.BlockSpec((tm, d), row),
        out_shape=jax.ShapeDtypeStruct((m, d), F32),
        compiler_params=_cparams(("parallel",)),
        name="combine_norm",
    )(x2, y, y, y, y, gates, gain)


def _routing_tables(top_idx, n_tok):
    n_assign = n_tok * TOP_K
    e_flat = top_idx.reshape(-1)
    order = jnp.argsort(e_flat).astype(jnp.int32)
    experts = jnp.arange(N_EXPERTS, dtype=jnp.int32)
    counts = jnp.sum((e_flat[:, None] == experts[None, :]).astype(jnp.int32), axis=0)
    starts = jnp.cumsum(counts) - counts
    padded = (counts + MOE_BLOCK - 1) // MOE_BLOCK * MOE_BLOCK
    pad_ends = jnp.cumsum(padded)
    pad_starts = pad_ends - padded
    cap = -(-n_assign // MOE_BLOCK) * MOE_BLOCK + N_EXPERTS * MOE_BLOCK
    n_blocks = cap // MOE_BLOCK
    block_start = jnp.arange(n_blocks, dtype=jnp.int32) * MOE_BLOCK
    block_expert = jnp.minimum(
        jnp.sum((pad_ends[None, :] <= block_start[:, None]).astype(jnp.int32), axis=1), N_EXPERTS - 1)
    r = block_start[:, None] + jnp.arange(MOE_BLOCK, dtype=jnp.int32)[None, :]
    pos = r - pad_starts[block_expert][:, None]
    valid = pos < counts[block_expert][:, None]
    src = jnp.clip(starts[block_expert][:, None] + pos, 0, n_assign - 1)
    tok = jnp.where(valid, order[src] % n_tok, 0).reshape(-1)
    sorted_pos = jnp.argsort(order).astype(jnp.int32)
    row_of = sorted_pos + (pad_starts - starts)[e_flat]
    n_used = pad_ends[-1:] // MOE_BLOCK
    return tok, row_of, jnp.concatenate([block_expert, n_used]).astype(jnp.int32)


def kernel(x, mem, t5_table, lb_logits, norm_mix, w_in, b_gates, hg_norm_gain, swa_sinks, w_branch_a, w_branch_b, w_mix_out, norm_xa, xa_w_q, xa_w_k, xa_w_v, xa_w_o, norm_ffn, w_router, b_router, w_gate_up, b_gate_up, w_down, b_down, norm_final):
    batch, seq, d = x.shape
    mem_len = mem.shape[1]
    m = batch * seq
    depth = norm_mix.shape[0]
    assert depth == 1, "the final norm is fused into the layer's last kernel"
    lower_bounds = jnp.cumsum(jax.nn.softmax(lb_logits.astype(F32), axis=0), axis=0)
    bias = _t5_bias_table(t5_table)
    xt = x.reshape(m, d)
    for l in range(depth):
        w = w_in[l]
        cuts = np.cumsum([0, 1024, 1024, 1024, 1024, 1024, 256, 256, 1024, 1024])
        seg = [w[:, cuts[k]:cuts[k + 1]] for k in range(9)]
        w_perm = jnp.concatenate(seg[0:5] + seg[7:9] + seg[5:7], axis=1).astype(BF16)
        proj = _norm_matmul(xt, norm_mix[l][None], w_perm, use_norm=True, tm=1024, tn=1536, out_dtype=BF16)
        o_a = _hgrn(proj, lower_bounds[l][None], hg_norm_gain[l][None], batch=batch, seq=seq, t_rows=4096)
        o_b = _swa(proj, swa_sinks[l], bias, batch=batch, seq=seq, q_blk=4, k_blk=28, v_blk=29)
        x1 = _merge(xt, o_a, o_b, proj, b_gates[l], w_branch_a[l].astype(BF16), w_branch_b[l].astype(BF16),
                    w_mix_out[l].astype(BF16), tm=512, ga_blk=5, gb_blk=6)
        w_kv = jnp.concatenate([xa_w_k[l], xa_w_v[l]], axis=1).astype(BF16)
        kvmem = _norm_matmul(mem.reshape(batch * mem_len, d), norm_xa[l][None], w_kv, use_norm=False,
                             tm=1024, tn=1024, out_dtype=BF16)
        wr_hi = w_router[l].astype(BF16)
        wr_lo = (w_router[l] - wr_hi.astype(F32)).astype(BF16)
        x2, hn, top_idx, gates = _xa(x1, norm_xa[l][None], xa_w_q[l].astype(BF16), kvmem, xa_w_o[l].astype(BF16),
                                     norm_ffn[l][None], jnp.concatenate([wr_hi, wr_lo], axis=1).T,
                                     b_router[l][:, None], batch=batch, seq=seq, mem_len=mem_len, tm=512)
        gates = gates.T
        tok, row_of, block_expert = _routing_tables(top_idx, m)
        x_sorted = _sc_gather(hn, tok)
        y_sorted = _moe(block_expert, x_sorted, w_gate_up[l], b_gate_up[l][:, None, :],
                        w_down[l], b_down[l][:, None, :])
        y = _sc_gather(y_sorted, row_of)
        xt = _combine(x2, y, gates, norm_final[None], tm=256)
    return xt.reshape(batch, seq, d)
```

```python
import functools
import math

import jax
import jax.numpy as jnp
import numpy as np
from jax import lax
from jax.experimental import pallas as pl
from jax.experimental.pallas import tpu as pltpu
from jax.experimental.pallas import tpu_sc as plsc

F32 = jnp.float32
BF16 = jnp.bfloat16

RMS_EPS = 1e-5
D_MODEL = 1024
HG_HEADS = 8
HG_DK = 128
HG_DV = 128
HG_CHUNK = 64
HG_LEVELS = (32, 16, 8)
HG_DIAG = 8
HG_GROUP = 4
LOG2E = 1.4426950408889634
SWA_HEADS = 16
SWA_KV_HEADS = 4
SWA_GROUP = 4
SWA_HEAD_DIM = 64
SWA_WINDOW = 128
SWA_BLOCK = 128
T5_BUCKETS = 32
T5_MAX_EXACT = 16
T5_MAX_DIST = 128
XA_HEADS = 4
XA_HEAD_DIM = 256
N_EXPERTS = 32
TOP_K = 4
D_FF = 1024
SWIGLU_ALPHA = 1.702
SWIGLU_LIMIT = 7.0
MOE_BLOCK = 512
SC_CORES = 2
SC_WORKERS = 32
SC_CHUNK = 64
HALF = D_MODEL // 2
NEG_BIG = -1e30

VMEM_LIMIT = 56 * 1024 * 1024


def _cparams(sem):
    return pltpu.CompilerParams(dimension_semantics=sem, vmem_limit_bytes=VMEM_LIMIT)


def _rms(xf, gain):
    return xf * lax.rsqrt(jnp.mean(xf * xf, axis=-1, keepdims=True) + RMS_EPS) * gain


def _dot(a, b):
    return jnp.dot(a, b, preferred_element_type=F32)


def _dot_nt(a, b):
    return lax.dot_general(a, b, (((1,), (1,)), ((), ())), preferred_element_type=F32)


def _dot_tn(a, b):
    return lax.dot_general(a, b, (((0,), (0,)), ((), ())), preferred_element_type=F32)


def _pack_rows(x):
    hi = lax.bitcast_convert_type(x[:, :HALF].astype(BF16).astype(F32), jnp.uint32)
    lo = lax.bitcast_convert_type(x[:, HALF:].astype(BF16).astype(F32), jnp.uint32)
    return hi | (lo >> 16)


def _unpack_rows(p):
    hi = lax.bitcast_convert_type(p & jnp.uint32(0xFFFF0000), F32)
    lo = lax.bitcast_convert_type(p << 16, F32)
    return jnp.concatenate([hi, lo], axis=-1)


def _norm_matmul_kernel(x_ref, g_ref, w_ref, o_ref, h_ref, *, use_norm):
    @pl.when(pl.program_id(1) == 0)
    def _():
        xf = x_ref[...].astype(F32)
        if use_norm:
            xf = _rms(xf, g_ref[...])
        h_ref[...] = xf.astype(BF16)

    o_ref[...] = _dot(h_ref[...], w_ref[...]).astype(o_ref.dtype)


def _norm_matmul(x, gain, w, *, use_norm, tm, tn, out_dtype):
    m, k = x.shape
    n = w.shape[1]
    tm = min(tm, m)
    tn = min(tn, n)
    return pl.pallas_call(
        functools.partial(_norm_matmul_kernel, use_norm=use_norm),
        grid=(m // tm, n // tn),
        in_specs=[
            pl.BlockSpec((tm, k), lambda i, j: (i, 0)),
            pl.BlockSpec((1, k), lambda i, j: (0, 0)),
            pl.BlockSpec((k, tn), lambda i, j: (0, j)),
        ],
        out_specs=pl.BlockSpec((tm, tn), lambda i, j: (i, j)),
        out_shape=jax.ShapeDtypeStruct((m, n), out_dtype),
        scratch_shapes=[pltpu.VMEM((tm, k), BF16)],
        compiler_params=_cparams(("parallel", "arbitrary")),
        name="norm_matmul",
    )(x, gain, w)


def _rows_bcast(ref, base, group, row, n):
    parts = [jnp.broadcast_to(ref[pl.ds(base + g0 + row, 1), :], (group, ref.shape[1]))
             for g0 in range(0, n, group)]
    return jnp.concatenate(parts, axis=0)


def _hgrn_kernel(q_ref, f_ref, i_ref, g_ref, lb_ref, gain_ref, o_ref,
                 st_ref, kk_ref, gc_ref, a_ref, qd_ref, kv_ref, dec_ref, *, t_rows):
    c = HG_CHUNK
    d = HG_DIAG
    grows = HG_GROUP * c
    ngroups = t_rows // grows

    @pl.when(pl.program_id(2) == 0)
    def _():
        st_ref[...] = jnp.zeros_like(st_ref)

    lb = lb_ref[...]
    gain = gain_ref[...]
    row = lax.broadcasted_iota(jnp.int32, (c, 1), 0)
    col = lax.broadcasted_iota(jnp.int32, (c, c), 1)
    cd = col - (row // d) * d
    cd = jnp.where(cd <= row % d, cd, -1)
    lvl_mask = []
    for b in HG_LEVELS:
        rb = lax.broadcasted_iota(jnp.int32, (c, c), 0) // b
        lvl_mask.append(((rb - col // b) * 2 + (col // b) % 2) == 2)

    def rows(ref_row0):
        return pl.ds(pl.multiple_of(ref_row0, c), c)

    def phase_a(gi, buf):
        for ch in range(HG_GROUP):
            src = rows(gi * grows + ch * c)
            dst = buf * grows + ch * c
            _hgrn_front(q_ref, f_ref, i_ref, lb, row, cd, lvl_mask, kk_ref, gc_ref, a_ref, qd_ref, kv_ref,
                        dec_ref, src, dst, buf * HG_GROUP + ch)

    def phase_b(gi, buf, st):
        sts = []
        for ch in range(HG_GROUP):
            sts.append(st.astype(BF16))
            slot = buf * HG_GROUP + ch
            st = st * dec_ref[pl.ds(slot, 1), :] + kv_ref[slot]
        for ch in range(HG_GROUP):
            src = rows(gi * grows + ch * c)
            dst = rows(buf * grows + ch * c)
            o = _dot_nt(qd_ref[dst, :], sts[ch]) + _dot(a_ref[dst, :], i_ref[src, :])
            on = _rms(o, gain)
            gv = g_ref[src, :].astype(F32)
            o_ref[src, :] = (on * gv * (1.0 / (1.0 + jnp.exp(-gv)))).astype(o_ref.dtype)
        return st

    phase_a(0, 0)

    def body(k, st):
        st = phase_b(k, k % 2, st)
        phase_a(k + 1, (k + 1) % 2)
        return st

    st = lax.fori_loop(0, ngroups - 1, body, st_ref[...])
    st_ref[...] = phase_b(ngroups - 1, (ngroups - 1) % 2, st)


def _hgrn_front(q_ref, f_ref, i_ref, lb, row, cd, lvl_mask, kk_ref, gc_ref, a_ref, qd_ref, kv_ref, dec_ref,
                sl, dst, slot):
    c = HG_CHUNK
    d = HG_DIAG
    ds = pl.ds(pl.multiple_of(dst, c), c)
    ff = f_ref[sl, :].astype(F32)
    e = jnp.exp(-jnp.abs(ff))
    r = 1.0 / (1.0 + e)
    sig_pos = jnp.where(ff >= 0, r, e * r)
    sig_neg = jnp.where(ff >= 0, e * r, r)
    logf = jnp.log(lb + (1.0 - lb) * sig_pos)
    kk = (1.0 - lb) * sig_neg
    qv = q_ref[sl, :].astype(F32)
    qf = qv * (1.0 / (1.0 + jnp.exp(-qv)))

    gcum = logf * LOG2E
    sh = 1
    while sh < c:
        gcum = gcum + jnp.where(row >= sh, pltpu.roll(gcum, sh, 0), 0.0)
        sh *= 2
    kk_ref[ds, :] = kk
    gc_ref[ds, :] = gcum
    g_last = gcum[c - 1:c, :]

    a = jnp.zeros((c, c), F32)
    for j in range(d):
        gj = _rows_bcast(gc_ref, dst, d, j, c)
        kj = _rows_bcast(kk_ref, dst, d, j, c)
        p = qf * jnp.exp2(gcum - gj) * kj
        a = jnp.where(cd == j, jnp.sum(p, axis=-1, keepdims=True), a)

    for li, b in enumerate(HG_LEVELS):
        gref = _rows_bcast(gc_ref, dst, 2 * b, b - 1, c)
        upper = (row % (2 * b)) >= b
        qa = jnp.where(upper, qf * jnp.exp2(gcum - gref), 0.0).astype(BF16)
        kb = jnp.where(upper, 0.0, kk * jnp.exp2(gref - gcum)).astype(BF16)
        a = a + jnp.where(lvl_mask[li], _dot_nt(qa, kb), 0.0)
    a_ref[ds, :] = a.astype(BF16)

    qd_ref[ds, :] = (qf * jnp.exp2(gcum)).astype(BF16)
    k_dec = (kk * jnp.exp2(g_last - gcum)).astype(BF16)
    kv_ref[slot] = _dot_tn(i_ref[sl, :], k_dec)
    dec_ref[pl.ds(slot, 1), :] = jnp.exp2(g_last)


def _hgrn(proj, lb, gain, *, batch, seq, t_rows):
    m = batch * seq
    t_rows = min(t_rows, seq)
    nt = seq // t_rows
    h = HG_HEADS
    grows = HG_GROUP * HG_CHUNK
    assert t_rows % grows == 0

    def spec(off):
        return pl.BlockSpec((t_rows, 128), lambda b, hh, t: (b * nt + t, off + hh))

    return pl.pallas_call(
        functools.partial(_hgrn_kernel, t_rows=t_rows),
        grid=(batch, h, nt),
        in_specs=[spec(0), spec(h), spec(2 * h), spec(3 * h),
                  pl.BlockSpec((1, 128), lambda b, hh, t: (0, hh)),
                  pl.BlockSpec((1, 128), lambda b, hh, t: (0, 0))],
        out_specs=pl.BlockSpec((t_rows, 128), lambda b, hh, t: (b * nt + t, hh)),
        out_shape=jax.ShapeDtypeStruct((m, h * HG_DV), BF16),
        scratch_shapes=[
            pltpu.VMEM((HG_DV, HG_DK), F32),
            pltpu.VMEM((2 * grows, 128), F32),
            pltpu.VMEM((2 * grows, 128), F32),
            pltpu.VMEM((2 * grows, HG_CHUNK), BF16),
            pltpu.VMEM((2 * grows, 128), BF16),
            pltpu.VMEM((2 * HG_GROUP, HG_DV, HG_DK), F32),
            pltpu.VMEM((2 * HG_GROUP, 128), F32),
        ],
        compiler_params=_cparams(("parallel", "parallel", "arbitrary")),
        name="hgrn2",
    )(proj, proj, proj, proj, lb, gain)


def _swa_kernel(sink_ref, q_ref, kp_ref, kc_ref, vp_ref, vc_ref, bias_ref, o_ref):
    blk = SWA_BLOCK
    dh = SWA_HEAD_DIM
    first = pl.program_id(1) == 0
    colk = lax.broadcasted_iota(jnp.int32, (blk, 2 * blk), 1)
    no_prev = first & (colk < blk)
    scale = dh ** -0.5
    outs = []
    for h in range(SWA_KV_HEADS):
        ks = slice(h * dh, (h + 1) * dh)
        kcat = jnp.concatenate([kp_ref[:, ks], kc_ref[:, ks]], axis=0)
        vcat = jnp.concatenate([vp_ref[:, ks], vc_ref[:, ks]], axis=0)
        for g in range(SWA_GROUP):
            hq = h * SWA_GROUP + g
            q = q_ref[:, hq * dh:(hq + 1) * dh]
            s = _dot_nt(q, kcat) * scale + bias_ref[hq]
            s = jnp.where(no_prev, NEG_BIG, s)
            sink = sink_ref[hq]
            mx = jnp.maximum(jnp.max(s, axis=-1, keepdims=True), sink)
            p = jnp.exp(s - mx)
            den = jnp.sum(p, axis=-1, keepdims=True) + jnp.exp(sink - mx)
            o = _dot(p.astype(BF16), vcat) / den
            outs.append(o)
    o_ref[...] = jnp.concatenate(outs, axis=-1).astype(o_ref.dtype)


def _swa(proj, sinks, bias, *, batch, seq, q_blk, k_blk, v_blk):
    m = batch * seq
    nb = seq // SWA_BLOCK
    blk = SWA_BLOCK
    kw = SWA_KV_HEADS * SWA_HEAD_DIM
    qw = SWA_HEADS * SWA_HEAD_DIM

    def cur(cb):
        return lambda b, n, s: (b * nb + n, cb)

    def prev(cb):
        return lambda b, n, s: (b * nb + jnp.maximum(n - 1, 0), cb)

    grid_spec = pltpu.PrefetchScalarGridSpec(
        num_scalar_prefetch=1,
        grid=(batch, nb),
        in_specs=[
            pl.BlockSpec((blk, qw), cur(q_blk)),
            pl.BlockSpec((blk, kw), prev(k_blk)),
            pl.BlockSpec((blk, kw), cur(k_blk)),
            pl.BlockSpec((blk, kw), prev(v_blk)),
            pl.BlockSpec((blk, kw), cur(v_blk)),
            pl.BlockSpec((SWA_HEADS, blk, 2 * blk), lambda b, n, s: (0, 0, 0)),
        ],
        out_specs=pl.BlockSpec((blk, qw), lambda b, n, s: (b * nb + n, 0)),
    )
    return pl.pallas_call(
        _swa_kernel,
        grid_spec=grid_spec,
        out_shape=jax.ShapeDtypeStruct((m, qw), BF16),
        compiler_params=_cparams(("parallel", "arbitrary")),
        name="swa",
    )(sinks, proj, proj, proj, proj, proj, bias)


def _t5_bias_table(t5_table):
    t_loc = np.arange(SWA_BLOCK, dtype=np.int32)[:, None]
    s_loc = np.arange(2 * SWA_BLOCK, dtype=np.int32)[None, :]
    dist = t_loc + SWA_BLOCK - s_loc
    n = np.maximum(dist, 0)
    nf = np.maximum(n, 1).astype(np.float32)
    large = T5_MAX_EXACT + (np.log(nf / np.float32(T5_MAX_EXACT)) / np.float32(math.log(T5_MAX_DIST / T5_MAX_EXACT))
                            * np.float32(T5_BUCKETS - T5_MAX_EXACT)).astype(np.int32)
    large = np.minimum(large, T5_BUCKETS - 1)
    bucket = np.where(n < T5_MAX_EXACT, n, large).astype(np.int32)
    in_band = (dist >= 0) & (dist < SWA_WINDOW)
    bias = t5_table.astype(F32)[jnp.asarray(bucket)]
    bias = jnp.where(jnp.asarray(in_band)[..., None], bias, NEG_BIG)
    return bias.transpose(2, 0, 1)


def _merge_rows(x_ref, oa_ref, ob_ref, ga_ref, gb_ref, bg_ref, wa_ref, wb_ref, wm_ref):
    a = _dot(oa_ref[...], wa_ref[...])
    b = _dot(ob_ref[...], wb_ref[...])
    bg = bg_ref[...]
    za = ga_ref[...].astype(F32) + bg[0:1]
    zb = gb_ref[...].astype(F32) + bg[1:2]
    merged = a / (1.0 + jnp.exp(-za)) + b / (1.0 + jnp.exp(-zb))
    return x_ref[...] + _dot(merged.astype(BF16), wm_ref[...])


def _xa_kernel(x_ref, oa_ref, ob_ref, ga_ref, gb_ref, bg_ref, wa_ref, wb_ref, wm_ref,
               gxa_ref, wq_ref, km_ref, vm_ref, wo_ref, gffn_ref, wr_ref, br_ref,
               x2_ref, hn_ref, idx_ref, gate_ref):
    x1 = _merge_rows(x_ref, oa_ref, ob_ref, ga_ref, gb_ref, bg_ref, wa_ref, wb_ref, wm_ref)
    hx = _rms(x1, gxa_ref[...]).astype(BF16)
    q = _dot(hx, wq_ref[...]).astype(BF16)
    scale = XA_HEAD_DIM ** -0.5
    outs = []
    for h in range(XA_HEADS):
        sl = slice(h * XA_HEAD_DIM, (h + 1) * XA_HEAD_DIM)
        s = _dot_nt(q[:, sl], km_ref[:, sl]) * scale
        mx = jnp.max(s, axis=-1, keepdims=True)
        p = jnp.exp(s - mx)
        den = jnp.sum(p, axis=-1, keepdims=True)
        outs.append((_dot(p.astype(BF16), vm_ref[:, sl]) / den).astype(BF16))
    o = jnp.concatenate(outs, axis=-1)
    x2 = x1 + _dot(o, wo_ref[...])
    x2_ref[...] = x2
    hn = _rms(x2, gffn_ref[...])
    hn_ref[...] = _pack_rows(hn)
    hn_hi = hn.astype(BF16)
    hn_lo = (hn - hn_hi.astype(F32)).astype(BF16)
    big = _dot_nt(wr_ref[...], hn_hi)
    l = big[:N_EXPERTS] + big[N_EXPERTS:] + _dot_nt(wr_ref[:N_EXPERTS, :], hn_lo) + br_ref[...]
    eid = lax.broadcasted_iota(jnp.int32, l.shape, 0)
    vals = []
    idxs = []
    for k in range(TOP_K):
        mk = jnp.max(l, axis=0, keepdims=True)
        ik = jnp.min(jnp.where(l == mk, eid, N_EXPERTS), axis=0, keepdims=True)
        vals.append(mk)
        idxs.append(ik)
        l = jnp.where(eid == ik, -jnp.inf, l)
    ev = [jnp.exp(v - vals[0]) for v in vals]
    inv_den = 1.0 / (ev[0] + ev[1] + ev[2] + ev[3])
    gate_ref[...] = jnp.concatenate([e_k * inv_den for e_k in ev], axis=0)
    idx_ref[...] = jnp.concatenate(idxs, axis=0)


def _xa(x, o_a, o_b, proj, b_gates, wa, wb, wm, gxa, wq, kvmem, wo, gffn, wr, br, *,
        batch, seq, mem_len, tm, ga_blk, gb_blk):
    m, d = x.shape
    tm = min(tm, seq)
    nt = seq // tm
    row = lambda i: (i, 0)
    full = lambda i: (0, 0)
    return pl.pallas_call(
        _xa_kernel,
        grid=(m // tm,),
        in_specs=[
            pl.BlockSpec((tm, d), row),
            pl.BlockSpec((tm, d), row),
            pl.BlockSpec((tm, d), row),
            pl.BlockSpec((tm, d), lambda i: (i, ga_blk)),
            pl.BlockSpec((tm, d), lambda i: (i, gb_blk)),
            pl.BlockSpec((2, d), full),
            pl.BlockSpec((d, d), full),
            pl.BlockSpec((d, d), full),
            pl.BlockSpec((d, d), full),
            pl.BlockSpec((1, d), full),
            pl.BlockSpec((d, d), full),
            pl.BlockSpec((mem_len, d), lambda i: (i // nt, 0)),
            pl.BlockSpec((mem_len, d), lambda i: (i // nt, 1)),
            pl.BlockSpec((d, d), full),
            pl.BlockSpec((1, d), full),
            pl.BlockSpec((2 * N_EXPERTS, d), full),
            pl.BlockSpec((N_EXPERTS, 1), full),
        ],
        out_specs=[
            pl.BlockSpec((tm, d), row),
            pl.BlockSpec((tm, HALF), row),
            pl.BlockSpec((TOP_K, tm), lambda i: (0, i)),
            pl.BlockSpec((TOP_K, tm), lambda i: (0, i)),
        ],
        out_shape=[
            jax.ShapeDtypeStruct((m, d), F32),
            jax.ShapeDtypeStruct((m, HALF), jnp.uint32),
            jax.ShapeDtypeStruct((TOP_K, m), jnp.int32),
            jax.ShapeDtypeStruct((TOP_K, m), F32),
        ],
        compiler_params=_cparams(("parallel",)),
        name="merge_xattn_router",
    )(x, o_a, o_b, proj, proj, b_gates, wa, wb, wm, gxa, wq, kvmem, kvmem, wo, gffn, wr, br)


def _moe_kernel(bexp_ref, x_ref, wgu_ref, bgu_ref, wd_ref, bd_ref, y_ref, wgu_bf, wd_bf):
    i = pl.program_id(0)

    @pl.when((i == 0) | (bexp_ref[i] != bexp_ref[jnp.maximum(i - 1, 0)]))
    def _():
        wgu_bf[...] = wgu_ref[0].astype(BF16)
        wd_bf[...] = wd_ref[0].astype(BF16)

    n_used = bexp_ref[pl.num_programs(0)]

    @pl.when(i < n_used)
    def _():
        xb = _unpack_rows(x_ref[...]).astype(BF16)
        gu = _dot(xb, wgu_bf[...]) + bgu_ref[0]
        x_glu = jnp.minimum(gu[:, :D_FF], SWIGLU_LIMIT)
        x_lin = jnp.clip(gu[:, D_FF:], -SWIGLU_LIMIT, SWIGLU_LIMIT)
        act = x_glu / (1.0 + jnp.exp(-SWIGLU_ALPHA * x_glu)) * (x_lin + 1.0)
        y_ref[...] = _pack_rows(_dot(act.astype(BF16), wd_bf[...]) + bd_ref[0])

    @pl.when(i >= n_used)
    def _():
        y_ref[...] = jnp.zeros_like(y_ref)


def _moe(block_expert, x_sorted, wgu, bgu, wd, bd):
    rows = x_sorted.shape[0]
    d = D_MODEL
    n_blocks = rows // MOE_BLOCK
    grid_spec = pltpu.PrefetchScalarGridSpec(
        num_scalar_prefetch=1,
        grid=(n_blocks,),
        in_specs=[
            pl.BlockSpec((MOE_BLOCK, HALF), lambda i, be: (i, 0)),
            pl.BlockSpec((1, d, 2 * D_FF), lambda i, be: (be[i], 0, 0)),
            pl.BlockSpec((1, 1, 2 * D_FF), lambda i, be: (be[i], 0, 0)),
            pl.BlockSpec((1, D_FF, d), lambda i, be: (be[i], 0, 0)),
            pl.BlockSpec((1, 1, d), lambda i, be: (be[i], 0, 0)),
        ],
        out_specs=pl.BlockSpec((MOE_BLOCK, HALF), lambda i, be: (i, 0)),
        scratch_shapes=[
            pltpu.VMEM((d, 2 * D_FF), BF16),
            pltpu.VMEM((D_FF, d), BF16),
        ],
    )
    return pl.pallas_call(
        _moe_kernel,
        grid_spec=grid_spec,
        out_shape=jax.ShapeDtypeStruct((rows, HALF), jnp.uint32),
        compiler_params=_cparams(("arbitrary",)),
        name="moe_experts",
    )(block_expert, x_sorted, wgu, bgu, wd, bd)


def _sc_gather(table, idx):
    b = idx.shape[0]
    d = table.shape[1]
    per_worker = b // SC_WORKERS
    n_chunks = per_worker // SC_CHUNK
    assert b % SC_WORKERS == 0 and per_worker % (2 * SC_CHUNK) == 0, (b, SC_WORKERS, SC_CHUNK)
    mesh = plsc.VectorSubcoreMesh(core_axis_name="c", subcore_axis_name="s")

    @functools.partial(
        pl.kernel, mesh=mesh,
        out_type=jax.ShapeDtypeStruct((b, d), table.dtype),
        scratch_types=[pltpu.VMEM((per_worker,), jnp.int32),
                       pltpu.VMEM((2, SC_CHUNK, d), table.dtype),
                       pltpu.SemaphoreType.DMA((2,)),
                       pltpu.SemaphoreType.DMA((2,))],
        name="sc_row_gather",
    )
    def gather_kernel(table_hbm, idx_hbm, out_hbm, idx_v, rows_v, g_sem, w_sem):
        worker = lax.axis_index("s") * SC_CORES + lax.axis_index("c")
        base = worker * per_worker
        pltpu.sync_copy(idx_hbm.at[pl.ds(base, per_worker)], idx_v)

        def gather(t, slot):
            return pltpu.make_async_copy(table_hbm.at[idx_v.at[pl.ds(t * SC_CHUNK, SC_CHUNK)]],
                                         rows_v.at[slot], g_sem.at[slot])

        def write(t, slot):
            return pltpu.make_async_copy(rows_v.at[slot], out_hbm.at[pl.ds(base + t * SC_CHUNK, SC_CHUNK)],
                                         w_sem.at[slot])

        gather(0, 0).start()

        def body(t2, carry):
            for slot in range(2):
                t = t2 * 2 + slot
                gather(t, slot).wait()

                @pl.when(t + 1 < n_chunks)
                def _():
                    @pl.when(t >= 1)
                    def _():
                        write(t - 1, 1 - slot).wait()
                    gather(t + 1, 1 - slot).start()
                write(t, slot).start()
            return carry
        lax.fori_loop(0, n_chunks // 2, body, 0)
        write(n_chunks - 2, 0).wait()
        write(n_chunks - 1, 1).wait()

    return gather_kernel(table, idx)


def _combine_kernel(x_ref, y0_ref, y1_ref, y2_ref, y3_ref, gate_ref, gain_ref, o_ref):
    g = gate_ref[...]
    acc = x_ref[...]
    for k, y_ref in enumerate((y0_ref, y1_ref, y2_ref, y3_ref)):
        acc = acc + g[:, k:k + 1] * _unpack_rows(y_ref[...])
    o_ref[...] = _rms(acc, gain_ref[...])


def _combine(x2, y, gates, gain, *, tm):
    m, d = x2.shape
    tm = min(tm, m)
    kb = m // tm
    row = lambda i: (i, 0)
    return pl.pallas_call(
        _combine_kernel,
        grid=(m // tm,),
        in_specs=[pl.BlockSpec((tm, d), row)]
        + [pl.BlockSpec((tm, HALF), functools.partial(lambda i, k: (k * kb + i, 0), k=k)) for k in range(TOP_K)]
        + [pl.BlockSpec((tm, TOP_K), row), pl.BlockSpec((1, d), lambda i: (0, 0))],
        out_specs=pl.BlockSpec((tm, d), row),
        out_shape=jax.ShapeDtypeStruct((m, d), F32),
        compiler_params=_cparams(("parallel",)),
        name="combine_norm",
    )(x2, y, y, y, y, gates, gain)


def _routing_tables(top_idx, n_tok):
    n_assign = n_tok * TOP_K
    e_flat = top_idx.reshape(-1)
    order = jnp.argsort(e_flat).astype(jnp.int32)
    experts = jnp.arange(N_EXPERTS, dtype=jnp.int32)
    counts = jnp.sum((e_flat[:, None] == experts[None, :]).astype(jnp.int32), axis=0)
    starts = jnp.cumsum(counts) - counts
    padded = (counts + MOE_BLOCK - 1) // MOE_BLOCK * MOE_BLOCK
    pad_ends = jnp.cumsum(padded)
    pad_starts = pad_ends - padded
    cap = -(-n_assign // MOE_BLOCK) * MOE_BLOCK + N_EXPERTS * MOE_BLOCK
    n_blocks = cap // MOE_BLOCK
    block_start = jnp.arange(n_blocks, dtype=jnp.int32) * MOE_BLOCK
    block_expert = jnp.minimum(
        jnp.sum((pad_ends[None, :] <= block_start[:, None]).astype(jnp.int32), axis=1), N_EXPERTS - 1)
    r = block_start[:, None] + jnp.arange(MOE_BLOCK, dtype=jnp.int32)[None, :]
    pos = r - pad_starts[block_expert][:, None]
    valid = pos < counts[block_expert][:, None]
    src = jnp.clip(starts[block_expert][:, None] + pos, 0, n_assign - 1)
    tok = jnp.where(valid, order[src] % n_tok, r % n_tok).reshape(-1)
    sorted_pos = jnp.argsort(order).astype(jnp.int32)
    row_of = sorted_pos + (pad_starts - starts)[e_flat]
    n_used = pad_ends[-1:] // MOE_BLOCK
    return tok, row_of, jnp.concatenate([block_expert, n_used]).astype(jnp.int32)


def kernel(x, mem, t5_table, lb_logits, norm_mix, w_in, b_gates, hg_norm_gain, swa_sinks, w_branch_a, w_branch_b, w_mix_out, norm_xa, xa_w_q, xa_w_k, xa_w_v, xa_w_o, norm_ffn, w_router, b_router, w_gate_up, b_gate_up, w_down, b_down, norm_final):
    batch, seq, d = x.shape
    mem_len = mem.shape[1]
    m = batch * seq
    depth = norm_mix.shape[0]
    assert depth == 1, "the final norm is fused into the layer's last kernel"
    lower_bounds = jnp.cumsum(jax.nn.softmax(lb_logits.astype(F32), axis=0), axis=0)
    bias = _t5_bias_table(t5_table)
    xt = x.reshape(m, d)
    for l in range(depth):
        w = w_in[l]
        cuts = np.cumsum([0, 1024, 1024, 1024, 1024, 1024, 256, 256, 1024, 1024])
        seg = [w[:, cuts[k]:cuts[k + 1]] for k in range(9)]
        w_perm = jnp.concatenate(seg[0:5] + seg[7:9] + seg[5:7], axis=1).astype(BF16)
        proj = _norm_matmul(xt, norm_mix[l][None], w_perm, use_norm=True, tm=2048, tn=1536, out_dtype=BF16)
        o_a = _hgrn(proj, lower_bounds[l][None], hg_norm_gain[l][None], batch=batch, seq=seq, t_rows=4096)
        o_b = _swa(proj, swa_sinks[l], bias, batch=batch, seq=seq, q_blk=4, k_blk=28, v_blk=29)
        w_kv = jnp.concatenate([xa_w_k[l], xa_w_v[l]], axis=1).astype(BF16)
        kvmem = _norm_matmul(mem.reshape(batch * mem_len, d), norm_xa[l][None], w_kv, use_norm=False,
                             tm=1024, tn=1024, out_dtype=BF16)
        wr_hi = w_router[l].astype(BF16)
        wr_lo = (w_router[l] - wr_hi.astype(F32)).astype(BF16)
        x2, hn, top_idx, gates = _xa(xt, o_a, o_b, proj, b_gates[l], w_branch_a[l].astype(BF16),
                                     w_branch_b[l].astype(BF16), w_mix_out[l].astype(BF16),
                                     norm_xa[l][None], xa_w_q[l].astype(BF16), kvmem, xa_w_o[l].astype(BF16),
                                     norm_ffn[l][None], jnp.concatenate([wr_hi, wr_lo], axis=1).T,
                                     b_router[l][:, None], batch=batch, seq=seq, mem_len=mem_len, tm=512,
                                     ga_blk=5, gb_blk=6)
        gates = gates.T
        tok, row_of, block_expert = _routing_tables(top_idx, m)
        x_sorted = _sc_gather(hn, tok)
        y_sorted = _moe(block_expert, x_sorted, w_gate_up[l], b_gate_up[l][:, None, :],
                        w_down[l], b_down[l][:, None, :])
        y = _sc_gather(y_sorted, row_of)
        xt = _combine(x2, y, gates, norm_final[None], tm=256)
    return xt.reshape(batch, seq, d)
```

```python
import functools
import math

import jax
import jax.numpy as jnp
import numpy as np
from jax import lax
from jax.experimental import pallas as pl
from jax.experimental.pallas import tpu as pltpu
from jax.experimental.pallas import tpu_sc as plsc

F32 = jnp.float32
BF16 = jnp.bfloat16

RMS_EPS = 1e-5
D_MODEL = 1024
HG_HEADS = 8
HG_DK = 128
HG_DV = 128
HG_CHUNK = 64
HG_LEVELS = (32, 16, 8)
HG_DIAG = 8
HG_GROUP = 4
LOG2E = 1.4426950408889634
SWA_HEADS = 16
SWA_KV_HEADS = 4
SWA_GROUP = 4
SWA_HEAD_DIM = 64
SWA_WINDOW = 128
SWA_BLOCK = 128
T5_BUCKETS = 32
T5_MAX_EXACT = 16
T5_MAX_DIST = 128
XA_HEADS = 4
XA_HEAD_DIM = 256
N_EXPERTS = 32
TOP_K = 4
D_FF = 1024
SWIGLU_ALPHA = 1.702
SWIGLU_LIMIT = 7.0
MOE_BLOCK = 512
SC_CORES = 2
SC_WORKERS = 32
SC_CHUNK = 64
HALF = D_MODEL // 2
NEG_BIG = -1e30

VMEM_LIMIT = 56 * 1024 * 1024


def _cparams(sem):
    return pltpu.CompilerParams(dimension_semantics=sem, vmem_limit_bytes=VMEM_LIMIT)


def _rms(xf, gain):
    return xf * lax.rsqrt(jnp.mean(xf * xf, axis=-1, keepdims=True) + RMS_EPS) * gain


def _dot(a, b):
    return jnp.dot(a, b, preferred_element_type=F32)


def _dot_nt(a, b):
    return lax.dot_general(a, b, (((1,), (1,)), ((), ())), preferred_element_type=F32)


def _dot_tn(a, b):
    return lax.dot_general(a, b, (((0,), (0,)), ((), ())), preferred_element_type=F32)


def _pack_rows(x):
    hi = lax.bitcast_convert_type(x[:, :HALF].astype(BF16).astype(F32), jnp.uint32)
    lo = lax.bitcast_convert_type(x[:, HALF:].astype(BF16).astype(F32), jnp.uint32)
    return hi | (lo >> 16)


def _unpack_rows(p):
    hi = lax.bitcast_convert_type(p & jnp.uint32(0xFFFF0000), F32)
    lo = lax.bitcast_convert_type(p << 16, F32)
    return jnp.concatenate([hi, lo], axis=-1)


def _norm_matmul_kernel(x_ref, g_ref, w_ref, o_ref, h_ref, *, use_norm):
    @pl.when(pl.program_id(1) == 0)
    def _():
        xf = x_ref[...].astype(F32)
        if use_norm:
            xf = _rms(xf, g_ref[...])
        h_ref[...] = xf.astype(BF16)

    o_ref[...] = _dot(h_ref[...], w_ref[...]).astype(o_ref.dtype)


def _norm_matmul(x, gain, w, *, use_norm, tm, tn, out_dtype):
    m, k = x.shape
    n = w.shape[1]
    tm = min(tm, m)
    tn = min(tn, n)
    return pl.pallas_call(
        functools.partial(_norm_matmul_kernel, use_norm=use_norm),
        grid=(m // tm, n // tn),
        in_specs=[
            pl.BlockSpec((tm, k), lambda i, j: (i, 0)),
            pl.BlockSpec((1, k), lambda i, j: (0, 0)),
            pl.BlockSpec((k, tn), lambda i, j: (0, j)),
        ],
        out_specs=pl.BlockSpec((tm, tn), lambda i, j: (i, j)),
        out_shape=jax.ShapeDtypeStruct((m, n), out_dtype),
        scratch_shapes=[pltpu.VMEM((tm, k), BF16)],
        compiler_params=_cparams(("parallel", "arbitrary")),
        name="norm_matmul",
    )(x, gain, w)


def _rows_bcast(ref, base, group, row, n):
    parts = [jnp.broadcast_to(ref[pl.ds(base + g0 + row, 1), :], (group, ref.shape[1]))
             for g0 in range(0, n, group)]
    return jnp.concatenate(parts, axis=0)


def _hgrn_kernel(q_ref, f_ref, i_ref, g_ref, lb_ref, gain_ref, o_ref,
                 st_ref, kk_ref, gc_ref, a_ref, qd_ref, kv_ref, dec_ref, *, t_rows):
    c = HG_CHUNK
    d = HG_DIAG
    grows = HG_GROUP * c
    ngroups = t_rows // grows

    @pl.when(pl.program_id(2) == 0)
    def _():
        st_ref[...] = jnp.zeros_like(st_ref)

    lb = lb_ref[...]
    gain = gain_ref[...]
    row = lax.broadcasted_iota(jnp.int32, (c, 1), 0)
    col = lax.broadcasted_iota(jnp.int32, (c, c), 1)
    cd = col - (row // d) * d
    cd = jnp.where(cd <= row % d, cd, -1)
    lvl_mask = []
    for b in HG_LEVELS:
        rb = lax.broadcasted_iota(jnp.int32, (c, c), 0) // b
        lvl_mask.append(((rb - col // b) * 2 + (col // b) % 2) == 2)

    def rows(ref_row0):
        return pl.ds(pl.multiple_of(ref_row0, c), c)

    def phase_a(gi, buf):
        for ch in range(HG_GROUP):
            src = rows(gi * grows + ch * c)
            dst = buf * grows + ch * c
            _hgrn_front(q_ref, f_ref, i_ref, lb, row, cd, lvl_mask, kk_ref, gc_ref, a_ref, qd_ref, kv_ref,
                        dec_ref, src, dst, buf * HG_GROUP + ch)

    def phase_b(gi, buf, st):
        sts = []
        for ch in range(HG_GROUP):
            sts.append(st.astype(BF16))
            slot = buf * HG_GROUP + ch
            st = st * dec_ref[pl.ds(slot, 1), :] + kv_ref[slot]
        for ch in range(HG_GROUP):
            src = rows(gi * grows + ch * c)
            dst = rows(buf * grows + ch * c)
            o = _dot_nt(qd_ref[dst, :], sts[ch]) + _dot(a_ref[dst, :], i_ref[src, :])
            on = _rms(o, gain)
            gv = g_ref[src, :].astype(F32)
            o_ref[src, :] = (on * gv * (1.0 / (1.0 + jnp.exp(-gv)))).astype(o_ref.dtype)
        return st

    phase_a(0, 0)

    def body(k, st):
        st = phase_b(k, k % 2, st)
        phase_a(k + 1, (k + 1) % 2)
        return st

    st = lax.fori_loop(0, ngroups - 1, body, st_ref[...])
    st_ref[...] = phase_b(ngroups - 1, (ngroups - 1) % 2, st)


def _hgrn_front(q_ref, f_ref, i_ref, lb, row, cd, lvl_mask, kk_ref, gc_ref, a_ref, qd_ref, kv_ref, dec_ref,
                sl, dst, slot):
    c = HG_CHUNK
    d = HG_DIAG
    ds = pl.ds(pl.multiple_of(dst, c), c)
    ff = f_ref[sl, :].astype(F32)
    e = jnp.exp(-jnp.abs(ff))
    r = 1.0 / (1.0 + e)
    sig_pos = jnp.where(ff >= 0, r, e * r)
    sig_neg = jnp.where(ff >= 0, e * r, r)
    logf = jnp.log(lb + (1.0 - lb) * sig_pos)
    kk = (1.0 - lb) * sig_neg
    qv = q_ref[sl, :].astype(F32)
    qf = qv * (1.0 / (1.0 + jnp.exp(-qv)))

    gcum = logf * LOG2E
    sh = 1
    while sh < c:
        gcum = gcum + jnp.where(row >= sh, pltpu.roll(gcum, sh, 0), 0.0)
        sh *= 2
    kk_ref[ds, :] = kk
    gc_ref[ds, :] = gcum
    g_last = gcum[c - 1:c, :]

    a = jnp.zeros((c, c), F32)
    for j in range(d):
        gj = _rows_bcast(gc_ref, dst, d, j, c)
        kj = _rows_bcast(kk_ref, dst, d, j, c)
        p = qf * jnp.exp2(gcum - gj) * kj
        a = jnp.where(cd == j, jnp.sum(p, axis=-1, keepdims=True), a)

    for li, b in enumerate(HG_LEVELS):
        gref = _rows_bcast(gc_ref, dst, 2 * b, b - 1, c)
        upper = (row % (2 * b)) >= b
        qa = jnp.where(upper, qf * jnp.exp2(gcum - gref), 0.0).astype(BF16)
        kb = jnp.where(upper, 0.0, kk * jnp.exp2(gref - gcum)).astype(BF16)
        a = a + jnp.where(lvl_mask[li], _dot_nt(qa, kb), 0.0)
    a_ref[ds, :] = a.astype(BF16)

    qd_ref[ds, :] = (qf * jnp.exp2(gcum)).astype(BF16)
    k_dec = (kk * jnp.exp2(g_last - gcum)).astype(BF16)
    kv_ref[slot] = _dot_tn(i_ref[sl, :], k_dec)
    dec_ref[pl.ds(slot, 1), :] = jnp.exp2(g_last)


def _hgrn(proj, lb, gain, *, batch, seq, t_rows):
    m = batch * seq
    t_rows = min(t_rows, seq)
    nt = seq // t_rows
    h = HG_HEADS
    grows = HG_GROUP * HG_CHUNK
    assert t_rows % grows == 0

    def spec(off):
        return pl.BlockSpec((t_rows, 128), lambda b, hh, t: (b * nt + t, off + hh))

    return pl.pallas_call(
        functools.partial(_hgrn_kernel, t_rows=t_rows),
        grid=(batch, h, nt),
        in_specs=[spec(0), spec(h), spec(2 * h), spec(3 * h),
                  pl.BlockSpec((1, 128), lambda b, hh, t: (0, hh)),
                  pl.BlockSpec((1, 128), lambda b, hh, t: (0, 0))],
        out_specs=pl.BlockSpec((t_rows, 128), lambda b, hh, t: (b * nt + t, hh)),
        out_shape=jax.ShapeDtypeStruct((m, h * HG_DV), BF16),
        scratch_shapes=[
            pltpu.VMEM((HG_DV, HG_DK), F32),
            pltpu.VMEM((2 * grows, 128), F32),
            pltpu.VMEM((2 * grows, 128), F32),
            pltpu.VMEM((2 * grows, HG_CHUNK), BF16),
            pltpu.VMEM((2 * grows, 128), BF16),
            pltpu.VMEM((2 * HG_GROUP, HG_DV, HG_DK), F32),
            pltpu.VMEM((2 * HG_GROUP, 128), F32),
        ],
        compiler_params=_cparams(("parallel", "parallel", "arbitrary")),
        name="hgrn2",
    )(proj, proj, proj, proj, lb, gain)


def _swa_kernel(sink_ref, q_ref, kp_ref, kc_ref, vp_ref, vc_ref, bias_ref, o_ref):
    blk = SWA_BLOCK
    dh = SWA_HEAD_DIM
    first = pl.program_id(1) == 0
    rt = lax.broadcasted_iota(jnp.int32, (blk, blk), 0)
    ct = lax.broadcasted_iota(jnp.int32, (blk, blk), 1)
    from_cur = ct <= rt
    no_prev = first & (ct > rt)
    scale = dh ** -0.5
    outs = []
    for h in range(SWA_KV_HEADS):
        ks = slice(h * dh, (h + 1) * dh)
        kcat = jnp.concatenate([kp_ref[:, ks], kc_ref[:, ks]], axis=0)
        vcat = jnp.concatenate([vp_ref[:, ks], vc_ref[:, ks]], axis=0)
        for g in range(SWA_GROUP):
            hq = h * SWA_GROUP + g
            q = q_ref[:, hq * dh:(hq + 1) * dh]
            s2 = _dot_nt(q, kcat)
            s = jnp.where(from_cur, s2[:, blk:], s2[:, :blk]) * scale + bias_ref[hq]
            s = jnp.where(no_prev, NEG_BIG, s)
            sink = sink_ref[hq]
            mx = jnp.maximum(jnp.max(s, axis=-1, keepdims=True), sink)
            p = jnp.exp(s - mx)
            den = jnp.sum(p, axis=-1, keepdims=True) + jnp.exp(sink - mx)
            p2 = jnp.concatenate([jnp.where(from_cur, 0.0, p), jnp.where(from_cur, p, 0.0)], axis=-1)
            o = _dot(p2.astype(BF16), vcat) / den
            outs.append(o)
    o_ref[...] = jnp.concatenate(outs, axis=-1).astype(o_ref.dtype)


def _swa(proj, sinks, bias, *, batch, seq, q_blk, k_blk, v_blk):
    m = batch * seq
    nb = seq // SWA_BLOCK
    blk = SWA_BLOCK
    kw = SWA_KV_HEADS * SWA_HEAD_DIM
    qw = SWA_HEADS * SWA_HEAD_DIM

    def cur(cb):
        return lambda b, n, s: (b * nb + n, cb)

    def prev(cb):
        return lambda b, n, s: (b * nb + jnp.maximum(n - 1, 0), cb)

    grid_spec = pltpu.PrefetchScalarGridSpec(
        num_scalar_prefetch=1,
        grid=(batch, nb),
        in_specs=[
            pl.BlockSpec((blk, qw), cur(q_blk)),
            pl.BlockSpec((blk, kw), prev(k_blk)),
            pl.BlockSpec((blk, kw), cur(k_blk)),
            pl.BlockSpec((blk, kw), prev(v_blk)),
            pl.BlockSpec((blk, kw), cur(v_blk)),
            pl.BlockSpec((SWA_HEADS, blk, blk), lambda b, n, s: (0, 0, 0)),
        ],
        out_specs=pl.BlockSpec((blk, qw), lambda b, n, s: (b * nb + n, 0)),
    )
    return pl.pallas_call(
        _swa_kernel,
        grid_spec=grid_spec,
        out_shape=jax.ShapeDtypeStruct((m, qw), BF16),
        compiler_params=_cparams(("parallel", "arbitrary")),
        name="swa",
    )(sinks, proj, proj, proj, proj, proj, bias)


def _t5_bias_table(t5_table):
    t_loc = np.arange(SWA_BLOCK, dtype=np.int32)[:, None]
    s_loc = np.arange(2 * SWA_BLOCK, dtype=np.int32)[None, :]
    dist = t_loc + SWA_BLOCK - s_loc
    n = np.maximum(dist, 0)
    nf = np.maximum(n, 1).astype(np.float32)
    large = T5_MAX_EXACT + (np.log(nf / np.float32(T5_MAX_EXACT)) / np.float32(math.log(T5_MAX_DIST / T5_MAX_EXACT))
                            * np.float32(T5_BUCKETS - T5_MAX_EXACT)).astype(np.int32)
    large = np.minimum(large, T5_BUCKETS - 1)
    bucket = np.where(n < T5_MAX_EXACT, n, large).astype(np.int32)
    in_band = (dist >= 0) & (dist < SWA_WINDOW)
    bias = t5_table.astype(F32)[jnp.asarray(bucket)]
    bias = jnp.where(jnp.asarray(in_band)[..., None], bias, NEG_BIG).transpose(2, 0, 1)
    from_cur = jnp.asarray(np.tril(np.ones((SWA_BLOCK, SWA_BLOCK), dtype=bool)))
    return jnp.where(from_cur[None], bias[:, :, SWA_BLOCK:], bias[:, :, :SWA_BLOCK])


def _merge_rows(x_ref, oa_ref, ob_ref, ga_ref, gb_ref, bg_ref, wa_ref, wb_ref, wm_ref):
    a = _dot(oa_ref[...], wa_ref[...])
    b = _dot(ob_ref[...], wb_ref[...])
    bg = bg_ref[...]
    za = ga_ref[...].astype(F32) + bg[0:1]
    zb = gb_ref[...].astype(F32) + bg[1:2]
    merged = a / (1.0 + jnp.exp(-za)) + b / (1.0 + jnp.exp(-zb))
    return x_ref[...] + _dot(merged.astype(BF16), wm_ref[...])


def _xa_kernel(x_ref, oa_ref, ob_ref, ga_ref, gb_ref, bg_ref, wa_ref, wb_ref, wm_ref,
               gxa_ref, wq_ref, km_ref, vm_ref, wo_ref, gffn_ref, wr_ref, br_ref,
               x2_ref, hn_ref, idx_ref, gate_ref):
    x1 = _merge_rows(x_ref, oa_ref, ob_ref, ga_ref, gb_ref, bg_ref, wa_ref, wb_ref, wm_ref)
    hx = _rms(x1, gxa_ref[...]).astype(BF16)
    q = _dot(hx, wq_ref[...]).astype(BF16)
    scale = XA_HEAD_DIM ** -0.5
    outs = []
    for h in range(XA_HEADS):
        sl = slice(h * XA_HEAD_DIM, (h + 1) * XA_HEAD_DIM)
        s = _dot_nt(q[:, sl], km_ref[:, sl]) * scale
        mx = jnp.max(s, axis=-1, keepdims=True)
        p = jnp.exp(s - mx)
        den = jnp.sum(p, axis=-1, keepdims=True)
        outs.append((_dot(p.astype(BF16), vm_ref[:, sl]) / den).astype(BF16))
    o = jnp.concatenate(outs, axis=-1)
    x2 = x1 + _dot(o, wo_ref[...])
    x2_ref[...] = x2
    hn = _rms(x2, gffn_ref[...])
    hn_ref[...] = _pack_rows(hn)
    hn_hi = hn.astype(BF16)
    hn_lo = (hn - hn_hi.astype(F32)).astype(BF16)
    big = _dot_nt(wr_ref[...], hn_hi)
    l = big[:N_EXPERTS] + big[N_EXPERTS:] + _dot_nt(wr_ref[:N_EXPERTS, :], hn_lo) + br_ref[...]
    eid = lax.broadcasted_iota(jnp.int32, l.shape, 0)
    vals = []
    idxs = []
    for k in range(TOP_K):
        mk = jnp.max(l, axis=0, keepdims=True)
        ik = jnp.min(jnp.where(l == mk, eid, N_EXPERTS), axis=0, keepdims=True)
        vals.append(mk)
        idxs.append(ik)
        l = jnp.where(eid == ik, -jnp.inf, l)
    ev = [jnp.exp(v - vals[0]) for v in vals]
    inv_den = 1.0 / (ev[0] + ev[1] + ev[2] + ev[3])
    gate_ref[...] = jnp.concatenate([e_k * inv_den for e_k in ev], axis=0)
    idx_ref[...] = jnp.concatenate(idxs, axis=0)


def _xa(x, o_a, o_b, proj, b_gates, wa, wb, wm, gxa, wq, kvmem, wo, gffn, wr, br, *,
        batch, seq, mem_len, tm, ga_blk, gb_blk):
    m, d = x.shape
    tm = min(tm, seq)
    nt = seq // tm
    row = lambda i: (i, 0)
    full = lambda i: (0, 0)
    return pl.pallas_call(
        _xa_kernel,
        grid=(m // tm,),
        in_specs=[
            pl.BlockSpec((tm, d), row),
            pl.BlockSpec((tm, d), row),
            pl.BlockSpec((tm, d), row),
            pl.BlockSpec((tm, d), lambda i: (i, ga_blk)),
            pl.BlockSpec((tm, d), lambda i: (i, gb_blk)),
            pl.BlockSpec((2, d), full),
            pl.BlockSpec((d, d), full),
            pl.BlockSpec((d, d), full),
            pl.BlockSpec((d, d), full),
            pl.BlockSpec((1, d), full),
            pl.BlockSpec((d, d), full),
            pl.BlockSpec((mem_len, d), lambda i: (i // nt, 0)),
            pl.BlockSpec((mem_len, d), lambda i: (i // nt, 1)),
            pl.BlockSpec((d, d), full),
            pl.BlockSpec((1, d), full),
            pl.BlockSpec((2 * N_EXPERTS, d), full),
            pl.BlockSpec((N_EXPERTS, 1), full),
        ],
        out_specs=[
            pl.BlockSpec((tm, d), row),
            pl.BlockSpec((tm, HALF), row),
            pl.BlockSpec((TOP_K, tm), lambda i: (0, i)),
            pl.BlockSpec((TOP_K, tm), lambda i: (0, i)),
        ],
        out_shape=[
            jax.ShapeDtypeStruct((m, d), F32),
            jax.ShapeDtypeStruct((m, HALF), jnp.uint32),
            jax.ShapeDtypeStruct((TOP_K, m), jnp.int32),
            jax.ShapeDtypeStruct((TOP_K, m), F32),
        ],
        compiler_params=_cparams(("parallel",)),
        name="merge_xattn_router",
    )(x, o_a, o_b, proj, proj, b_gates, wa, wb, wm, gxa, wq, kvmem, kvmem, wo, gffn, wr, br)


def _moe_kernel(bexp_ref, x_ref, wgu_ref, bgu_ref, wd_ref, bd_ref, y_ref, wgu_bf, wd_bf):
    i = pl.program_id(0)

    @pl.when((i == 0) | (bexp_ref[i] != bexp_ref[jnp.maximum(i - 1, 0)]))
    def _():
        wgu_bf[...] = wgu_ref[0].astype(BF16)
        wd_bf[...] = wd_ref[0].astype(BF16)

    n_used = bexp_ref[pl.num_programs(0)]

    @pl.when(i < n_used)
    def _():
        xb = _unpack_rows(x_ref[...]).astype(BF16)
        gu = _dot(xb, wgu_bf[...]) + bgu_ref[0]
        x_glu = jnp.minimum(gu[:, :D_FF], SWIGLU_LIMIT)
        x_lin = jnp.clip(gu[:, D_FF:], -SWIGLU_LIMIT, SWIGLU_LIMIT)
        act = x_glu / (1.0 + jnp.exp(-SWIGLU_ALPHA * x_glu)) * (x_lin + 1.0)
        y_ref[...] = _pack_rows(_dot(act.astype(BF16), wd_bf[...]) + bd_ref[0])

    @pl.when(i >= n_used)
    def _():
        y_ref[...] = jnp.zeros_like(y_ref)


def _moe(block_expert, x_sorted, wgu, bgu, wd, bd):
    rows = x_sorted.shape[0]
    d = D_MODEL
    n_blocks = rows // MOE_BLOCK
    grid_spec = pltpu.PrefetchScalarGridSpec(
        num_scalar_prefetch=1,
        grid=(n_blocks,),
        in_specs=[
            pl.BlockSpec((MOE_BLOCK, HALF), lambda i, be: (i, 0)),
            pl.BlockSpec((1, d, 2 * D_FF), lambda i, be: (be[i], 0, 0)),
            pl.BlockSpec((1, 1, 2 * D_FF), lambda i, be: (be[i], 0, 0)),
            pl.BlockSpec((1, D_FF, d), lambda i, be: (be[i], 0, 0)),
            pl.BlockSpec((1, 1, d), lambda i, be: (be[i], 0, 0)),
        ],
        out_specs=pl.BlockSpec((MOE_BLOCK, HALF), lambda i, be: (i, 0)),
        scratch_shapes=[
            pltpu.VMEM((d, 2 * D_FF), BF16),
            pltpu.VMEM((D_FF, d), BF16),
        ],
    )
    return pl.pallas_call(
        _moe_kernel,
        grid_spec=grid_spec,
        out_shape=jax.ShapeDtypeStruct((rows, HALF), jnp.uint32),
        compiler_params=_cparams(("arbitrary",)),
        name="moe_experts",
    )(block_expert, x_sorted, wgu, bgu, wd, bd)


def _sc_gather(table, idx):
    b = idx.shape[0]
    d = table.shape[1]
    per_worker = b // SC_WORKERS
    n_chunks = per_worker // SC_CHUNK
    assert b % SC_WORKERS == 0 and per_worker % (2 * SC_CHUNK) == 0, (b, SC_WORKERS, SC_CHUNK)
    mesh = plsc.VectorSubcoreMesh(core_axis_name="c", subcore_axis_name="s")

    @functools.partial(
        pl.kernel, mesh=mesh,
        out_type=jax.ShapeDtypeStruct((b, d), table.dtype),
        scratch_types=[pltpu.VMEM((per_worker,), jnp.int32),
                       pltpu.VMEM((2, SC_CHUNK, d), table.dtype),
                       pltpu.SemaphoreType.DMA((2,)),
                       pltpu.SemaphoreType.DMA((2,))],
        name="sc_row_gather",
    )
    def gather_kernel(table_hbm, idx_hbm, out_hbm, idx_v, rows_v, g_sem, w_sem):
        worker = lax.axis_index("s") * SC_CORES + lax.axis_index("c")
        base = worker * per_worker
        pltpu.sync_copy(idx_hbm.at[pl.ds(base, per_worker)], idx_v)

        def gather(t, slot):
            return pltpu.make_async_copy(table_hbm.at[idx_v.at[pl.ds(t * SC_CHUNK, SC_CHUNK)]],
                                         rows_v.at[slot], g_sem.at[slot])

        def write(t, slot):
            return pltpu.make_async_copy(rows_v.at[slot], out_hbm.at[pl.ds(base + t * SC_CHUNK, SC_CHUNK)],
                                         w_sem.at[slot])

        gather(0, 0).start()

        def body(t2, carry):
            for slot in range(2):
                t = t2 * 2 + slot
                gather(t, slot).wait()

                @pl.when(t + 1 < n_chunks)
                def _():
                    @pl.when(t >= 1)
                    def _():
                        write(t - 1, 1 - slot).wait()
                    gather(t + 1, 1 - slot).start()
                write(t, slot).start()
            return carry
        lax.fori_loop(0, n_chunks // 2, body, 0)
        write(n_chunks - 2, 0).wait()
        write(n_chunks - 1, 1).wait()

    return gather_kernel(table, idx)


def _combine_kernel(x_ref, y0_ref, y1_ref, y2_ref, y3_ref, gate_ref, gain_ref, o_ref):
    g = gate_ref[...]
    acc = x_ref[...]
    for k, y_ref in enumerate((y0_ref, y1_ref, y2_ref, y3_ref)):
        acc = acc + g[:, k:k + 1] * _unpack_rows(y_ref[...])
    o_ref[...] = _rms(acc, gain_ref[...])


def _combine(x2, y, gates, gain, *, tm):
    m, d = x2.shape
    tm = min(tm, m)
    kb = m // tm
    row = lambda i: (i, 0)
    return pl.pallas_call(
        _combine_kernel,
        grid=(m // tm,),
        in_specs=[pl.BlockSpec((tm, d), row)]
        + [pl.BlockSpec((tm, HALF), functools.partial(lambda i, k: (k * kb + i, 0), k=k)) for k in range(TOP_K)]
        + [pl.BlockSpec((tm, TOP_K), row), pl.BlockSpec((1, d), lambda i: (0, 0))],
        out_specs=pl.BlockSpec((tm, d), row),
        out_shape=jax.ShapeDtypeStruct((m, d), F32),
        compiler_params=_cparams(("parallel",)),
        name="combine_norm",
    )(x2, y, y, y, y, gates, gain)


def _routing_tables(top_idx, n_tok):
    n_assign = n_tok * TOP_K
    e_flat = top_idx.reshape(-1)
    assert N_EXPERTS * n_assign < 2 ** 31
    order = jnp.sort(e_flat * n_assign + jnp.arange(n_assign, dtype=jnp.int32)) % n_assign
    experts = jnp.arange(N_EXPERTS, dtype=jnp.int32)
    counts = jnp.sum((e_flat[:, None] == experts[None, :]).astype(jnp.int32), axis=0)
    starts = jnp.cumsum(counts) - counts
    padded = (counts + MOE_BLOCK - 1) // MOE_BLOCK * MOE_BLOCK
    pad_ends = jnp.cumsum(padded)
    pad_starts = pad_ends - padded
    cap = -(-n_assign // MOE_BLOCK) * MOE_BLOCK + N_EXPERTS * MOE_BLOCK
    n_blocks = cap // MOE_BLOCK
    block_start = jnp.arange(n_blocks, dtype=jnp.int32) * MOE_BLOCK
    block_expert = jnp.minimum(
        jnp.sum((pad_ends[None, :] <= block_start[:, None]).astype(jnp.int32), axis=1), N_EXPERTS - 1)
    r = block_start[:, None] + jnp.arange(MOE_BLOCK, dtype=jnp.int32)[None, :]
    pos = r - pad_starts[block_expert][:, None]
    valid = pos < counts[block_expert][:, None]
    src = jnp.clip(starts[block_expert][:, None] + pos, 0, n_assign - 1)
    tok = jnp.where(valid, order[src] % n_tok, r % n_tok).reshape(-1)
    sorted_pos = jnp.argsort(order).astype(jnp.int32)
    row_of = sorted_pos + (pad_starts - starts)[e_flat]
    n_used = pad_ends[-1:] // MOE_BLOCK
    return tok, row_of, jnp.concatenate([block_expert, n_used]).astype(jnp.int32)


def kernel(x, mem, t5_table, lb_logits, norm_mix, w_in, b_gates, hg_norm_gain, swa_sinks, w_branch_a, w_branch_b, w_mix_out, norm_xa, xa_w_q, xa_w_k, xa_w_v, xa_w_o, norm_ffn, w_router, b_router, w_gate_up, b_gate_up, w_down, b_down, norm_final):
    batch, seq, d = x.shape
    mem_len = mem.shape[1]
    m = batch * seq
    depth = norm_mix.shape[0]
    assert depth == 1, "the final norm is fused into the layer's last kernel"
    lower_bounds = jnp.cumsum(jax.nn.softmax(lb_logits.astype(F32), axis=0), axis=0)
    bias = _t5_bias_table(t5_table)
    xt = x.reshape(m, d)
    for l in range(depth):
        w = w_in[l]
        cuts = np.cumsum([0, 1024, 1024, 1024, 1024, 1024, 256, 256, 1024, 1024])
        seg = [w[:, cuts[k]:cuts[k + 1]] for k in range(9)]
        w_perm = jnp.concatenate(seg[0:5] + seg[7:9] + seg[5:7], axis=1).astype(BF16)
        proj = _norm_matmul(xt, norm_mix[l][None], w_perm, use_norm=True, tm=2048, tn=1536, out_dtype=BF16)
        o_a = _hgrn(proj, lower_bounds[l][None], hg_norm_gain[l][None], batch=batch, seq=seq, t_rows=4096)
        o_b = _swa(proj, swa_sinks[l], bias, batch=batch, seq=seq, q_blk=4, k_blk=28, v_blk=29)
        w_kv = jnp.concatenate([xa_w_k[l], xa_w_v[l]], axis=1).astype(BF16)
        kvmem = _norm_matmul(mem.reshape(batch * mem_len, d), norm_xa[l][None], w_kv, use_norm=False,
                             tm=1024, tn=1024, out_dtype=BF16)
        wr_hi = w_router[l].astype(BF16)
        wr_lo = (w_router[l] - wr_hi.astype(F32)).astype(BF16)
        x2, hn, top_idx, gates = _xa(xt, o_a, o_b, proj, b_gates[l], w_branch_a[l].astype(BF16),
                                     w_branch_b[l].astype(BF16), w_mix_out[l].astype(BF16),
                                     norm_xa[l][None], xa_w_q[l].astype(BF16), kvmem, xa_w_o[l].astype(BF16),
                                     norm_ffn[l][None], jnp.concatenate([wr_hi, wr_lo], axis=1).T,
                                     b_router[l][:, None], batch=batch, seq=seq, mem_len=mem_len, tm=512,
                                     ga_blk=5, gb_blk=6)
        gates = gates.T
        tok, row_of, block_expert = _routing_tables(top_idx, m)
        x_sorted = _sc_gather(hn, tok)
        y_sorted = _moe(block_expert, x_sorted, w_gate_up[l], b_gate_up[l][:, None, :],
                        w_down[l], b_down[l][:, None, :])
        y = _sc_gather(y_sorted, row_of)
        xt = _combine(x2, y, gates, norm_final[None], tm=256)
    return xt.reshape(batch, seq, d)
```

```python
import functools
import math

import jax
import jax.numpy as jnp
import numpy as np
from jax import lax
from jax.experimental import pallas as pl
from jax.experimental.pallas import tpu as pltpu
from jax.experimental.pallas import tpu_sc as plsc

F32 = jnp.float32
BF16 = jnp.bfloat16

RMS_EPS = 1e-5
D_MODEL = 1024
HG_HEADS = 8
HG_DK = 128
HG_DV = 128
HG_CHUNK = 64
HG_LEVELS = (32, 16, 8)
HG_DIAG = 8
HG_GROUP = 4
LOG2E = 1.4426950408889634
SWA_HEADS = 16
SWA_KV_HEADS = 4
SWA_GROUP = 4
SWA_HEAD_DIM = 64
SWA_WINDOW = 128
SWA_BLOCK = 128
T5_BUCKETS = 32
T5_MAX_EXACT = 16
T5_MAX_DIST = 128
XA_HEADS = 4
XA_HEAD_DIM = 256
N_EXPERTS = 32
TOP_K = 4
D_FF = 1024
SWIGLU_ALPHA = 1.702
SWIGLU_LIMIT = 7.0
MOE_BLOCK = 512
SC_CORES = 2
SC_WORKERS = 32
SC_CHUNK = 64
COMBINE_CHUNKS = 2
HALF = D_MODEL // 2
NEG_BIG = -1e30

VMEM_LIMIT = 56 * 1024 * 1024


def _cparams(sem):
    return pltpu.CompilerParams(dimension_semantics=sem, vmem_limit_bytes=VMEM_LIMIT)


def _rms(xf, gain):
    return xf * lax.rsqrt(jnp.mean(xf * xf, axis=-1, keepdims=True) + RMS_EPS) * gain


def _dot(a, b):
    return jnp.dot(a, b, preferred_element_type=F32)


def _dot_nt(a, b):
    return lax.dot_general(a, b, (((1,), (1,)), ((), ())), preferred_element_type=F32)


def _dot_tn(a, b):
    return lax.dot_general(a, b, (((0,), (0,)), ((), ())), preferred_element_type=F32)


def _pack_rows(x):
    hi = lax.bitcast_convert_type(x[:, :HALF].astype(BF16).astype(F32), jnp.uint32)
    lo = lax.bitcast_convert_type(x[:, HALF:].astype(BF16).astype(F32), jnp.uint32)
    return hi | (lo >> 16)


def _unpack_rows(p):
    hi = lax.bitcast_convert_type(p & jnp.uint32(0xFFFF0000), F32)
    lo = lax.bitcast_convert_type(p << 16, F32)
    return jnp.concatenate([hi, lo], axis=-1)


def _norm_matmul_kernel(x_ref, g_ref, w_ref, o_ref, h_ref, *, use_norm):
    @pl.when(pl.program_id(1) == 0)
    def _():
        xf = x_ref[...].astype(F32)
        if use_norm:
            xf = _rms(xf, g_ref[...])
        h_ref[...] = xf.astype(BF16)

    o_ref[...] = _dot(h_ref[...], w_ref[...]).astype(o_ref.dtype)


def _norm_matmul(x, gain, w, *, use_norm, tm, tn, out_dtype):
    m, k = x.shape
    n = w.shape[1]
    tm = min(tm, m)
    tn = min(tn, n)
    return pl.pallas_call(
        functools.partial(_norm_matmul_kernel, use_norm=use_norm),
        grid=(m // tm, n // tn),
        in_specs=[
            pl.BlockSpec((tm, k), lambda i, j: (i, 0)),
            pl.BlockSpec((1, k), lambda i, j: (0, 0)),
            pl.BlockSpec((k, tn), lambda i, j: (0, j)),
        ],
        out_specs=pl.BlockSpec((tm, tn), lambda i, j: (i, j)),
        out_shape=jax.ShapeDtypeStruct((m, n), out_dtype),
        scratch_shapes=[pltpu.VMEM((tm, k), BF16)],
        compiler_params=_cparams(("parallel", "arbitrary")),
        name="norm_matmul",
    )(x, gain, w)


def _rows_bcast(ref, base, group, row, n):
    parts = [jnp.broadcast_to(ref[pl.ds(base + g0 + row, 1), :], (group, ref.shape[1]))
             for g0 in range(0, n, group)]
    return jnp.concatenate(parts, axis=0)


def _hgrn_kernel(q_ref, f_ref, i_ref, g_ref, lb_ref, gain_ref, o_ref,
                 st_ref, kk_ref, gc_ref, a_ref, qd_ref, kv_ref, dec_ref, *, t_rows):
    c = HG_CHUNK
    d = HG_DIAG
    grows = HG_GROUP * c
    ngroups = t_rows // grows

    @pl.when(pl.program_id(2) == 0)
    def _():
        st_ref[...] = jnp.zeros_like(st_ref)

    lb = lb_ref[...]
    gain = gain_ref[...]
    row = lax.broadcasted_iota(jnp.int32, (c, 1), 0)
    col = lax.broadcasted_iota(jnp.int32, (c, c), 1)
    cd = col - (row // d) * d
    cd = jnp.where(cd <= row % d, cd, -1)
    lvl_mask = []
    for b in HG_LEVELS:
        rb = lax.broadcasted_iota(jnp.int32, (c, c), 0) // b
        lvl_mask.append(((rb - col // b) * 2 + (col // b) % 2) == 2)

    def rows(ref_row0):
        return pl.ds(pl.multiple_of(ref_row0, c), c)

    def phase_a(gi, buf):
        for ch in range(HG_GROUP):
            src = rows(gi * grows + ch * c)
            dst = buf * grows + ch * c
            _hgrn_front(q_ref, f_ref, i_ref, lb, row, cd, lvl_mask, kk_ref, gc_ref, a_ref, qd_ref, kv_ref,
                        dec_ref, src, dst, buf * HG_GROUP + ch)

    def phase_b(gi, buf, st):
        sts = []
        for ch in range(HG_GROUP):
            sts.append(st.astype(BF16))
            slot = buf * HG_GROUP + ch
            st = st * dec_ref[pl.ds(slot, 1), :] + kv_ref[slot]
        for ch in range(HG_GROUP):
            src = rows(gi * grows + ch * c)
            dst = rows(buf * grows + ch * c)
            o = _dot_nt(qd_ref[dst, :], sts[ch]) + _dot(a_ref[dst, :], i_ref[src, :])
            on = _rms(o, gain)
            gv = g_ref[src, :].astype(F32)
            o_ref[src, :] = (on * gv * (1.0 / (1.0 + jnp.exp(-gv)))).astype(o_ref.dtype)
        return st

    phase_a(0, 0)

    def body(k, st):
        st = phase_b(k, k % 2, st)
        phase_a(k + 1, (k + 1) % 2)
        return st

    st = lax.fori_loop(0, ngroups - 1, body, st_ref[...])
    st_ref[...] = phase_b(ngroups - 1, (ngroups - 1) % 2, st)


def _hgrn_front(q_ref, f_ref, i_ref, lb, row, cd, lvl_mask, kk_ref, gc_ref, a_ref, qd_ref, kv_ref, dec_ref,
                sl, dst, slot):
    c = HG_CHUNK
    d = HG_DIAG
    ds = pl.ds(pl.multiple_of(dst, c), c)
    ff = f_ref[sl, :].astype(F32)
    e = jnp.exp(-jnp.abs(ff))
    r = 1.0 / (1.0 + e)
    sig_pos = jnp.where(ff >= 0, r, e * r)
    sig_neg = jnp.where(ff >= 0, e * r, r)
    logf = jnp.log(lb + (1.0 - lb) * sig_pos)
    kk = (1.0 - lb) * sig_neg
    qv = q_ref[sl, :].astype(F32)
    qf = qv * (1.0 / (1.0 + jnp.exp(-qv)))

    gcum = logf * LOG2E
    sh = 1
    while sh < c:
        gcum = gcum + jnp.where(row >= sh, pltpu.roll(gcum, sh, 0), 0.0)
        sh *= 2
    kk_ref[ds, :] = kk
    gc_ref[ds, :] = gcum
    g_last = gcum[c - 1:c, :]

    a = jnp.zeros((c, c), F32)
    for j in range(d):
        gj = _rows_bcast(gc_ref, dst, d, j, c)
        kj = _rows_bcast(kk_ref, dst, d, j, c)
        p = qf * jnp.exp2(gcum - gj) * kj
        a = jnp.where(cd == j, jnp.sum(p, axis=-1, keepdims=True), a)

    for li, b in enumerate(HG_LEVELS):
        gref = _rows_bcast(gc_ref, dst, 2 * b, b - 1, c)
        upper = (row % (2 * b)) >= b
        qa = jnp.where(upper, qf * jnp.exp2(gcum - gref), 0.0).astype(BF16)
        kb = jnp.where(upper, 0.0, kk * jnp.exp2(gref - gcum)).astype(BF16)
        a = a + jnp.where(lvl_mask[li], _dot_nt(qa, kb), 0.0)
    a_ref[ds, :] = a.astype(BF16)

    qd_ref[ds, :] = (qf * jnp.exp2(gcum)).astype(BF16)
    k_dec = (kk * jnp.exp2(g_last - gcum)).astype(BF16)
    kv_ref[slot] = _dot_tn(i_ref[sl, :], k_dec)
    dec_ref[pl.ds(slot, 1), :] = jnp.exp2(g_last)


def _hgrn(proj, lb, gain, *, batch, seq, t_rows):
    m = batch * seq
    t_rows = min(t_rows, seq)
    nt = seq // t_rows
    h = HG_HEADS
    grows = HG_GROUP * HG_CHUNK
    assert t_rows % grows == 0

    def spec(off):
        return pl.BlockSpec((t_rows, 128), lambda b, hh, t: (b * nt + t, off + hh))

    return pl.pallas_call(
        functools.partial(_hgrn_kernel, t_rows=t_rows),
        grid=(batch, h, nt),
        in_specs=[spec(0), spec(h), spec(2 * h), spec(3 * h),
                  pl.BlockSpec((1, 128), lambda b, hh, t: (0, hh)),
                  pl.BlockSpec((1, 128), lambda b, hh, t: (0, 0))],
        out_specs=pl.BlockSpec((t_rows, 128), lambda b, hh, t: (b * nt + t, hh)),
        out_shape=jax.ShapeDtypeStruct((m, h * HG_DV), BF16),
        scratch_shapes=[
            pltpu.VMEM((HG_DV, HG_DK), F32),
            pltpu.VMEM((2 * grows, 128), F32),
            pltpu.VMEM((2 * grows, 128), F32),
            pltpu.VMEM((2 * grows, HG_CHUNK), BF16),
            pltpu.VMEM((2 * grows, 128), BF16),
            pltpu.VMEM((2 * HG_GROUP, HG_DV, HG_DK), F32),
            pltpu.VMEM((2 * HG_GROUP, 128), F32),
        ],
        compiler_params=_cparams(("parallel", "parallel", "arbitrary")),
        name="hgrn2",
    )(proj, proj, proj, proj, lb, gain)


def _swa_kernel(sink_ref, q_ref, kp_ref, kc_ref, vp_ref, vc_ref, bias_ref, o_ref):
    blk = SWA_BLOCK
    dh = SWA_HEAD_DIM
    first = pl.program_id(1) == 0
    rt = lax.broadcasted_iota(jnp.int32, (blk, blk), 0)
    ct = lax.broadcasted_iota(jnp.int32, (blk, blk), 1)
    from_cur = ct <= rt
    no_prev = first & (ct > rt)
    scale = dh ** -0.5
    outs = []
    for h in range(SWA_KV_HEADS):
        ks = slice(h * dh, (h + 1) * dh)
        kcat = jnp.concatenate([kp_ref[:, ks], kc_ref[:, ks]], axis=0)
        vcat = jnp.concatenate([vp_ref[:, ks], vc_ref[:, ks]], axis=0)
        for g in range(SWA_GROUP):
            hq = h * SWA_GROUP + g
            q = q_ref[:, hq * dh:(hq + 1) * dh]
            s2 = _dot_nt(q, kcat)
            s = jnp.where(from_cur, s2[:, blk:], s2[:, :blk]) * scale + bias_ref[hq]
            s = jnp.where(no_prev, NEG_BIG, s)
            sink = sink_ref[hq]
            mx = jnp.maximum(jnp.max(s, axis=-1, keepdims=True), sink)
            p = jnp.exp(s - mx)
            den = jnp.sum(p, axis=-1, keepdims=True) + jnp.exp(sink - mx)
            p2 = jnp.concatenate([jnp.where(from_cur, 0.0, p), jnp.where(from_cur, p, 0.0)], axis=-1)
            o = _dot(p2.astype(BF16), vcat) / den
            outs.append(o)
    o_ref[...] = jnp.concatenate(outs, axis=-1).astype(o_ref.dtype)


def _swa(proj, sinks, bias, *, batch, seq, q_blk, k_blk, v_blk):
    m = batch * seq
    nb = seq // SWA_BLOCK
    blk = SWA_BLOCK
    kw = SWA_KV_HEADS * SWA_HEAD_DIM
    qw = SWA_HEADS * SWA_HEAD_DIM

    def cur(cb):
        return lambda b, n, s: (b * nb + n, cb)

    def prev(cb):
        return lambda b, n, s: (b * nb + jnp.maximum(n - 1, 0), cb)

    grid_spec = pltpu.PrefetchScalarGridSpec(
        num_scalar_prefetch=1,
        grid=(batch, nb),
        in_specs=[
            pl.BlockSpec((blk, qw), cur(q_blk)),
            pl.BlockSpec((blk, kw), prev(k_blk)),
            pl.BlockSpec((blk, kw), cur(k_blk)),
            pl.BlockSpec((blk, kw), prev(v_blk)),
            pl.BlockSpec((blk, kw), cur(v_blk)),
            pl.BlockSpec((SWA_HEADS, blk, blk), lambda b, n, s: (0, 0, 0)),
        ],
        out_specs=pl.BlockSpec((blk, qw), lambda b, n, s: (b * nb + n, 0)),
    )
    return pl.pallas_call(
        _swa_kernel,
        grid_spec=grid_spec,
        out_shape=jax.ShapeDtypeStruct((m, qw), BF16),
        compiler_params=_cparams(("parallel", "arbitrary")),
        name="swa",
    )(sinks, proj, proj, proj, proj, proj, bias)


def _t5_bias_table(t5_table):
    t_loc = np.arange(SWA_BLOCK, dtype=np.int32)[:, None]
    s_loc = np.arange(2 * SWA_BLOCK, dtype=np.int32)[None, :]
    dist = t_loc + SWA_BLOCK - s_loc
    n = np.maximum(dist, 0)
    nf = np.maximum(n, 1).astype(np.float32)
    large = T5_MAX_EXACT + (np.log(nf / np.float32(T5_MAX_EXACT)) / np.float32(math.log(T5_MAX_DIST / T5_MAX_EXACT))
                            * np.float32(T5_BUCKETS - T5_MAX_EXACT)).astype(np.int32)
    large = np.minimum(large, T5_BUCKETS - 1)
    bucket = np.where(n < T5_MAX_EXACT, n, large).astype(np.int32)
    in_band = (dist >= 0) & (dist < SWA_WINDOW)
    bias = t5_table.astype(F32)[jnp.asarray(bucket)]
    bias = jnp.where(jnp.asarray(in_band)[..., None], bias, NEG_BIG).transpose(2, 0, 1)
    from_cur = jnp.asarray(np.tril(np.ones((SWA_BLOCK, SWA_BLOCK), dtype=bool)))
    return jnp.where(from_cur[None], bias[:, :, SWA_BLOCK:], bias[:, :, :SWA_BLOCK])


def _merge_rows(x_ref, oa_ref, ob_ref, ga_ref, gb_ref, bg_ref, wa_ref, wb_ref, wm_ref):
    a = _dot(oa_ref[...], wa_ref[...])
    b = _dot(ob_ref[...], wb_ref[...])
    bg = bg_ref[...]
    za = ga_ref[...].astype(F32) + bg[0:1]
    zb = gb_ref[...].astype(F32) + bg[1:2]
    merged = a / (1.0 + jnp.exp(-za)) + b / (1.0 + jnp.exp(-zb))
    return x_ref[...] + _dot(merged.astype(BF16), wm_ref[...])


def _xa_kernel(x_ref, oa_ref, ob_ref, ga_ref, gb_ref, bg_ref, wa_ref, wb_ref, wm_ref,
               gxa_ref, wq_ref, km_ref, vm_ref, wo_ref, gffn_ref, wr_ref, br_ref,
               x2_ref, hn_ref, idx_ref, gate_ref, cnt_ref):
    x1 = _merge_rows(x_ref, oa_ref, ob_ref, ga_ref, gb_ref, bg_ref, wa_ref, wb_ref, wm_ref)
    hx = _rms(x1, gxa_ref[...]).astype(BF16)
    q = _dot(hx, wq_ref[...]).astype(BF16)
    scale = XA_HEAD_DIM ** -0.5
    outs = []
    for h in range(XA_HEADS):
        sl = slice(h * XA_HEAD_DIM, (h + 1) * XA_HEAD_DIM)
        s = _dot_nt(q[:, sl], km_ref[:, sl]) * scale
        mx = jnp.max(s, axis=-1, keepdims=True)
        p = jnp.exp(s - mx)
        den = jnp.sum(p, axis=-1, keepdims=True)
        outs.append((_dot(p.astype(BF16), vm_ref[:, sl]) / den).astype(BF16))
    o = jnp.concatenate(outs, axis=-1)
    x2 = x1 + _dot(o, wo_ref[...])
    x2_ref[...] = x2
    hn = _rms(x2, gffn_ref[...])
    hn_ref[...] = _pack_rows(hn)
    hn_hi = hn.astype(BF16)
    hn_lo = (hn - hn_hi.astype(F32)).astype(BF16)
    big = _dot_nt(wr_ref[...], hn_hi)
    l = big[:N_EXPERTS] + big[N_EXPERTS:] + _dot_nt(wr_ref[:N_EXPERTS, :], hn_lo) + br_ref[...]
    eid = lax.broadcasted_iota(jnp.int32, l.shape, 0)
    vals = []
    idxs = []
    for k in range(TOP_K):
        mk = jnp.max(l, axis=0, keepdims=True)
        ik = jnp.min(jnp.where(l == mk, eid, N_EXPERTS), axis=0, keepdims=True)
        vals.append(mk)
        idxs.append(ik)
        l = jnp.where(eid == ik, -jnp.inf, l)
    ev = [jnp.exp(v - vals[0]) for v in vals]
    inv_den = 1.0 / (ev[0] + ev[1] + ev[2] + ev[3])
    gate_ref[...] = jnp.concatenate([e_k * inv_den for e_k in ev], axis=0)
    idx_ref[...] = jnp.concatenate(idxs, axis=0)
    hits = sum(jnp.where(eid == ik, 1.0, 0.0) for ik in idxs)
    cnt_ref[0] = jnp.sum(hits, axis=1, keepdims=True).astype(jnp.int32)


def _xa(x, o_a, o_b, proj, b_gates, wa, wb, wm, gxa, wq, kvmem, wo, gffn, wr, br, *,
        batch, seq, mem_len, tm, ga_blk, gb_blk):
    m, d = x.shape
    tm = min(tm, seq)
    nt = seq // tm
    row = lambda i: (i, 0)
    full = lambda i: (0, 0)
    return pl.pallas_call(
        _xa_kernel,
        grid=(m // tm,),
        in_specs=[
            pl.BlockSpec((tm, d), row),
            pl.BlockSpec((tm, d), row),
            pl.BlockSpec((tm, d), row),
            pl.BlockSpec((tm, d), lambda i: (i, ga_blk)),
            pl.BlockSpec((tm, d), lambda i: (i, gb_blk)),
            pl.BlockSpec((2, d), full),
            pl.BlockSpec((d, d), full),
            pl.BlockSpec((d, d), full),
            pl.BlockSpec((d, d), full),
            pl.BlockSpec((1, d), full),
            pl.BlockSpec((d, d), full),
            pl.BlockSpec((mem_len, d), lambda i: (i // nt, 0)),
            pl.BlockSpec((mem_len, d), lambda i: (i // nt, 1)),
            pl.BlockSpec((d, d), full),
            pl.BlockSpec((1, d), full),
            pl.BlockSpec((2 * N_EXPERTS, d), full),
            pl.BlockSpec((N_EXPERTS, 1), full),
        ],
        out_specs=[
            pl.BlockSpec((tm, d), row),
            pl.BlockSpec((tm, HALF), row),
            pl.BlockSpec((TOP_K, tm), lambda i: (0, i)),
            pl.BlockSpec((TOP_K, tm), lambda i: (0, i)),
            pl.BlockSpec((1, N_EXPERTS, 1), lambda i: (i, 0, 0)),
        ],
        out_shape=[
            jax.ShapeDtypeStruct((m, d), F32),
            jax.ShapeDtypeStruct((m, HALF), jnp.uint32),
            jax.ShapeDtypeStruct((TOP_K, m), jnp.int32),
            jax.ShapeDtypeStruct((TOP_K, m), F32),
            jax.ShapeDtypeStruct((m // tm, N_EXPERTS, 1), jnp.int32),
        ],
        compiler_params=_cparams(("parallel",)),
        name="merge_xattn_router",
    )(x, o_a, o_b, proj, proj, b_gates, wa, wb, wm, gxa, wq, kvmem, kvmem, wo, gffn, wr, br)


def _moe_kernel(bexp_ref, x_ref, wgu_ref, bgu_ref, wd_ref, bd_ref, y_ref, wgu_bf, wd_bf):
    i = pl.program_id(0)

    @pl.when((i == 0) | (bexp_ref[i] != bexp_ref[jnp.maximum(i - 1, 0)]))
    def _():
        wgu_bf[...] = wgu_ref[0].astype(BF16)
        wd_bf[...] = wd_ref[0].astype(BF16)

    n_used = bexp_ref[pl.num_programs(0)]

    @pl.when(i < n_used)
    def _():
        xb = _unpack_rows(x_ref[...]).astype(BF16)
        gu = _dot(xb, wgu_bf[...]) + bgu_ref[0]
        x_glu = jnp.minimum(gu[:, :D_FF], SWIGLU_LIMIT)
        x_lin = jnp.clip(gu[:, D_FF:], -SWIGLU_LIMIT, SWIGLU_LIMIT)
        act = x_glu / (1.0 + jnp.exp(-SWIGLU_ALPHA * x_glu)) * (x_lin + 1.0)
        y_ref[...] = _pack_rows(_dot(act.astype(BF16), wd_bf[...]) + bd_ref[0])

    @pl.when(i >= n_used)
    def _():
        y_ref[...] = jnp.zeros_like(y_ref)


def _moe(block_expert, x_sorted, wgu, bgu, wd, bd):
    rows = x_sorted.shape[0]
    d = D_MODEL
    n_blocks = rows // MOE_BLOCK
    grid_spec = pltpu.PrefetchScalarGridSpec(
        num_scalar_prefetch=1,
        grid=(n_blocks,),
        in_specs=[
            pl.BlockSpec((MOE_BLOCK, HALF), lambda i, be: (i, 0)),
            pl.BlockSpec((1, d, 2 * D_FF), lambda i, be: (be[i], 0, 0)),
            pl.BlockSpec((1, 1, 2 * D_FF), lambda i, be: (be[i], 0, 0)),
            pl.BlockSpec((1, D_FF, d), lambda i, be: (be[i], 0, 0)),
            pl.BlockSpec((1, 1, d), lambda i, be: (be[i], 0, 0)),
        ],
        out_specs=pl.BlockSpec((MOE_BLOCK, HALF), lambda i, be: (i, 0)),
        scratch_shapes=[
            pltpu.VMEM((d, 2 * D_FF), BF16),
            pltpu.VMEM((D_FF, d), BF16),
        ],
    )
    return pl.pallas_call(
        _moe_kernel,
        grid_spec=grid_spec,
        out_shape=jax.ShapeDtypeStruct((rows, HALF), jnp.uint32),
        compiler_params=_cparams(("arbitrary",)),
        name="moe_experts",
    )(block_expert, x_sorted, wgu, bgu, wd, bd)


def _sc_gather(table, idx):
    b = idx.shape[0]
    d = table.shape[1]
    per_worker = b // SC_WORKERS
    n_chunks = per_worker // SC_CHUNK
    assert b % SC_WORKERS == 0 and per_worker % (2 * SC_CHUNK) == 0, (b, SC_WORKERS, SC_CHUNK)
    mesh = plsc.VectorSubcoreMesh(core_axis_name="c", subcore_axis_name="s")

    @functools.partial(
        pl.kernel, mesh=mesh,
        out_type=jax.ShapeDtypeStruct((b, d), table.dtype),
        scratch_types=[pltpu.VMEM((per_worker,), jnp.int32),
                       pltpu.VMEM((2, SC_CHUNK, d), table.dtype),
                       pltpu.SemaphoreType.DMA((2,)),
                       pltpu.SemaphoreType.DMA((2,))],
        name="sc_row_gather",
    )
    def gather_kernel(table_hbm, idx_hbm, out_hbm, idx_v, rows_v, g_sem, w_sem):
        worker = lax.axis_index("s") * SC_CORES + lax.axis_index("c")
        base = worker * per_worker
        pltpu.sync_copy(idx_hbm.at[pl.ds(base, per_worker)], idx_v)

        def gather(t, slot):
            return pltpu.make_async_copy(table_hbm.at[idx_v.at[pl.ds(t * SC_CHUNK, SC_CHUNK)]],
                                         rows_v.at[slot], g_sem.at[slot])

        def write(t, slot):
            return pltpu.make_async_copy(rows_v.at[slot], out_hbm.at[pl.ds(base + t * SC_CHUNK, SC_CHUNK)],
                                         w_sem.at[slot])

        gather(0, 0).start()

        def body(t2, carry):
            for slot in range(2):
                t = t2 * 2 + slot
                gather(t, slot).wait()

                @pl.when(t + 1 < n_chunks)
                def _():
                    @pl.when(t >= 1)
                    def _():
                        write(t - 1, 1 - slot).wait()
                    gather(t + 1, 1 - slot).start()
                write(t, slot).start()
            return carry
        lax.fori_loop(0, n_chunks // 2, body, 0)
        write(n_chunks - 2, 0).wait()
        write(n_chunks - 1, 1).wait()

    return gather_kernel(table, idx)


def _combine_kernel(x_ref, y0_ref, y1_ref, y2_ref, y3_ref, gate_ref, gain_ref, *rest):
    o_ref = rest[-1]
    g = gate_ref[...]
    acc = x_ref[...]
    for k, y_ref in enumerate((y0_ref, y1_ref, y2_ref, y3_ref)):
        acc = acc + g[:, k:k + 1] * _unpack_rows(y_ref[...])
    o_ref[...] = _rms(acc, gain_ref[...])


def _combine(x2, y, gates, gain, prev, *, tm, chunk, n_chunks):
    m, d = x2.shape
    mc = m // n_chunks
    tm = min(tm, mc)
    kb = mc // tm
    off = chunk * kb
    row = lambda i: (off + i, 0)
    in_specs = ([pl.BlockSpec((tm, d), row)]
                + [pl.BlockSpec((tm, HALF), functools.partial(lambda i, k: (k * kb + i, 0), k=k))
                   for k in range(TOP_K)]
                + [pl.BlockSpec((tm, TOP_K), row), pl.BlockSpec((1, d), lambda i: (0, 0))])
    args = [x2, y, y, y, y, gates, gain]
    aliases = {}
    if prev is not None:
        in_specs.append(pl.BlockSpec(memory_space=pl.ANY))
        args.append(prev)
        aliases = {len(args) - 1: 0}
    return pl.pallas_call(
        _combine_kernel,
        grid=(kb,),
        in_specs=in_specs,
        out_specs=pl.BlockSpec((tm, d), row),
        out_shape=jax.ShapeDtypeStruct((m, d), F32),
        input_output_aliases=aliases,
        compiler_params=_cparams(("parallel",)),
        name="combine_norm",
    )(*args)


def _routing_tables(top_idx, counts, n_tok):
    n_assign = n_tok * TOP_K
    e_flat = top_idx.reshape(-1)
    assert N_EXPERTS * n_assign < 2 ** 31
    order = jnp.sort(e_flat * n_assign + jnp.arange(n_assign, dtype=jnp.int32)) % n_assign
    starts = jnp.cumsum(counts) - counts
    padded = (counts + MOE_BLOCK - 1) // MOE_BLOCK * MOE_BLOCK
    pad_ends = jnp.cumsum(padded)
    pad_starts = pad_ends - padded
    cap = -(-n_assign // MOE_BLOCK) * MOE_BLOCK + N_EXPERTS * MOE_BLOCK
    n_blocks = cap // MOE_BLOCK
    block_start = jnp.arange(n_blocks, dtype=jnp.int32) * MOE_BLOCK
    block_expert = jnp.minimum(
        jnp.sum((pad_ends[None, :] <= block_start[:, None]).astype(jnp.int32), axis=1), N_EXPERTS - 1)
    r = block_start[:, None] + jnp.arange(MOE_BLOCK, dtype=jnp.int32)[None, :]
    pos = r - pad_starts[block_expert][:, None]
    valid = pos < counts[block_expert][:, None]
    src = jnp.clip(starts[block_expert][:, None] + pos, 0, n_assign - 1)
    tok = jnp.where(valid, order[src] % n_tok, r % n_tok).reshape(-1)
    sorted_pos = jnp.argsort(order).astype(jnp.int32)
    row_of = sorted_pos + (pad_starts - starts)[e_flat]
    n_used = pad_ends[-1:] // MOE_BLOCK
    return tok, row_of, jnp.concatenate([block_expert, n_used]).astype(jnp.int32)


def kernel(x, mem, t5_table, lb_logits, norm_mix, w_in, b_gates, hg_norm_gain, swa_sinks, w_branch_a, w_branch_b, w_mix_out, norm_xa, xa_w_q, xa_w_k, xa_w_v, xa_w_o, norm_ffn, w_router, b_router, w_gate_up, b_gate_up, w_down, b_down, norm_final):
    batch, seq, d = x.shape
    mem_len = mem.shape[1]
    m = batch * seq
    depth = norm_mix.shape[0]
    assert depth == 1, "the final norm is fused into the layer's last kernel"
    lower_bounds = jnp.cumsum(jax.nn.softmax(lb_logits.astype(F32), axis=0), axis=0)
    bias = _t5_bias_table(t5_table)
    xt = x.reshape(m, d)
    for l in range(depth):
        w = w_in[l]
        cuts = np.cumsum([0, 1024, 1024, 1024, 1024, 1024, 256, 256, 1024, 1024])
        seg = [w[:, cuts[k]:cuts[k + 1]] for k in range(9)]
        w_perm = jnp.concatenate(seg[0:5] + seg[7:9] + seg[5:7], axis=1).astype(BF16)
        proj = _norm_matmul(xt, norm_mix[l][None], w_perm, use_norm=True, tm=2048, tn=1536, out_dtype=BF16)
        o_a = _hgrn(proj, lower_bounds[l][None], hg_norm_gain[l][None], batch=batch, seq=seq, t_rows=4096)
        o_b = _swa(proj, swa_sinks[l], bias, batch=batch, seq=seq, q_blk=4, k_blk=28, v_blk=29)
        w_kv = jnp.concatenate([xa_w_k[l], xa_w_v[l]], axis=1).astype(BF16)
        kvmem = _norm_matmul(mem.reshape(batch * mem_len, d), norm_xa[l][None], w_kv, use_norm=False,
                             tm=1024, tn=1024, out_dtype=BF16)
        wr_hi = w_router[l].astype(BF16)
        wr_lo = (w_router[l] - wr_hi.astype(F32)).astype(BF16)
        x2, hn, top_idx, gates, tile_counts = _xa(xt, o_a, o_b, proj, b_gates[l], w_branch_a[l].astype(BF16),
                                     w_branch_b[l].astype(BF16), w_mix_out[l].astype(BF16),
                                     norm_xa[l][None], xa_w_q[l].astype(BF16), kvmem, xa_w_o[l].astype(BF16),
                                     norm_ffn[l][None], jnp.concatenate([wr_hi, wr_lo], axis=1).T,
                                     b_router[l][:, None], batch=batch, seq=seq, mem_len=mem_len, tm=512,
                                     ga_blk=5, gb_blk=6)
        gates = gates.T
        tok, row_of, block_expert = _routing_tables(top_idx, jnp.sum(tile_counts[:, :, 0], axis=0), m)
        x_sorted = _sc_gather(hn, tok)
        y_sorted = _moe(block_expert, x_sorted, w_gate_up[l], b_gate_up[l][:, None, :],
                        w_down[l], b_down[l][:, None, :])
        rows_kt = row_of.reshape(TOP_K, m)
        mc = m // COMBINE_CHUNKS
        xt = None
        for c in range(COMBINE_CHUNKS):
            y = _sc_gather(y_sorted, rows_kt[:, c * mc:(c + 1) * mc].reshape(-1))
            xt = _combine(x2, y, gates, norm_final[None], xt, tm=256, chunk=c, n_chunks=COMBINE_CHUNKS)
    return xt.reshape(batch, seq, d)
```

```python
import functools
import math

import jax
import jax.numpy as jnp
import numpy as np
from jax import lax
from jax.experimental import pallas as pl
from jax.experimental.pallas import tpu as pltpu
from jax.experimental.pallas import tpu_sc as plsc

F32 = jnp.float32
BF16 = jnp.bfloat16

RMS_EPS = 1e-5
D_MODEL = 1024
HG_HEADS = 8
HG_DK = 128
HG_DV = 128
HG_CHUNK = 64
HG_LEVELS = (32, 16, 8)
HG_DIAG = 8
HG_GROUP = 4
LOG2E = 1.4426950408889634
SWA_HEADS = 16
SWA_KV_HEADS = 4
SWA_GROUP = 4
SWA_HEAD_DIM = 64
SWA_WINDOW = 128
SWA_BLOCK = 128
SWA_STEP_BLOCKS = 1
T5_BUCKETS = 32
T5_MAX_EXACT = 16
T5_MAX_DIST = 128
XA_HEADS = 4
XA_HEAD_DIM = 256
N_EXPERTS = 32
TOP_K = 4
D_FF = 1024
SWIGLU_ALPHA = 1.702
SWIGLU_LIMIT = 7.0
MOE_BLOCK = 512
SC_CORES = 2
SC_WORKERS = 32
SC_CHUNK = 64
COMBINE_CHUNKS = 4
HALF = D_MODEL // 2
NEG_BIG = -1e30

VMEM_LIMIT = 56 * 1024 * 1024


def _cparams(sem):
    return pltpu.CompilerParams(dimension_semantics=sem, vmem_limit_bytes=VMEM_LIMIT)


def _rms(xf, gain):
    return xf * lax.rsqrt(jnp.mean(xf * xf, axis=-1, keepdims=True) + RMS_EPS) * gain


def _dot(a, b):
    return jnp.dot(a, b, preferred_element_type=F32)


def _dot_nt(a, b):
    return lax.dot_general(a, b, (((1,), (1,)), ((), ())), preferred_element_type=F32)


def _dot_tn(a, b):
    return lax.dot_general(a, b, (((0,), (0,)), ((), ())), preferred_element_type=F32)


def _pack_rows(x):
    hi = lax.bitcast_convert_type(x[:, :HALF].astype(BF16).astype(F32), jnp.uint32)
    lo = lax.bitcast_convert_type(x[:, HALF:].astype(BF16).astype(F32), jnp.uint32)
    return hi | (lo >> 16)


def _unpack_rows(p):
    hi = lax.bitcast_convert_type(p & jnp.uint32(0xFFFF0000), F32)
    lo = lax.bitcast_convert_type(p << 16, F32)
    return jnp.concatenate([hi, lo], axis=-1)


def _norm_matmul_kernel(x_ref, g_ref, w_ref, o_ref, h_ref, *, use_norm):
    @pl.when(pl.program_id(1) == 0)
    def _():
        xf = x_ref[...].astype(F32)
        if use_norm:
            xf = _rms(xf, g_ref[...])
        h_ref[...] = xf.astype(BF16)

    o_ref[...] = _dot(h_ref[...], w_ref[...]).astype(o_ref.dtype)


def _norm_matmul(x, gain, w, *, use_norm, tm, tn, out_dtype):
    m, k = x.shape
    n = w.shape[1]
    tm = min(tm, m)
    tn = min(tn, n)
    return pl.pallas_call(
        functools.partial(_norm_matmul_kernel, use_norm=use_norm),
        grid=(m // tm, n // tn),
        in_specs=[
            pl.BlockSpec((tm, k), lambda i, j: (i, 0)),
            pl.BlockSpec((1, k), lambda i, j: (0, 0)),
            pl.BlockSpec((k, tn), lambda i, j: (0, j)),
        ],
        out_specs=pl.BlockSpec((tm, tn), lambda i, j: (i, j)),
        out_shape=jax.ShapeDtypeStruct((m, n), out_dtype),
        scratch_shapes=[pltpu.VMEM((tm, k), BF16)],
        compiler_params=_cparams(("parallel", "arbitrary")),
        name="norm_matmul",
    )(x, gain, w)


def _rows_bcast(ref, base, group, row, n):
    parts = [jnp.broadcast_to(ref[pl.ds(base + g0 + row, 1), :], (group, ref.shape[1]))
             for g0 in range(0, n, group)]
    return jnp.concatenate(parts, axis=0)


def _hgrn_kernel(q_ref, f_ref, i_ref, g_ref, lb_ref, gain_ref, o_ref,
                 st_ref, kk_ref, gc_ref, a_ref, qd_ref, kv_ref, dec_ref, *, t_rows):
    c = HG_CHUNK
    d = HG_DIAG
    grows = HG_GROUP * c
    ngroups = t_rows // grows

    @pl.when(pl.program_id(2) == 0)
    def _():
        st_ref[...] = jnp.zeros_like(st_ref)

    lb = lb_ref[...]
    gain = gain_ref[...]
    row = lax.broadcasted_iota(jnp.int32, (c, 1), 0)
    col = lax.broadcasted_iota(jnp.int32, (c, c), 1)
    cd = col - (row // d) * d
    cd = jnp.where(cd <= row % d, cd, -1)
    lvl_mask = []
    for b in HG_LEVELS:
        rb = lax.broadcasted_iota(jnp.int32, (c, c), 0) // b
        lvl_mask.append(((rb - col // b) * 2 + (col // b) % 2) == 2)

    def rows(ref_row0):
        return pl.ds(pl.multiple_of(ref_row0, c), c)

    def phase_a(gi, buf):
        for ch in range(HG_GROUP):
            src = rows(gi * grows + ch * c)
            dst = buf * grows + ch * c
            _hgrn_front(q_ref, f_ref, i_ref, lb, row, cd, lvl_mask, kk_ref, gc_ref, a_ref, qd_ref, kv_ref,
                        dec_ref, src, dst, buf * HG_GROUP + ch)

    def phase_b(gi, buf, st):
        sts = []
        for ch in range(HG_GROUP):
            sts.append(st.astype(BF16))
            slot = buf * HG_GROUP + ch
            st = st * dec_ref[pl.ds(slot, 1), :] + kv_ref[slot]
        for ch in range(HG_GROUP):
            src = rows(gi * grows + ch * c)
            dst = rows(buf * grows + ch * c)
            o = _dot_nt(qd_ref[dst, :], sts[ch]) + _dot(a_ref[dst, :], i_ref[src, :])
            on = _rms(o, gain)
            gv = g_ref[src, :].astype(F32)
            o_ref[src, :] = (on * gv * (1.0 / (1.0 + jnp.exp(-gv)))).astype(o_ref.dtype)
        return st

    phase_a(0, 0)

    def body(k, st):
        st = phase_b(k, k % 2, st)
        phase_a(k + 1, (k + 1) % 2)
        return st

    st = lax.fori_loop(0, ngroups - 1, body, st_ref[...])
    st_ref[...] = phase_b(ngroups - 1, (ngroups - 1) % 2, st)


def _hgrn_front(q_ref, f_ref, i_ref, lb, row, cd, lvl_mask, kk_ref, gc_ref, a_ref, qd_ref, kv_ref, dec_ref,
                sl, dst, slot):
    c = HG_CHUNK
    d = HG_DIAG
    ds = pl.ds(pl.multiple_of(dst, c), c)
    ff = f_ref[sl, :].astype(F32)
    e = jnp.exp(-jnp.abs(ff))
    r = 1.0 / (1.0 + e)
    sig_pos = jnp.where(ff >= 0, r, e * r)
    sig_neg = jnp.where(ff >= 0, e * r, r)
    logf = jnp.log(lb + (1.0 - lb) * sig_pos)
    kk = (1.0 - lb) * sig_neg
    qv = q_ref[sl, :].astype(F32)
    qf = qv * (1.0 / (1.0 + jnp.exp(-qv)))

    gcum = logf * LOG2E
    sh = 1
    while sh < c:
        gcum = gcum + jnp.where(row >= sh, pltpu.roll(gcum, sh, 0), 0.0)
        sh *= 2
    kk_ref[ds, :] = kk
    gc_ref[ds, :] = gcum
    g_last = gcum[c - 1:c, :]

    a = jnp.zeros((c, c), F32)
    for j in range(d):
        gj = _rows_bcast(gc_ref, dst, d, j, c)
        kj = _rows_bcast(kk_ref, dst, d, j, c)
        p = qf * jnp.exp2(gcum - gj) * kj
        a = jnp.where(cd == j, jnp.sum(p, axis=-1, keepdims=True), a)

    for li, b in enumerate(HG_LEVELS):
        gref = _rows_bcast(gc_ref, dst, 2 * b, b - 1, c)
        upper = (row % (2 * b)) >= b
        qa = jnp.where(upper, qf * jnp.exp2(gcum - gref), 0.0).astype(BF16)
        kb = jnp.where(upper, 0.0, kk * jnp.exp2(gref - gcum)).astype(BF16)
        a = a + jnp.where(lvl_mask[li], _dot_nt(qa, kb), 0.0)
    a_ref[ds, :] = a.astype(BF16)

    qd_ref[ds, :] = (qf * jnp.exp2(gcum)).astype(BF16)
    k_dec = (kk * jnp.exp2(g_last - gcum)).astype(BF16)
    kv_ref[slot] = _dot_tn(i_ref[sl, :], k_dec)
    dec_ref[pl.ds(slot, 1), :] = jnp.exp2(g_last)


def _hgrn(proj, lb, gain, *, batch, seq, t_rows):
    m = batch * seq
    t_rows = min(t_rows, seq)
    nt = seq // t_rows
    h = HG_HEADS
    grows = HG_GROUP * HG_CHUNK
    assert t_rows % grows == 0

    def spec(off):
        return pl.BlockSpec((t_rows, 128), lambda b, hh, t: (b * nt + t, off + hh))

    return pl.pallas_call(
        functools.partial(_hgrn_kernel, t_rows=t_rows),
        grid=(batch, h, nt),
        in_specs=[spec(0), spec(h), spec(2 * h), spec(3 * h),
                  pl.BlockSpec((1, 128), lambda b, hh, t: (0, hh)),
                  pl.BlockSpec((1, 128), lambda b, hh, t: (0, 0))],
        out_specs=pl.BlockSpec((t_rows, 128), lambda b, hh, t: (b * nt + t, hh)),
        out_shape=jax.ShapeDtypeStruct((m, h * HG_DV), BF16),
        scratch_shapes=[
            pltpu.VMEM((HG_DV, HG_DK), F32),
            pltpu.VMEM((2 * grows, 128), F32),
            pltpu.VMEM((2 * grows, 128), F32),
            pltpu.VMEM((2 * grows, HG_CHUNK), BF16),
            pltpu.VMEM((2 * grows, 128), BF16),
            pltpu.VMEM((2 * HG_GROUP, HG_DV, HG_DK), F32),
            pltpu.VMEM((2 * HG_GROUP, 128), F32),
        ],
        compiler_params=_cparams(("parallel", "parallel", "arbitrary")),
        name="hgrn2",
    )(proj, proj, proj, proj, lb, gain)


def _swa_kernel(sink_ref, q_ref, kp_ref, kc_ref, vp_ref, vc_ref, bias_ref, o_ref):
    blk = SWA_BLOCK
    dh = SWA_HEAD_DIM
    first = pl.program_id(1) == 0
    rt = lax.broadcasted_iota(jnp.int32, (blk, blk), 0)
    ct = lax.broadcasted_iota(jnp.int32, (blk, blk), 1)
    from_cur = ct <= rt
    no_prev = first & (ct > rt)
    scale = dh ** -0.5
    for sub in range(SWA_STEP_BLOCKS):
        rows = slice(sub * blk, (sub + 1) * blk)
        outs = []
        for h in range(SWA_KV_HEADS):
            ks = slice(h * dh, (h + 1) * dh)
            k_prev = kp_ref[:, ks] if sub == 0 else kc_ref[(sub - 1) * blk:sub * blk, ks]
            v_prev = vp_ref[:, ks] if sub == 0 else vc_ref[(sub - 1) * blk:sub * blk, ks]
            kcat = jnp.concatenate([k_prev, kc_ref[rows, ks]], axis=0)
            vcat = jnp.concatenate([v_prev, vc_ref[rows, ks]], axis=0)
            for g in range(SWA_GROUP):
                hq = h * SWA_GROUP + g
                q = q_ref[rows, hq * dh:(hq + 1) * dh]
                s2 = _dot_nt(q, kcat)
                s = jnp.where(from_cur, s2[:, blk:], s2[:, :blk]) * scale + bias_ref[hq]
                if sub == 0:
                    s = jnp.where(no_prev, NEG_BIG, s)
                sink = sink_ref[hq]
                mx = jnp.maximum(jnp.max(s, axis=-1, keepdims=True), sink)
                p = jnp.exp(s - mx)
                den = jnp.sum(p, axis=-1, keepdims=True) + jnp.exp(sink - mx)
                p2 = jnp.concatenate([jnp.where(from_cur, 0.0, p), jnp.where(from_cur, p, 0.0)], axis=-1)
                o = _dot(p2.astype(BF16), vcat) / den
                outs.append(o)
        o_ref[rows, :] = jnp.concatenate(outs, axis=-1).astype(o_ref.dtype)


def _swa(proj, sinks, bias, *, batch, seq, q_blk, k_blk, v_blk):
    m = batch * seq
    nb = seq // SWA_BLOCK
    blk = SWA_BLOCK
    kw = SWA_KV_HEADS * SWA_HEAD_DIM
    qw = SWA_HEADS * SWA_HEAD_DIM

    sb = SWA_STEP_BLOCKS
    ns = nb // sb
    assert nb % sb == 0

    def cur(cb):
        return lambda b, n, s: (b * ns + n, cb)

    def prev(cb):
        return lambda b, n, s: (b * nb + jnp.maximum(n * sb - 1, 0), cb)

    grid_spec = pltpu.PrefetchScalarGridSpec(
        num_scalar_prefetch=1,
        grid=(batch, ns),
        in_specs=[
            pl.BlockSpec((sb * blk, qw), cur(q_blk)),
            pl.BlockSpec((blk, kw), prev(k_blk)),
            pl.BlockSpec((sb * blk, kw), cur(k_blk)),
            pl.BlockSpec((blk, kw), prev(v_blk)),
            pl.BlockSpec((sb * blk, kw), cur(v_blk)),
            pl.BlockSpec((SWA_HEADS, blk, blk), lambda b, n, s: (0, 0, 0)),
        ],
        out_specs=pl.BlockSpec((sb * blk, qw), lambda b, n, s: (b * ns + n, 0)),
    )
    return pl.pallas_call(
        _swa_kernel,
        grid_spec=grid_spec,
        out_shape=jax.ShapeDtypeStruct((m, qw), BF16),
        compiler_params=_cparams(("parallel", "arbitrary")),
        name="swa",
    )(sinks, proj, proj, proj, proj, proj, bias)


def _t5_bias_table(t5_table):
    t_loc = np.arange(SWA_BLOCK, dtype=np.int32)[:, None]
    s_loc = np.arange(2 * SWA_BLOCK, dtype=np.int32)[None, :]
    dist = t_loc + SWA_BLOCK - s_loc
    n = np.maximum(dist, 0)
    nf = np.maximum(n, 1).astype(np.float32)
    large = T5_MAX_EXACT + (np.log(nf / np.float32(T5_MAX_EXACT)) / np.float32(math.log(T5_MAX_DIST / T5_MAX_EXACT))
                            * np.float32(T5_BUCKETS - T5_MAX_EXACT)).astype(np.int32)
    large = np.minimum(large, T5_BUCKETS - 1)
    bucket = np.where(n < T5_MAX_EXACT, n, large).astype(np.int32)
    in_band = (dist >= 0) & (dist < SWA_WINDOW)
    bias = t5_table.astype(F32)[jnp.asarray(bucket)]
    bias = jnp.where(jnp.asarray(in_band)[..., None], bias, NEG_BIG).transpose(2, 0, 1)
    from_cur = jnp.asarray(np.tril(np.ones((SWA_BLOCK, SWA_BLOCK), dtype=bool)))
    return jnp.where(from_cur[None], bias[:, :, SWA_BLOCK:], bias[:, :, :SWA_BLOCK])


def _merge_rows(x_ref, oa_ref, ob_ref, ga_ref, gb_ref, bg_ref, wa_ref, wb_ref, wm_ref):
    a = _dot(oa_ref[...], wa_ref[...])
    b = _dot(ob_ref[...], wb_ref[...])
    bg = bg_ref[...]
    za = ga_ref[...].astype(F32) + bg[0:1]
    zb = gb_ref[...].astype(F32) + bg[1:2]
    merged = a / (1.0 + jnp.exp(-za)) + b / (1.0 + jnp.exp(-zb))
    return x_ref[...] + _dot(merged.astype(BF16), wm_ref[...])


def _xa_kernel(x_ref, oa_ref, ob_ref, ga_ref, gb_ref, bg_ref, wa_ref, wb_ref, wm_ref,
               gxa_ref, wq_ref, km_ref, vm_ref, wo_ref, gffn_ref, wr_ref, br_ref,
               x2_ref, hn_ref, idx_ref, gate_ref, cnt_ref):
    x1 = _merge_rows(x_ref, oa_ref, ob_ref, ga_ref, gb_ref, bg_ref, wa_ref, wb_ref, wm_ref)
    hx = _rms(x1, gxa_ref[...]).astype(BF16)
    q = _dot(hx, wq_ref[...]).astype(BF16)
    scale = XA_HEAD_DIM ** -0.5
    outs = []
    for h in range(XA_HEADS):
        sl = slice(h * XA_HEAD_DIM, (h + 1) * XA_HEAD_DIM)
        s = _dot_nt(q[:, sl], km_ref[:, sl]) * scale
        mx = jnp.max(s, axis=-1, keepdims=True)
        p = jnp.exp(s - mx)
        den = jnp.sum(p, axis=-1, keepdims=True)
        outs.append((_dot(p.astype(BF16), vm_ref[:, sl]) / den).astype(BF16))
    o = jnp.concatenate(outs, axis=-1)
    x2 = x1 + _dot(o, wo_ref[...])
    x2_ref[...] = x2
    hn = _rms(x2, gffn_ref[...])
    hn_ref[...] = _pack_rows(hn)
    hn_hi = hn.astype(BF16)
    hn_lo = (hn - hn_hi.astype(F32)).astype(BF16)
    big = _dot_nt(wr_ref[...], hn_hi)
    l = big[:N_EXPERTS] + big[N_EXPERTS:] + _dot_nt(wr_ref[:N_EXPERTS, :], hn_lo) + br_ref[...]
    eid = lax.broadcasted_iota(jnp.int32, l.shape, 0)
    vals = []
    idxs = []
    for k in range(TOP_K):
        mk = jnp.max(l, axis=0, keepdims=True)
        ik = jnp.min(jnp.where(l == mk, eid, N_EXPERTS), axis=0, keepdims=True)
        vals.append(mk)
        idxs.append(ik)
        l = jnp.where(eid == ik, -jnp.inf, l)
    ev = [jnp.exp(v - vals[0]) for v in vals]
    inv_den = 1.0 / (ev[0] + ev[1] + ev[2] + ev[3])
    gate_ref[...] = jnp.concatenate([e_k * inv_den for e_k in ev], axis=0)
    idx_ref[...] = jnp.concatenate(idxs, axis=0)
    hits = sum(jnp.where(eid == ik, 1.0, 0.0) for ik in idxs)
    cnt_ref[0] = jnp.sum(hits, axis=1, keepdims=True).astype(jnp.int32)


def _xa(x, o_a, o_b, proj, b_gates, wa, wb, wm, gxa, wq, kvmem, wo, gffn, wr, br, *,
        batch, seq, mem_len, tm, ga_blk, gb_blk):
    m, d = x.shape
    tm = min(tm, seq)
    nt = seq // tm
    row = lambda i: (i, 0)
    full = lambda i: (0, 0)
    return pl.pallas_call(
        _xa_kernel,
        grid=(m // tm,),
        in_specs=[
            pl.BlockSpec((tm, d), row),
            pl.BlockSpec((tm, d), row),
            pl.BlockSpec((tm, d), row),
            pl.BlockSpec((tm, d), lambda i: (i, ga_blk)),
            pl.BlockSpec((tm, d), lambda i: (i, gb_blk)),
            pl.BlockSpec((2, d), full),
            pl.BlockSpec((d, d), full),
            pl.BlockSpec((d, d), full),
            pl.BlockSpec((d, d), full),
            pl.BlockSpec((1, d), full),
            pl.BlockSpec((d, d), full),
            pl.BlockSpec((mem_len, d), lambda i: (i // nt, 0)),
            pl.BlockSpec((mem_len, d), lambda i: (i // nt, 1)),
            pl.BlockSpec((d, d), full),
            pl.BlockSpec((1, d), full),
            pl.BlockSpec((2 * N_EXPERTS, d), full),
            pl.BlockSpec((N_EXPERTS, 1), full),
        ],
        out_specs=[
            pl.BlockSpec((tm, d), row),
            pl.BlockSpec((tm, HALF), row),
            pl.BlockSpec((TOP_K, tm), lambda i: (0, i)),
            pl.BlockSpec((TOP_K, tm), lambda i: (0, i)),
            pl.BlockSpec((1, N_EXPERTS, 1), lambda i: (i, 0, 0)),
        ],
        out_shape=[
            jax.ShapeDtypeStruct((m, d), F32),
            jax.ShapeDtypeStruct((m, HALF), jnp.uint32),
            jax.ShapeDtypeStruct((TOP_K, m), jnp.int32),
            jax.ShapeDtypeStruct((TOP_K, m), F32),
            jax.ShapeDtypeStruct((m // tm, N_EXPERTS, 1), jnp.int32),
        ],
        compiler_params=_cparams(("parallel",)),
        name="merge_xattn_router",
    )(x, o_a, o_b, proj, proj, b_gates, wa, wb, wm, gxa, wq, kvmem, kvmem, wo, gffn, wr, br)


def _moe_kernel(bexp_ref, x_ref, wgu_ref, bgu_ref, wd_ref, bd_ref, y_ref, wgu_bf, wd_bf):
    i = pl.program_id(0)

    @pl.when((i == 0) | (bexp_ref[i] != bexp_ref[jnp.maximum(i - 1, 0)]))
    def _():
        wgu_bf[...] = wgu_ref[0].astype(BF16)
        wd_bf[...] = wd_ref[0].astype(BF16)

    n_used = bexp_ref[pl.num_programs(0)]

    @pl.when(i < n_used)
    def _():
        xb = _unpack_rows(x_ref[...]).astype(BF16)
        gu = _dot(xb, wgu_bf[...]) + bgu_ref[0]
        x_glu = jnp.minimum(gu[:, :D_FF], SWIGLU_LIMIT)
        x_lin = jnp.clip(gu[:, D_FF:], -SWIGLU_LIMIT, SWIGLU_LIMIT)
        act = x_glu / (1.0 + jnp.exp(-SWIGLU_ALPHA * x_glu)) * (x_lin + 1.0)
        y_ref[...] = _pack_rows(_dot(act.astype(BF16), wd_bf[...]) + bd_ref[0])

    @pl.when(i >= n_used)
    def _():
        y_ref[...] = jnp.zeros_like(y_ref)


def _moe(block_expert, x_sorted, wgu, bgu, wd, bd):
    rows = x_sorted.shape[0]
    d = D_MODEL
    n_blocks = rows // MOE_BLOCK
    grid_spec = pltpu.PrefetchScalarGridSpec(
        num_scalar_prefetch=1,
        grid=(n_blocks,),
        in_specs=[
            pl.BlockSpec((MOE_BLOCK, HALF), lambda i, be: (i, 0)),
            pl.BlockSpec((1, d, 2 * D_FF), lambda i, be: (be[i], 0, 0)),
            pl.BlockSpec((1, 1, 2 * D_FF), lambda i, be: (be[i], 0, 0)),
            pl.BlockSpec((1, D_FF, d), lambda i, be: (be[i], 0, 0)),
            pl.BlockSpec((1, 1, d), lambda i, be: (be[i], 0, 0)),
        ],
        out_specs=pl.BlockSpec((MOE_BLOCK, HALF), lambda i, be: (i, 0)),
        scratch_shapes=[
            pltpu.VMEM((d, 2 * D_FF), BF16),
            pltpu.VMEM((D_FF, d), BF16),
        ],
    )
    return pl.pallas_call(
        _moe_kernel,
        grid_spec=grid_spec,
        out_shape=jax.ShapeDtypeStruct((rows, HALF), jnp.uint32),
        compiler_params=_cparams(("arbitrary",)),
        name="moe_experts",
    )(block_expert, x_sorted, wgu, bgu, wd, bd)


def _sc_gather(table, idx):
    b = idx.shape[0]
    d = table.shape[1]
    per_worker = b // SC_WORKERS
    n_chunks = per_worker // SC_CHUNK
    assert b % SC_WORKERS == 0 and per_worker % (2 * SC_CHUNK) == 0, (b, SC_WORKERS, SC_CHUNK)
    mesh = plsc.VectorSubcoreMesh(core_axis_name="c", subcore_axis_name="s")

    @functools.partial(
        pl.kernel, mesh=mesh,
        out_type=jax.ShapeDtypeStruct((b, d), table.dtype),
        scratch_types=[pltpu.VMEM((per_worker,), jnp.int32),
                       pltpu.VMEM((2, SC_CHUNK, d), table.dtype),
                       pltpu.SemaphoreType.DMA((2,)),
                       pltpu.SemaphoreType.DMA((2,))],
        name="sc_row_gather",
    )
    def gather_kernel(table_hbm, idx_hbm, out_hbm, idx_v, rows_v, g_sem, w_sem):
        worker = lax.axis_index("s") * SC_CORES + lax.axis_index("c")
        base = worker * per_worker
        pltpu.sync_copy(idx_hbm.at[pl.ds(base, per_worker)], idx_v)

        def gather(t, slot):
            return pltpu.make_async_copy(table_hbm.at[idx_v.at[pl.ds(t * SC_CHUNK, SC_CHUNK)]],
                                         rows_v.at[slot], g_sem.at[slot])

        def write(t, slot):
            return pltpu.make_async_copy(rows_v.at[slot], out_hbm.at[pl.ds(base + t * SC_CHUNK, SC_CHUNK)],
                                         w_sem.at[slot])

        gather(0, 0).start()

        def body(t2, carry):
            for slot in range(2):
                t = t2 * 2 + slot
                gather(t, slot).wait()

                @pl.when(t + 1 < n_chunks)
                def _():
                    @pl.when(t >= 1)
                    def _():
                        write(t - 1, 1 - slot).wait()
                    gather(t + 1, 1 - slot).start()
                write(t, slot).start()
            return carry
        lax.fori_loop(0, n_chunks // 2, body, 0)
        write(n_chunks - 2, 0).wait()
        write(n_chunks - 1, 1).wait()

    return gather_kernel(table, idx)


def _combine_kernel(x_ref, y0_ref, y1_ref, y2_ref, y3_ref, gate_ref, gain_ref, *rest):
    o_ref = rest[-1]
    g = gate_ref[...]
    acc = x_ref[...]
    for k, y_ref in enumerate((y0_ref, y1_ref, y2_ref, y3_ref)):
        acc = acc + g[:, k:k + 1] * _unpack_rows(y_ref[...])
    o_ref[...] = _rms(acc, gain_ref[...])


def _combine(x2, y, gates, gain, prev, *, tm, chunk, n_chunks):
    m, d = x2.shape
    mc = m // n_chunks
    tm = min(tm, mc)
    kb = mc // tm
    off = chunk * kb
    row = lambda i: (off + i, 0)
    in_specs = ([pl.BlockSpec((tm, d), row)]
                + [pl.BlockSpec((tm, HALF), functools.partial(lambda i, k: (k * kb + i, 0), k=k))
                   for k in range(TOP_K)]
                + [pl.BlockSpec((tm, TOP_K), row), pl.BlockSpec((1, d), lambda i: (0, 0))])
    args = [x2, y, y, y, y, gates, gain]
    aliases = {}
    if prev is not None:
        in_specs.append(pl.BlockSpec(memory_space=pl.ANY))
        args.append(prev)
        aliases = {len(args) - 1: 0}
    return pl.pallas_call(
        _combine_kernel,
        grid=(kb,),
        in_specs=in_specs,
        out_specs=pl.BlockSpec((tm, d), row),
        out_shape=jax.ShapeDtypeStruct((m, d), F32),
        input_output_aliases=aliases,
        compiler_params=_cparams(("parallel",)),
        name="combine_norm",
    )(*args)


def _routing_tables(top_idx, counts, n_tok):
    n_assign = n_tok * TOP_K
    e_flat = top_idx.reshape(-1)
    assert N_EXPERTS * n_assign < 2 ** 31
    order = jnp.sort(e_flat * n_assign + jnp.arange(n_assign, dtype=jnp.int32)) % n_assign
    starts = jnp.cumsum(counts) - counts
    padded = (counts + MOE_BLOCK - 1) // MOE_BLOCK * MOE_BLOCK
    pad_ends = jnp.cumsum(padded)
    pad_starts = pad_ends - padded
    cap = -(-n_assign // MOE_BLOCK) * MOE_BLOCK + N_EXPERTS * MOE_BLOCK
    n_blocks = cap // MOE_BLOCK
    block_start = jnp.arange(n_blocks, dtype=jnp.int32) * MOE_BLOCK
    block_expert = jnp.minimum(
        jnp.sum((pad_ends[None, :] <= block_start[:, None]).astype(jnp.int32), axis=1), N_EXPERTS - 1)
    r = block_start[:, None] + jnp.arange(MOE_BLOCK, dtype=jnp.int32)[None, :]
    pos = r - pad_starts[block_expert][:, None]
    valid = pos < counts[block_expert][:, None]
    src = jnp.clip(starts[block_expert][:, None] + pos, 0, n_assign - 1)
    tok = jnp.where(valid, order[src] % n_tok, r % n_tok).reshape(-1)
    sorted_pos = jnp.argsort(order).astype(jnp.int32)
    row_of = sorted_pos + (pad_starts - starts)[e_flat]
    n_used = pad_ends[-1:] // MOE_BLOCK
    return tok, row_of, jnp.concatenate([block_expert, n_used]).astype(jnp.int32)


def kernel(x, mem, t5_table, lb_logits, norm_mix, w_in, b_gates, hg_norm_gain, swa_sinks, w_branch_a, w_branch_b, w_mix_out, norm_xa, xa_w_q, xa_w_k, xa_w_v, xa_w_o, norm_ffn, w_router, b_router, w_gate_up, b_gate_up, w_down, b_down, norm_final):
    batch, seq, d = x.shape
    mem_len = mem.shape[1]
    m = batch * seq
    depth = norm_mix.shape[0]
    assert depth == 1, "the final norm is fused into the layer's last kernel"
    lower_bounds = jnp.cumsum(jax.nn.softmax(lb_logits.astype(F32), axis=0), axis=0)
    bias = _t5_bias_table(t5_table)
    xt = x.reshape(m, d)
    for l in range(depth):
        w = w_in[l]
        cuts = np.cumsum([0, 1024, 1024, 1024, 1024, 1024, 256, 256, 1024, 1024])
        seg = [w[:, cuts[k]:cuts[k + 1]] for k in range(9)]
        w_perm = jnp.concatenate(seg[0:5] + seg[7:9] + seg[5:7], axis=1).astype(BF16)
        proj = _norm_matmul(xt, norm_mix[l][None], w_perm, use_norm=True, tm=2048, tn=1536, out_dtype=BF16)
        o_a = _hgrn(proj, lower_bounds[l][None], hg_norm_gain[l][None], batch=batch, seq=seq, t_rows=4096)
        o_b = _swa(proj, swa_sinks[l], bias, batch=batch, seq=seq, q_blk=4, k_blk=28, v_blk=29)
        w_kv = jnp.concatenate([xa_w_k[l], xa_w_v[l]], axis=1).astype(BF16)
        kvmem = _norm_matmul(mem.reshape(batch * mem_len, d), norm_xa[l][None], w_kv, use_norm=False,
                             tm=1024, tn=1024, out_dtype=BF16)
        wr_hi = w_router[l].astype(BF16)
        wr_lo = (w_router[l] - wr_hi.astype(F32)).astype(BF16)
        x2, hn, top_idx, gates, tile_counts = _xa(xt, o_a, o_b, proj, b_gates[l], w_branch_a[l].astype(BF16),
                                     w_branch_b[l].astype(BF16), w_mix_out[l].astype(BF16),
                                     norm_xa[l][None], xa_w_q[l].astype(BF16), kvmem, xa_w_o[l].astype(BF16),
                                     norm_ffn[l][None], jnp.concatenate([wr_hi, wr_lo], axis=1).T,
                                     b_router[l][:, None], batch=batch, seq=seq, mem_len=mem_len, tm=512,
                                     ga_blk=5, gb_blk=6)
        gates = gates.T
        tok, row_of, block_expert = _routing_tables(top_idx, jnp.sum(tile_counts[:, :, 0], axis=0), m)
        x_sorted = _sc_gather(hn, tok)
        y_sorted = _moe(block_expert, x_sorted, w_gate_up[l], b_gate_up[l][:, None, :],
                        w_down[l], b_down[l][:, None, :])
        rows_kt = row_of.reshape(TOP_K, m)
        mc = m // COMBINE_CHUNKS
        xt = None
        for c in range(COMBINE_CHUNKS):
            y = _sc_gather(y_sorted, rows_kt[:, c * mc:(c + 1) * mc].reshape(-1))
            xt = _combine(x2, y, gates, norm_final[None], xt, tm=256, chunk=c, n_chunks=COMBINE_CHUNKS)
    return xt.reshape(batch, seq, d)
```

```python
import functools
import math

import jax
import jax.numpy as jnp
import numpy as np
from jax import lax
from jax.experimental import pallas as pl
from jax.experimental.pallas import tpu as pltpu
from jax.experimental.pallas import tpu_sc as plsc

F32 = jnp.float32
BF16 = jnp.bfloat16

RMS_EPS = 1e-5
D_MODEL = 1024
HG_HEADS = 8
HG_DK = 128
HG_DV = 128
HG_CHUNK = 64
HG_LEVELS = (32, 16, 8)
HG_DIAG = 8
HG_GROUP = 4
LOG2E = 1.4426950408889634
SWA_HEADS = 16
SWA_KV_HEADS = 4
SWA_GROUP = 4
SWA_HEAD_DIM = 64
SWA_WINDOW = 128
SWA_BLOCK = 128
SWA_STEP_BLOCKS = 1
T5_BUCKETS = 32
T5_MAX_EXACT = 16
T5_MAX_DIST = 128
XA_HEADS = 4
XA_HEAD_DIM = 256
N_EXPERTS = 32
TOP_K = 4
D_FF = 1024
SWIGLU_ALPHA = 1.702
SWIGLU_LIMIT = 7.0
MOE_BLOCK = 512
SC_CORES = 2
SC_WORKERS = 32
SC_CHUNK = 64
COMBINE_CHUNKS = 4
HALF = D_MODEL // 2
NEG_BIG = -1e30

VMEM_LIMIT = 56 * 1024 * 1024


def _cparams(sem):
    return pltpu.CompilerParams(dimension_semantics=sem, vmem_limit_bytes=VMEM_LIMIT)


def _rms(xf, gain):
    return xf * lax.rsqrt(jnp.mean(xf * xf, axis=-1, keepdims=True) + RMS_EPS) * gain


def _dot(a, b):
    return jnp.dot(a, b, preferred_element_type=F32)


def _dot_nt(a, b):
    return lax.dot_general(a, b, (((1,), (1,)), ((), ())), preferred_element_type=F32)


def _dot_tn(a, b):
    return lax.dot_general(a, b, (((0,), (0,)), ((), ())), preferred_element_type=F32)


def _pack_rows(x):
    hi = lax.bitcast_convert_type(x[:, :HALF].astype(BF16).astype(F32), jnp.uint32)
    lo = lax.bitcast_convert_type(x[:, HALF:].astype(BF16).astype(F32), jnp.uint32)
    return hi | (lo >> 16)


def _unpack_rows(p):
    hi = lax.bitcast_convert_type(p & jnp.uint32(0xFFFF0000), F32)
    lo = lax.bitcast_convert_type(p << 16, F32)
    return jnp.concatenate([hi, lo], axis=-1)


def _norm_matmul_kernel(x_ref, g_ref, w_ref, o_ref, h_ref, *, use_norm):
    @pl.when(pl.program_id(1) == 0)
    def _():
        xf = x_ref[...].astype(F32)
        if use_norm:
            xf = _rms(xf, g_ref[...])
        h_ref[...] = xf.astype(BF16)

    o_ref[...] = _dot(h_ref[...], w_ref[...]).astype(o_ref.dtype)


def _norm_matmul(x, gain, w, *, use_norm, tm, tn, out_dtype):
    m, k = x.shape
    n = w.shape[1]
    tm = min(tm, m)
    tn = min(tn, n)
    return pl.pallas_call(
        functools.partial(_norm_matmul_kernel, use_norm=use_norm),
        grid=(m // tm, n // tn),
        in_specs=[
            pl.BlockSpec((tm, k), lambda i, j: (i, 0)),
            pl.BlockSpec((1, k), lambda i, j: (0, 0)),
            pl.BlockSpec((k, tn), lambda i, j: (0, j)),
        ],
        out_specs=pl.BlockSpec((tm, tn), lambda i, j: (i, j)),
        out_shape=jax.ShapeDtypeStruct((m, n), out_dtype),
        scratch_shapes=[pltpu.VMEM((tm, k), BF16)],
        compiler_params=_cparams(("parallel", "arbitrary")),
        name="norm_matmul",
    )(x, gain, w)


def _rows_bcast(ref, base, group, row, n):
    parts = [jnp.broadcast_to(ref[pl.ds(base + g0 + row, 1), :], (group, ref.shape[1]))
             for g0 in range(0, n, group)]
    return jnp.concatenate(parts, axis=0)


def _hgrn_kernel(q_ref, f_ref, i_ref, g_ref, lb_ref, gain_ref, o_ref,
                 st_ref, kk_ref, gc_ref, a_ref, qd_ref, kv_ref, dec_ref, *, t_rows):
    c = HG_CHUNK
    d = HG_DIAG
    grows = HG_GROUP * c
    ngroups = t_rows // grows

    @pl.when(pl.program_id(2) == 0)
    def _():
        st_ref[...] = jnp.zeros_like(st_ref)

    lb = lb_ref[...]
    gain = gain_ref[...]
    row = lax.broadcasted_iota(jnp.int32, (c, 1), 0)
    col = lax.broadcasted_iota(jnp.int32, (c, c), 1)
    cd = col - (row // d) * d
    cd = jnp.where(cd <= row % d, cd, -1)
    lvl_mask = []
    for b in HG_LEVELS:
        rb = lax.broadcasted_iota(jnp.int32, (c, c), 0) // b
        lvl_mask.append(((rb - col // b) * 2 + (col // b) % 2) == 2)

    def rows(ref_row0):
        return pl.ds(pl.multiple_of(ref_row0, c), c)

    def phase_a(gi, buf):
        for ch in range(HG_GROUP):
            src = rows(gi * grows + ch * c)
            dst = buf * grows + ch * c
            _hgrn_front(q_ref, f_ref, i_ref, lb, row, cd, lvl_mask, kk_ref, gc_ref, a_ref, qd_ref, kv_ref,
                        dec_ref, src, dst, buf * HG_GROUP + ch)

    def phase_b(gi, buf, st):
        sts = []
        for ch in range(HG_GROUP):
            sts.append(st.astype(BF16))
            slot = buf * HG_GROUP + ch
            st = st * dec_ref[pl.ds(slot, 1), :] + kv_ref[slot]
        for ch in range(HG_GROUP):
            src = rows(gi * grows + ch * c)
            dst = rows(buf * grows + ch * c)
            o = _dot_nt(qd_ref[dst, :], sts[ch]) + _dot(a_ref[dst, :], i_ref[src, :])
            on = _rms(o, gain)
            gv = g_ref[src, :].astype(F32)
            o_ref[src, :] = (on * gv * (1.0 / (1.0 + jnp.exp(-gv)))).astype(o_ref.dtype)
        return st

    phase_a(0, 0)

    def body(k, st):
        st = phase_b(k, k % 2, st)
        phase_a(k + 1, (k + 1) % 2)
        return st

    st = lax.fori_loop(0, ngroups - 1, body, st_ref[...])
    st_ref[...] = phase_b(ngroups - 1, (ngroups - 1) % 2, st)


def _hgrn_front(q_ref, f_ref, i_ref, lb, row, cd, lvl_mask, kk_ref, gc_ref, a_ref, qd_ref, kv_ref, dec_ref,
                sl, dst, slot):
    c = HG_CHUNK
    d = HG_DIAG
    ds = pl.ds(pl.multiple_of(dst, c), c)
    ff = f_ref[sl, :].astype(F32)
    e = jnp.exp(-jnp.abs(ff))
    r = 1.0 / (1.0 + e)
    sig_pos = jnp.where(ff >= 0, r, e * r)
    sig_neg = jnp.where(ff >= 0, e * r, r)
    logf = jnp.log(lb + (1.0 - lb) * sig_pos)
    kk = (1.0 - lb) * sig_neg
    qv = q_ref[sl, :].astype(F32)
    qf = qv * (1.0 / (1.0 + jnp.exp(-qv)))

    gcum = logf * LOG2E
    sh = 1
    while sh < c:
        gcum = gcum + jnp.where(row >= sh, pltpu.roll(gcum, sh, 0), 0.0)
        sh *= 2
    kk_ref[ds, :] = kk
    gc_ref[ds, :] = gcum
    g_last = gcum[c - 1:c, :]

    a = jnp.zeros((c, c), F32)
    for j in range(d):
        gj = _rows_bcast(gc_ref, dst, d, j, c)
        kj = _rows_bcast(kk_ref, dst, d, j, c)
        p = qf * jnp.exp2(gcum - gj) * kj
        a = jnp.where(cd == j, jnp.sum(p, axis=-1, keepdims=True), a)

    for li, b in enumerate(HG_LEVELS):
        gref = _rows_bcast(gc_ref, dst, 2 * b, b - 1, c)
        upper = (row % (2 * b)) >= b
        qa = jnp.where(upper, qf * jnp.exp2(gcum - gref), 0.0).astype(BF16)
        kb = jnp.where(upper, 0.0, kk * jnp.exp2(gref - gcum)).astype(BF16)
        a = a + jnp.where(lvl_mask[li], _dot_nt(qa, kb), 0.0)
    a_ref[ds, :] = a.astype(BF16)

    qd_ref[ds, :] = (qf * jnp.exp2(gcum)).astype(BF16)
    k_dec = (kk * jnp.exp2(g_last - gcum)).astype(BF16)
    kv_ref[slot] = _dot_tn(i_ref[sl, :], k_dec)
    dec_ref[pl.ds(slot, 1), :] = jnp.exp2(g_last)


def _hgrn(proj, lb, gain, *, batch, seq, t_rows):
    m = batch * seq
    t_rows = min(t_rows, seq)
    nt = seq // t_rows
    h = HG_HEADS
    grows = HG_GROUP * HG_CHUNK
    assert t_rows % grows == 0

    def spec(off):
        return pl.BlockSpec((t_rows, 128), lambda b, hh, t: (b * nt + t, off + hh))

    return pl.pallas_call(
        functools.partial(_hgrn_kernel, t_rows=t_rows),
        grid=(batch, h, nt),
        in_specs=[spec(0), spec(h), spec(2 * h), spec(3 * h),
                  pl.BlockSpec((1, 128), lambda b, hh, t: (0, hh)),
                  pl.BlockSpec((1, 128), lambda b, hh, t: (0, 0))],
        out_specs=pl.BlockSpec((t_rows, 128), lambda b, hh, t: (b * nt + t, hh)),
        out_shape=jax.ShapeDtypeStruct((m, h * HG_DV), BF16),
        scratch_shapes=[
            pltpu.VMEM((HG_DV, HG_DK), F32),
            pltpu.VMEM((2 * grows, 128), F32),
            pltpu.VMEM((2 * grows, 128), F32),
            pltpu.VMEM((2 * grows, HG_CHUNK), BF16),
            pltpu.VMEM((2 * grows, 128), BF16),
            pltpu.VMEM((2 * HG_GROUP, HG_DV, HG_DK), F32),
            pltpu.VMEM((2 * HG_GROUP, 128), F32),
        ],
        compiler_params=_cparams(("parallel", "parallel", "arbitrary")),
        name="hgrn2",
    )(proj, proj, proj, proj, lb, gain)


def _swa_kernel(sink_ref, q_ref, kp_ref, kc_ref, vp_ref, vc_ref, bias_ref, o_ref):
    blk = SWA_BLOCK
    dh = SWA_HEAD_DIM
    first = pl.program_id(1) == 0
    rt = lax.broadcasted_iota(jnp.int32, (blk, blk), 0)
    ct = lax.broadcasted_iota(jnp.int32, (blk, blk), 1)
    from_cur = ct <= rt
    no_prev = first & (ct > rt)
    scale = dh ** -0.5
    for sub in range(SWA_STEP_BLOCKS):
        rows = slice(sub * blk, (sub + 1) * blk)
        outs = []
        for h in range(SWA_KV_HEADS):
            ks = slice(h * dh, (h + 1) * dh)
            k_prev = kp_ref[:, ks] if sub == 0 else kc_ref[(sub - 1) * blk:sub * blk, ks]
            v_prev = vp_ref[:, ks] if sub == 0 else vc_ref[(sub - 1) * blk:sub * blk, ks]
            kcat = jnp.concatenate([k_prev, kc_ref[rows, ks]], axis=0)
            vcat = jnp.concatenate([v_prev, vc_ref[rows, ks]], axis=0)
            for g in range(SWA_GROUP):
                hq = h * SWA_GROUP + g
                q = q_ref[rows, hq * dh:(hq + 1) * dh]
                s2 = _dot_nt(q, kcat)
                s = jnp.where(from_cur, s2[:, blk:], s2[:, :blk]) * scale + bias_ref[hq]
                if sub == 0:
                    s = jnp.where(no_prev, NEG_BIG, s)
                sink = sink_ref[hq]
                mx = jnp.maximum(jnp.max(s, axis=-1, keepdims=True), sink)
                p = jnp.exp(s - mx)
                den = jnp.sum(p, axis=-1, keepdims=True) + jnp.exp(sink - mx)
                p2 = jnp.concatenate([jnp.where(from_cur, 0.0, p), jnp.where(from_cur, p, 0.0)], axis=-1)
                o = _dot(p2.astype(BF16), vcat) / den
                outs.append(o)
        o_ref[rows, :] = jnp.concatenate(outs, axis=-1).astype(o_ref.dtype)


def _swa(proj, sinks, bias, *, batch, seq, q_blk, k_blk, v_blk):
    m = batch * seq
    nb = seq // SWA_BLOCK
    blk = SWA_BLOCK
    kw = SWA_KV_HEADS * SWA_HEAD_DIM
    qw = SWA_HEADS * SWA_HEAD_DIM

    sb = SWA_STEP_BLOCKS
    ns = nb // sb
    assert nb % sb == 0

    def cur(cb):
        return lambda b, n, s: (b * ns + n, cb)

    def prev(cb):
        return lambda b, n, s: (b * nb + jnp.maximum(n * sb - 1, 0), cb)

    grid_spec = pltpu.PrefetchScalarGridSpec(
        num_scalar_prefetch=1,
        grid=(batch, ns),
        in_specs=[
            pl.BlockSpec((sb * blk, qw), cur(q_blk)),
            pl.BlockSpec((blk, kw), prev(k_blk)),
            pl.BlockSpec((sb * blk, kw), cur(k_blk)),
            pl.BlockSpec((blk, kw), prev(v_blk)),
            pl.BlockSpec((sb * blk, kw), cur(v_blk)),
            pl.BlockSpec((SWA_HEADS, blk, blk), lambda b, n, s: (0, 0, 0)),
        ],
        out_specs=pl.BlockSpec((sb * blk, qw), lambda b, n, s: (b * ns + n, 0)),
    )
    return pl.pallas_call(
        _swa_kernel,
        grid_spec=grid_spec,
        out_shape=jax.ShapeDtypeStruct((m, qw), BF16),
        compiler_params=_cparams(("parallel", "arbitrary")),
        name="swa",
    )(sinks, proj, proj, proj, proj, proj, bias)


def _t5_bias_table(t5_table):
    t_loc = np.arange(SWA_BLOCK, dtype=np.int32)[:, None]
    s_loc = np.arange(2 * SWA_BLOCK, dtype=np.int32)[None, :]
    dist = t_loc + SWA_BLOCK - s_loc
    n = np.maximum(dist, 0)
    nf = np.maximum(n, 1).astype(np.float32)
    large = T5_MAX_EXACT + (np.log(nf / np.float32(T5_MAX_EXACT)) / np.float32(math.log(T5_MAX_DIST / T5_MAX_EXACT))
                            * np.float32(T5_BUCKETS - T5_MAX_EXACT)).astype(np.int32)
    large = np.minimum(large, T5_BUCKETS - 1)
    bucket = np.where(n < T5_MAX_EXACT, n, large).astype(np.int32)
    in_band = (dist >= 0) & (dist < SWA_WINDOW)
    bias = t5_table.astype(F32)[jnp.asarray(bucket)]
    bias = jnp.where(jnp.asarray(in_band)[..., None], bias, NEG_BIG).transpose(2, 0, 1)
    from_cur = jnp.asarray(np.tril(np.ones((SWA_BLOCK, SWA_BLOCK), dtype=bool)))
    return jnp.where(from_cur[None], bias[:, :, SWA_BLOCK:], bias[:, :, :SWA_BLOCK])


def _merge_rows(x_ref, oa_ref, ob_ref, ga_ref, gb_ref, bg_ref, wa_ref, wb_ref, wm_ref):
    a = _dot(oa_ref[...], wa_ref[...])
    b = _dot(ob_ref[...], wb_ref[...])
    bg = bg_ref[...]
    za = ga_ref[...].astype(F32) + bg[0:1]
    zb = gb_ref[...].astype(F32) + bg[1:2]
    merged = a / (1.0 + jnp.exp(-za)) + b / (1.0 + jnp.exp(-zb))
    return x_ref[...] + _dot(merged.astype(BF16), wm_ref[...])


def _xa_kernel(x_ref, oa_ref, ob_ref, ga_ref, gb_ref, bg_ref, wa_ref, wb_ref, wm_ref,
               gxa_ref, wq_ref, km_ref, vm_ref, wo_ref, gffn_ref, wr_ref, br_ref,
               x2_ref, hn_ref, idx_ref, gate_ref, cnt_ref):
    x1 = _merge_rows(x_ref, oa_ref, ob_ref, ga_ref, gb_ref, bg_ref, wa_ref, wb_ref, wm_ref)
    hx = _rms(x1, gxa_ref[...]).astype(BF16)
    q = _dot(hx, wq_ref[...]).astype(BF16)
    scale = XA_HEAD_DIM ** -0.5
    outs = []
    for h in range(XA_HEADS):
        sl = slice(h * XA_HEAD_DIM, (h + 1) * XA_HEAD_DIM)
        s = _dot_nt(q[:, sl], km_ref[:, sl]) * scale
        mx = jnp.max(s, axis=-1, keepdims=True)
        p = jnp.exp(s - mx)
        den = jnp.sum(p, axis=-1, keepdims=True)
        outs.append((_dot(p.astype(BF16), vm_ref[:, sl]) / den).astype(BF16))
    o = jnp.concatenate(outs, axis=-1)
    x2 = x1 + _dot(o, wo_ref[...])
    x2_ref[...] = x2
    hn = _rms(x2, gffn_ref[...])
    hn_ref[...] = _pack_rows(hn)
    hn_hi = hn.astype(BF16)
    hn_lo = (hn - hn_hi.astype(F32)).astype(BF16)
    big = _dot_nt(wr_ref[...], hn_hi)
    l = big[:N_EXPERTS] + big[N_EXPERTS:] + _dot_nt(wr_ref[:N_EXPERTS, :], hn_lo) + br_ref[...]
    eid = lax.broadcasted_iota(jnp.int32, l.shape, 0)
    vals = []
    idxs = []
    for k in range(TOP_K):
        mk = jnp.max(l, axis=0, keepdims=True)
        ik = jnp.min(jnp.where(l == mk, eid, N_EXPERTS), axis=0, keepdims=True)
        vals.append(mk)
        idxs.append(ik)
        l = jnp.where(eid == ik, -jnp.inf, l)
    ev = [jnp.exp(v - vals[0]) for v in vals]
    inv_den = 1.0 / (ev[0] + ev[1] + ev[2] + ev[3])
    gate_ref[...] = jnp.concatenate([e_k * inv_den for e_k in ev], axis=0)
    idx_ref[...] = jnp.concatenate(idxs, axis=0)
    hits = sum(jnp.where(eid == ik, 1.0, 0.0) for ik in idxs)
    cnt_ref[0] = jnp.sum(hits, axis=1, keepdims=True).astype(jnp.int32)


def _xa(x, o_a, o_b, proj, b_gates, wa, wb, wm, gxa, wq, kvmem, wo, gffn, wr, br, *,
        batch, seq, mem_len, tm, ga_blk, gb_blk):
    m, d = x.shape
    tm = min(tm, seq)
    nt = seq // tm
    row = lambda i: (i, 0)
    full = lambda i: (0, 0)
    return pl.pallas_call(
        _xa_kernel,
        grid=(m // tm,),
        in_specs=[
            pl.BlockSpec((tm, d), row),
            pl.BlockSpec((tm, d), row),
            pl.BlockSpec((tm, d), row),
            pl.BlockSpec((tm, d), lambda i: (i, ga_blk)),
            pl.BlockSpec((tm, d), lambda i: (i, gb_blk)),
            pl.BlockSpec((2, d), full),
            pl.BlockSpec((d, d), full),
            pl.BlockSpec((d, d), full),
            pl.BlockSpec((d, d), full),
            pl.BlockSpec((1, d), full),
            pl.BlockSpec((d, d), full),
            pl.BlockSpec((mem_len, d), lambda i: (i // nt, 0)),
            pl.BlockSpec((mem_len, d), lambda i: (i // nt, 1)),
            pl.BlockSpec((d, d), full),
            pl.BlockSpec((1, d), full),
            pl.BlockSpec((2 * N_EXPERTS, d), full),
            pl.BlockSpec((N_EXPERTS, 1), full),
        ],
        out_specs=[
            pl.BlockSpec((tm, d), row),
            pl.BlockSpec((tm, HALF), row),
            pl.BlockSpec((TOP_K, tm), lambda i: (0, i)),
            pl.BlockSpec((TOP_K, tm), lambda i: (0, i)),
            pl.BlockSpec((1, N_EXPERTS, 1), lambda i: (i, 0, 0)),
        ],
        out_shape=[
            jax.ShapeDtypeStruct((m, d), F32),
            jax.ShapeDtypeStruct((m, HALF), jnp.uint32),
            jax.ShapeDtypeStruct((TOP_K, m), jnp.int32),
            jax.ShapeDtypeStruct((TOP_K, m), F32),
            jax.ShapeDtypeStruct((m // tm, N_EXPERTS, 1), jnp.int32),
        ],
        compiler_params=_cparams(("parallel",)),
        name="merge_xattn_router",
    )(x, o_a, o_b, proj, proj, b_gates, wa, wb, wm, gxa, wq, kvmem, kvmem, wo, gffn, wr, br)


def _moe_kernel(bexp_ref, x_ref, wgu_ref, bgu_ref, wd_ref, bd_ref, y_ref, wgu_bf, wd_bf):
    i = pl.program_id(0)

    @pl.when((i == 0) | (bexp_ref[i] != bexp_ref[jnp.maximum(i - 1, 0)]))
    def _():
        wgu_bf[...] = wgu_ref[0].astype(BF16)
        wd_bf[...] = wd_ref[0].astype(BF16)

    n_used = bexp_ref[pl.num_programs(0)]

    @pl.when(i < n_used)
    def _():
        xb = _unpack_rows(x_ref[...]).astype(BF16)
        gu = _dot(xb, wgu_bf[...]) + bgu_ref[0]
        x_glu = jnp.minimum(gu[:, :D_FF], SWIGLU_LIMIT)
        x_lin = jnp.clip(gu[:, D_FF:], -SWIGLU_LIMIT, SWIGLU_LIMIT)
        act = x_glu / (1.0 + jnp.exp(-SWIGLU_ALPHA * x_glu)) * (x_lin + 1.0)
        y_ref[...] = _pack_rows(_dot(act.astype(BF16), wd_bf[...]) + bd_ref[0])

    @pl.when(i >= n_used)
    def _():
        y_ref[...] = jnp.zeros_like(y_ref)


def _moe(block_expert, x_sorted, wgu, bgu, wd, bd):
    rows = x_sorted.shape[0]
    d = D_MODEL
    n_blocks = rows // MOE_BLOCK
    grid_spec = pltpu.PrefetchScalarGridSpec(
        num_scalar_prefetch=1,
        grid=(n_blocks,),
        in_specs=[
            pl.BlockSpec((MOE_BLOCK, HALF), lambda i, be: (i, 0)),
            pl.BlockSpec((1, d, 2 * D_FF), lambda i, be: (be[i], 0, 0)),
            pl.BlockSpec((1, 1, 2 * D_FF), lambda i, be: (be[i], 0, 0)),
            pl.BlockSpec((1, D_FF, d), lambda i, be: (be[i], 0, 0)),
            pl.BlockSpec((1, 1, d), lambda i, be: (be[i], 0, 0)),
        ],
        out_specs=pl.BlockSpec((MOE_BLOCK, HALF), lambda i, be: (i, 0)),
        scratch_shapes=[
            pltpu.VMEM((d, 2 * D_FF), BF16),
            pltpu.VMEM((D_FF, d), BF16),
        ],
    )
    return pl.pallas_call(
        _moe_kernel,
        grid_spec=grid_spec,
        out_shape=jax.ShapeDtypeStruct((rows, HALF), jnp.uint32),
        compiler_params=_cparams(("arbitrary",)),
        name="moe_experts",
    )(block_expert, x_sorted, wgu, bgu, wd, bd)


def _sc_gather(table, idx):
    b = idx.shape[0]
    d = table.shape[1]
    per_worker = b // SC_WORKERS
    n_chunks = per_worker // SC_CHUNK
    assert b % SC_WORKERS == 0 and per_worker % (2 * SC_CHUNK) == 0, (b, SC_WORKERS, SC_CHUNK)
    mesh = plsc.VectorSubcoreMesh(core_axis_name="c", subcore_axis_name="s")

    @functools.partial(
        pl.kernel, mesh=mesh,
        out_type=jax.ShapeDtypeStruct((b, d), table.dtype),
        scratch_types=[pltpu.VMEM((per_worker,), jnp.int32),
                       pltpu.VMEM((2, SC_CHUNK, d), table.dtype),
                       pltpu.SemaphoreType.DMA((2,)),
                       pltpu.SemaphoreType.DMA((2,))],
        name="sc_row_gather",
    )
    def gather_kernel(table_hbm, idx_hbm, out_hbm, idx_v, rows_v, g_sem, w_sem):
        worker = lax.axis_index("s") * SC_CORES + lax.axis_index("c")
        base = worker * per_worker
        pltpu.sync_copy(idx_hbm.at[pl.ds(base, per_worker)], idx_v)

        def gather(t, slot):
            return pltpu.make_async_copy(table_hbm.at[idx_v.at[pl.ds(t * SC_CHUNK, SC_CHUNK)]],
                                         rows_v.at[slot], g_sem.at[slot])

        def write(t, slot):
            return pltpu.make_async_copy(rows_v.at[slot], out_hbm.at[pl.ds(base + t * SC_CHUNK, SC_CHUNK)],
                                         w_sem.at[slot])

        gather(0, 0).start()

        def body(t2, carry):
            for slot in range(2):
                t = t2 * 2 + slot
                gather(t, slot).wait()

                @pl.when(t + 1 < n_chunks)
                def _():
                    @pl.when(t >= 1)
                    def _():
                        write(t - 1, 1 - slot).wait()
                    gather(t + 1, 1 - slot).start()
                write(t, slot).start()
            return carry
        lax.fori_loop(0, n_chunks // 2, body, 0)
        write(n_chunks - 2, 0).wait()
        write(n_chunks - 1, 1).wait()

    return gather_kernel(table, idx)


def _combine_kernel(x_ref, y0_ref, y1_ref, y2_ref, y3_ref, gate_ref, gain_ref, *rest):
    o_ref = rest[-1]
    g = gate_ref[...]
    acc = x_ref[...]
    for k, y_ref in enumerate((y0_ref, y1_ref, y2_ref, y3_ref)):
        acc = acc + g[:, k:k + 1] * _unpack_rows(y_ref[...])
    o_ref[...] = _rms(acc, gain_ref[...])


def _combine(x2, y, gates, gain, prev, *, tm, chunk, n_chunks):
    m, d = x2.shape
    mc = m // n_chunks
    tm = min(tm, mc)
    kb = mc // tm
    off = chunk * kb
    row = lambda i: (off + i, 0)
    in_specs = ([pl.BlockSpec((tm, d), row)]
                + [pl.BlockSpec((tm, HALF), functools.partial(lambda i, k: (k * kb + i, 0), k=k))
                   for k in range(TOP_K)]
                + [pl.BlockSpec((tm, TOP_K), row), pl.BlockSpec((1, d), lambda i: (0, 0))])
    args = [x2, y, y, y, y, gates, gain]
    aliases = {}
    if prev is not None:
        in_specs.append(pl.BlockSpec(memory_space=pl.ANY))
        args.append(prev)
        aliases = {len(args) - 1: 0}
    return pl.pallas_call(
        _combine_kernel,
        grid=(kb,),
        in_specs=in_specs,
        out_specs=pl.BlockSpec((tm, d), row),
        out_shape=jax.ShapeDtypeStruct((m, d), F32),
        input_output_aliases=aliases,
        compiler_params=_cparams(("parallel",)),
        name="combine_norm",
    )(*args)


def _routing_tables(top_idx, counts, n_tok):
    n_assign = n_tok * TOP_K
    e_flat = top_idx.reshape(-1)
    assert N_EXPERTS * n_assign < 2 ** 31
    order = jnp.sort(e_flat * n_assign + jnp.arange(n_assign, dtype=jnp.int32)) % n_assign
    starts = jnp.cumsum(counts) - counts
    padded = (counts + MOE_BLOCK - 1) // MOE_BLOCK * MOE_BLOCK
    pad_ends = jnp.cumsum(padded)
    pad_starts = pad_ends - padded
    cap = -(-n_assign // MOE_BLOCK) * MOE_BLOCK + N_EXPERTS * MOE_BLOCK
    n_blocks = cap // MOE_BLOCK
    block_start = jnp.arange(n_blocks, dtype=jnp.int32) * MOE_BLOCK
    block_expert = jnp.minimum(
        jnp.sum((pad_ends[None, :] <= block_start[:, None]).astype(jnp.int32), axis=1), N_EXPERTS - 1)
    r = block_start[:, None] + jnp.arange(MOE_BLOCK, dtype=jnp.int32)[None, :]
    pos = r - pad_starts[block_expert][:, None]
    valid = pos < counts[block_expert][:, None]
    src = jnp.clip(starts[block_expert][:, None] + pos, 0, n_assign - 1)
    tok = jnp.where(valid, order[src] % n_tok, r % n_tok).reshape(-1)
    sorted_pos = jnp.argsort(order).astype(jnp.int32)
    row_of = sorted_pos + (pad_starts - starts)[e_flat]
    n_used = pad_ends[-1:] // MOE_BLOCK
    return tok, row_of, jnp.concatenate([block_expert, n_used]).astype(jnp.int32)


def kernel(x, mem, t5_table, lb_logits, norm_mix, w_in, b_gates, hg_norm_gain, swa_sinks, w_branch_a, w_branch_b, w_mix_out, norm_xa, xa_w_q, xa_w_k, xa_w_v, xa_w_o, norm_ffn, w_router, b_router, w_gate_up, b_gate_up, w_down, b_down, norm_final):
    batch, seq, d = x.shape
    mem_len = mem.shape[1]
    m = batch * seq
    depth = norm_mix.shape[0]
    assert depth == 1, "the final norm is fused into the layer's last kernel"
    lower_bounds = jnp.cumsum(jax.nn.softmax(lb_logits.astype(F32), axis=0), axis=0)
    bias = _t5_bias_table(t5_table)
    xt = x.reshape(m, d)
    for l in range(depth):
        w = w_in[l]
        cuts = np.cumsum([0, 1024, 1024, 1024, 1024, 1024, 256, 256, 1024, 1024])
        seg = [w[:, cuts[k]:cuts[k + 1]] for k in range(9)]
        w_perm = jnp.concatenate(seg[0:5] + seg[7:9] + seg[5:7], axis=1).astype(BF16)
        proj = _norm_matmul(xt, norm_mix[l][None], w_perm, use_norm=True, tm=2048, tn=1920, out_dtype=BF16)
        o_a = _hgrn(proj, lower_bounds[l][None], hg_norm_gain[l][None], batch=batch, seq=seq, t_rows=4096)
        o_b = _swa(proj, swa_sinks[l], bias, batch=batch, seq=seq, q_blk=4, k_blk=28, v_blk=29)
        w_kv = jnp.concatenate([xa_w_k[l], xa_w_v[l]], axis=1).astype(BF16)
        kvmem = _norm_matmul(mem.reshape(batch * mem_len, d), norm_xa[l][None], w_kv, use_norm=False,
                             tm=1024, tn=1024, out_dtype=BF16)
        wr_hi = w_router[l].astype(BF16)
        wr_lo = (w_router[l] - wr_hi.astype(F32)).astype(BF16)
        x2, hn, top_idx, gates, tile_counts = _xa(xt, o_a, o_b, proj, b_gates[l], w_branch_a[l].astype(BF16),
                                     w_branch_b[l].astype(BF16), w_mix_out[l].astype(BF16),
                                     norm_xa[l][None], xa_w_q[l].astype(BF16), kvmem, xa_w_o[l].astype(BF16),
                                     norm_ffn[l][None], jnp.concatenate([wr_hi, wr_lo], axis=1).T,
                                     b_router[l][:, None], batch=batch, seq=seq, mem_len=mem_len, tm=512,
                                     ga_blk=5, gb_blk=6)
        gates = gates.T
        tok, row_of, block_expert = _routing_tables(top_idx, jnp.sum(tile_counts[:, :, 0], axis=0), m)
        x_sorted = _sc_gather(hn, tok)
        y_sorted = _moe(block_expert, x_sorted, w_gate_up[l], b_gate_up[l][:, None, :],
                        w_down[l], b_down[l][:, None, :])
        rows_kt = row_of.reshape(TOP_K, m)
        mc = m // COMBINE_CHUNKS
        xt = None
        for c in range(COMBINE_CHUNKS):
            y = _sc_gather(y_sorted, rows_kt[:, c * mc:(c + 1) * mc].reshape(-1))
            xt = _combine(x2, y, gates, norm_final[None], xt, tm=512, chunk=c, n_chunks=COMBINE_CHUNKS)
    return xt.reshape(batch, seq, d)
```

```python
import functools
import math

import jax
import jax.numpy as jnp
import numpy as np
from jax import lax
from jax.experimental import pallas as pl
from jax.experimental.pallas import tpu as pltpu
from jax.experimental.pallas import tpu_sc as plsc

F32 = jnp.float32
BF16 = jnp.bfloat16

RMS_EPS = 1e-5
D_MODEL = 1024
HG_HEADS = 8
HG_DK = 128
HG_DV = 128
HG_CHUNK = 64
HG_LEVELS = (32, 16, 8)
HG_DIAG = 8
HG_GROUP = 4
LOG2E = 1.4426950408889634
SWA_HEADS = 16
SWA_KV_HEADS = 4
SWA_GROUP = 4
SWA_HEAD_DIM = 64
SWA_WINDOW = 128
SWA_BLOCK = 128
SWA_STEP_BLOCKS = 1
T5_BUCKETS = 32
T5_MAX_EXACT = 16
T5_MAX_DIST = 128
XA_HEADS = 4
XA_HEAD_DIM = 256
N_EXPERTS = 32
TOP_K = 4
D_FF = 1024
SWIGLU_ALPHA = 1.702
SWIGLU_LIMIT = 7.0
MOE_BLOCK = 512
SC_CORES = 2
SC_WORKERS = 32
SC_CHUNK = 64
SC_SPARE_ROWS = 4096
COMBINE_CHUNKS = 4
HALF = D_MODEL // 2
NEG_BIG = -1e30

VMEM_LIMIT = 56 * 1024 * 1024


def _cparams(sem):
    return pltpu.CompilerParams(dimension_semantics=sem, vmem_limit_bytes=VMEM_LIMIT)


def _rms(xf, gain):
    return xf * lax.rsqrt(jnp.mean(xf * xf, axis=-1, keepdims=True) + RMS_EPS) * gain


def _dot(a, b):
    return jnp.dot(a, b, preferred_element_type=F32)


def _dot_nt(a, b):
    return lax.dot_general(a, b, (((1,), (1,)), ((), ())), preferred_element_type=F32)


def _dot_tn(a, b):
    return lax.dot_general(a, b, (((0,), (0,)), ((), ())), preferred_element_type=F32)


def _pack_rows(x):
    hi = lax.bitcast_convert_type(x[:, :HALF].astype(BF16).astype(F32), jnp.uint32)
    lo = lax.bitcast_convert_type(x[:, HALF:].astype(BF16).astype(F32), jnp.uint32)
    return hi | (lo >> 16)


def _unpack_rows(p):
    hi = lax.bitcast_convert_type(p & jnp.uint32(0xFFFF0000), F32)
    lo = lax.bitcast_convert_type(p << 16, F32)
    return jnp.concatenate([hi, lo], axis=-1)


def _norm_matmul_kernel(x_ref, g_ref, w_ref, o_ref, h_ref, *, use_norm):
    @pl.when(pl.program_id(1) == 0)
    def _():
        xf = x_ref[...].astype(F32)
        if use_norm:
            xf = _rms(xf, g_ref[...])
        h_ref[...] = xf.astype(BF16)

    o_ref[...] = _dot(h_ref[...], w_ref[...]).astype(o_ref.dtype)


def _norm_matmul(x, gain, w, *, use_norm, tm, tn, out_dtype):
    m, k = x.shape
    n = w.shape[1]
    tm = min(tm, m)
    tn = min(tn, n)
    return pl.pallas_call(
        functools.partial(_norm_matmul_kernel, use_norm=use_norm),
        grid=(m // tm, n // tn),
        in_specs=[
            pl.BlockSpec((tm, k), lambda i, j: (i, 0)),
            pl.BlockSpec((1, k), lambda i, j: (0, 0)),
            pl.BlockSpec((k, tn), lambda i, j: (0, j)),
        ],
        out_specs=pl.BlockSpec((tm, tn), lambda i, j: (i, j)),
        out_shape=jax.ShapeDtypeStruct((m, n), out_dtype),
        scratch_shapes=[pltpu.VMEM((tm, k), BF16)],
        compiler_params=_cparams(("parallel", "arbitrary")),
        name="norm_matmul",
    )(x, gain, w)


def _rows_bcast(ref, base, group, row, n):
    parts = [jnp.broadcast_to(ref[pl.ds(base + g0 + row, 1), :], (group, ref.shape[1]))
             for g0 in range(0, n, group)]
    return jnp.concatenate(parts, axis=0)


def _hgrn_kernel(q_ref, f_ref, i_ref, g_ref, lb_ref, gain_ref, o_ref,
                 st_ref, kk_ref, gc_ref, a_ref, qd_ref, kv_ref, dec_ref, *, t_rows):
    c = HG_CHUNK
    d = HG_DIAG
    grows = HG_GROUP * c
    ngroups = t_rows // grows

    @pl.when(pl.program_id(2) == 0)
    def _():
        st_ref[...] = jnp.zeros_like(st_ref)

    lb = lb_ref[...]
    gain = gain_ref[...]
    row = lax.broadcasted_iota(jnp.int32, (c, 1), 0)
    col = lax.broadcasted_iota(jnp.int32, (c, c), 1)
    cd = col - (row // d) * d
    cd = jnp.where(cd <= row % d, cd, -1)
    lvl_mask = []
    for b in HG_LEVELS:
        rb = lax.broadcasted_iota(jnp.int32, (c, c), 0) // b
        lvl_mask.append(((rb - col // b) * 2 + (col // b) % 2) == 2)

    def rows(ref_row0):
        return pl.ds(pl.multiple_of(ref_row0, c), c)

    def phase_a(gi, buf):
        for ch in range(HG_GROUP):
            src = rows(gi * grows + ch * c)
            dst = buf * grows + ch * c
            _hgrn_front(q_ref, f_ref, i_ref, lb, row, cd, lvl_mask, kk_ref, gc_ref, a_ref, qd_ref, kv_ref,
                        dec_ref, src, dst, buf * HG_GROUP + ch)

    def phase_b(gi, buf, st):
        sts = []
        for ch in range(HG_GROUP):
            sts.append(st.astype(BF16))
            slot = buf * HG_GROUP + ch
            st = st * dec_ref[pl.ds(slot, 1), :] + kv_ref[slot]
        for ch in range(HG_GROUP):
            src = rows(gi * grows + ch * c)
            dst = rows(buf * grows + ch * c)
            o = _dot_nt(qd_ref[dst, :], sts[ch]) + _dot(a_ref[dst, :], i_ref[src, :])
            on = _rms(o, gain)
            gv = g_ref[src, :].astype(F32)
            o_ref[src, :] = (on * gv * (1.0 / (1.0 + jnp.exp(-gv)))).astype(o_ref.dtype)
        return st

    phase_a(0, 0)

    def body(k, st):
        st = phase_b(k, k % 2, st)
        phase_a(k + 1, (k + 1) % 2)
        return st

    st = lax.fori_loop(0, ngroups - 1, body, st_ref[...])
    st_ref[...] = phase_b(ngroups - 1, (ngroups - 1) % 2, st)


def _hgrn_front(q_ref, f_ref, i_ref, lb, row, cd, lvl_mask, kk_ref, gc_ref, a_ref, qd_ref, kv_ref, dec_ref,
                sl, dst, slot):
    c = HG_CHUNK
    d = HG_DIAG
    ds = pl.ds(pl.multiple_of(dst, c), c)
    ff = f_ref[sl, :].astype(F32)
    e = jnp.exp(-jnp.abs(ff))
    r = 1.0 / (1.0 + e)
    sig_pos = jnp.where(ff >= 0, r, e * r)
    sig_neg = jnp.where(ff >= 0, e * r, r)
    logf = jnp.log(lb + (1.0 - lb) * sig_pos)
    kk = (1.0 - lb) * sig_neg
    qv = q_ref[sl, :].astype(F32)
    qf = qv * (1.0 / (1.0 + jnp.exp(-qv)))

    gcum = logf * LOG2E
    sh = 1
    while sh < c:
        gcum = gcum + jnp.where(row >= sh, pltpu.roll(gcum, sh, 0), 0.0)
        sh *= 2
    kk_ref[ds, :] = kk
    gc_ref[ds, :] = gcum
    g_last = gcum[c - 1:c, :]

    a = jnp.zeros((c, c), F32)
    for j in range(d):
        gj = _rows_bcast(gc_ref, dst, d, j, c)
        kj = _rows_bcast(kk_ref, dst, d, j, c)
        p = qf * jnp.exp2(gcum - gj) * kj
        a = jnp.where(cd == j, jnp.sum(p, axis=-1, keepdims=True), a)

    for li, b in enumerate(HG_LEVELS):
        gref = _rows_bcast(gc_ref, dst, 2 * b, b - 1, c)
        upper = (row % (2 * b)) >= b
        qa = jnp.where(upper, qf * jnp.exp2(gcum - gref), 0.0).astype(BF16)
        kb = jnp.where(upper, 0.0, kk * jnp.exp2(gref - gcum)).astype(BF16)
        a = a + jnp.where(lvl_mask[li], _dot_nt(qa, kb), 0.0)
    a_ref[ds, :] = a.astype(BF16)

    qd_ref[ds, :] = (qf * jnp.exp2(gcum)).astype(BF16)
    k_dec = (kk * jnp.exp2(g_last - gcum)).astype(BF16)
    kv_ref[slot] = _dot_tn(i_ref[sl, :], k_dec)
    dec_ref[pl.ds(slot, 1), :] = jnp.exp2(g_last)


def _hgrn(proj, lb, gain, *, batch, seq, t_rows):
    m = batch * seq
    t_rows = min(t_rows, seq)
    nt = seq // t_rows
    h = HG_HEADS
    grows = HG_GROUP * HG_CHUNK
    assert t_rows % grows == 0

    def spec(off):
        return pl.BlockSpec((t_rows, 128), lambda b, hh, t: (b * nt + t, off + hh))

    return pl.pallas_call(
        functools.partial(_hgrn_kernel, t_rows=t_rows),
        grid=(batch, h, nt),
        in_specs=[spec(0), spec(h), spec(2 * h), spec(3 * h),
                  pl.BlockSpec((1, 128), lambda b, hh, t: (0, hh)),
                  pl.BlockSpec((1, 128), lambda b, hh, t: (0, 0))],
        out_specs=pl.BlockSpec((t_rows, 128), lambda b, hh, t: (b * nt + t, hh)),
        out_shape=jax.ShapeDtypeStruct((m, h * HG_DV), BF16),
        scratch_shapes=[
            pltpu.VMEM((HG_DV, HG_DK), F32),
            pltpu.VMEM((2 * grows, 128), F32),
            pltpu.VMEM((2 * grows, 128), F32),
            pltpu.VMEM((2 * grows, HG_CHUNK), BF16),
            pltpu.VMEM((2 * grows, 128), BF16),
            pltpu.VMEM((2 * HG_GROUP, HG_DV, HG_DK), F32),
            pltpu.VMEM((2 * HG_GROUP, 128), F32),
        ],
        compiler_params=_cparams(("parallel", "parallel", "arbitrary")),
        name="hgrn2",
    )(proj, proj, proj, proj, lb, gain)


def _swa_kernel(sink_ref, q_ref, kp_ref, kc_ref, vp_ref, vc_ref, bias_ref, o_ref):
    blk = SWA_BLOCK
    dh = SWA_HEAD_DIM
    first = pl.program_id(1) == 0
    rt = lax.broadcasted_iota(jnp.int32, (blk, blk), 0)
    ct = lax.broadcasted_iota(jnp.int32, (blk, blk), 1)
    from_cur = ct <= rt
    no_prev = first & (ct > rt)
    scale = dh ** -0.5
    for sub in range(SWA_STEP_BLOCKS):
        rows = slice(sub * blk, (sub + 1) * blk)
        outs = []
        for h in range(SWA_KV_HEADS):
            ks = slice(h * dh, (h + 1) * dh)
            k_prev = kp_ref[:, ks] if sub == 0 else kc_ref[(sub - 1) * blk:sub * blk, ks]
            v_prev = vp_ref[:, ks] if sub == 0 else vc_ref[(sub - 1) * blk:sub * blk, ks]
            kcat = jnp.concatenate([k_prev, kc_ref[rows, ks]], axis=0)
            vcat = jnp.concatenate([v_prev, vc_ref[rows, ks]], axis=0)
            for g in range(SWA_GROUP):
                hq = h * SWA_GROUP + g
                q = q_ref[rows, hq * dh:(hq + 1) * dh]
                s2 = _dot_nt(q, kcat)
                s = jnp.where(from_cur, s2[:, blk:], s2[:, :blk]) * scale + bias_ref[hq]
                if sub == 0:
                    s = jnp.where(no_prev, NEG_BIG, s)
                sink = sink_ref[hq]
                mx = jnp.maximum(jnp.max(s, axis=-1, keepdims=True), sink)
                p = jnp.exp(s - mx)
                den = jnp.sum(p, axis=-1, keepdims=True) + jnp.exp(sink - mx)
                p2 = jnp.concatenate([jnp.where(from_cur, 0.0, p), jnp.where(from_cur, p, 0.0)], axis=-1)
                o = _dot(p2.astype(BF16), vcat) / den
                outs.append(o)
        o_ref[rows, :] = jnp.concatenate(outs, axis=-1).astype(o_ref.dtype)


def _swa(proj, sinks, bias, *, batch, seq, q_blk, k_blk, v_blk):
    m = batch * seq
    nb = seq // SWA_BLOCK
    blk = SWA_BLOCK
    kw = SWA_KV_HEADS * SWA_HEAD_DIM
    qw = SWA_HEADS * SWA_HEAD_DIM

    sb = SWA_STEP_BLOCKS
    ns = nb // sb
    assert nb % sb == 0

    def cur(cb):
        return lambda b, n, s: (b * ns + n, cb)

    def prev(cb):
        return lambda b, n, s: (b * nb + jnp.maximum(n * sb - 1, 0), cb)

    grid_spec = pltpu.PrefetchScalarGridSpec(
        num_scalar_prefetch=1,
        grid=(batch, ns),
        in_specs=[
            pl.BlockSpec((sb * blk, qw), cur(q_blk)),
            pl.BlockSpec((blk, kw), prev(k_blk)),
            pl.BlockSpec((sb * blk, kw), cur(k_blk)),
            pl.BlockSpec((blk, kw), prev(v_blk)),
            pl.BlockSpec((sb * blk, kw), cur(v_blk)),
            pl.BlockSpec((SWA_HEADS, blk, blk), lambda b, n, s: (0, 0, 0)),
        ],
        out_specs=pl.BlockSpec((sb * blk, qw), lambda b, n, s: (b * ns + n, 0)),
    )
    return pl.pallas_call(
        _swa_kernel,
        grid_spec=grid_spec,
        out_shape=jax.ShapeDtypeStruct((m, qw), BF16),
        compiler_params=_cparams(("parallel", "arbitrary")),
        name="swa",
    )(sinks, proj, proj, proj, proj, proj, bias)


def _t5_bias_table(t5_table):
    t_loc = np.arange(SWA_BLOCK, dtype=np.int32)[:, None]
    s_loc = np.arange(2 * SWA_BLOCK, dtype=np.int32)[None, :]
    dist = t_loc + SWA_BLOCK - s_loc
    n = np.maximum(dist, 0)
    nf = np.maximum(n, 1).astype(np.float32)
    large = T5_MAX_EXACT + (np.log(nf / np.float32(T5_MAX_EXACT)) / np.float32(math.log(T5_MAX_DIST / T5_MAX_EXACT))
                            * np.float32(T5_BUCKETS - T5_MAX_EXACT)).astype(np.int32)
    large = np.minimum(large, T5_BUCKETS - 1)
    bucket = np.where(n < T5_MAX_EXACT, n, large).astype(np.int32)
    in_band = (dist >= 0) & (dist < SWA_WINDOW)
    bias = t5_table.astype(F32)[jnp.asarray(bucket)]
    bias = jnp.where(jnp.asarray(in_band)[..., None], bias, NEG_BIG).transpose(2, 0, 1)
    from_cur = jnp.asarray(np.tril(np.ones((SWA_BLOCK, SWA_BLOCK), dtype=bool)))
    return jnp.where(from_cur[None], bias[:, :, SWA_BLOCK:], bias[:, :, :SWA_BLOCK])


def _merge_rows(x_ref, oa_ref, ob_ref, ga_ref, gb_ref, bg_ref, wa_ref, wb_ref, wm_ref):
    a = _dot(oa_ref[...], wa_ref[...])
    b = _dot(ob_ref[...], wb_ref[...])
    bg = bg_ref[...]
    za = ga_ref[...].astype(F32) + bg[0:1]
    zb = gb_ref[...].astype(F32) + bg[1:2]
    merged = a / (1.0 + jnp.exp(-za)) + b / (1.0 + jnp.exp(-zb))
    return x_ref[...] + _dot(merged.astype(BF16), wm_ref[...])


def _xa_kernel(x_ref, oa_ref, ob_ref, ga_ref, gb_ref, bg_ref, wa_ref, wb_ref, wm_ref,
               gxa_ref, wq_ref, km_ref, vm_ref, wo_ref, gffn_ref, wr_ref, br_ref,
               x2_ref, hn_ref, idx_ref, gate_ref, cnt_ref):
    x1 = _merge_rows(x_ref, oa_ref, ob_ref, ga_ref, gb_ref, bg_ref, wa_ref, wb_ref, wm_ref)
    hx = _rms(x1, gxa_ref[...]).astype(BF16)
    q = _dot(hx, wq_ref[...]).astype(BF16)
    scale = XA_HEAD_DIM ** -0.5
    outs = []
    for h in range(XA_HEADS):
        sl = slice(h * XA_HEAD_DIM, (h + 1) * XA_HEAD_DIM)
        s = _dot_nt(q[:, sl], km_ref[:, sl]) * scale
        mx = jnp.max(s, axis=-1, keepdims=True)
        p = jnp.exp(s - mx)
        den = jnp.sum(p, axis=-1, keepdims=True)
        outs.append((_dot(p.astype(BF16), vm_ref[:, sl]) / den).astype(BF16))
    o = jnp.concatenate(outs, axis=-1)
    x2 = x1 + _dot(o, wo_ref[...])
    x2_ref[...] = x2
    hn = _rms(x2, gffn_ref[...])
    hn_ref[...] = _pack_rows(hn)
    hn_hi = hn.astype(BF16)
    hn_lo = (hn - hn_hi.astype(F32)).astype(BF16)
    big = _dot_nt(wr_ref[...], hn_hi)
    l = big[:N_EXPERTS] + big[N_EXPERTS:] + _dot_nt(wr_ref[:N_EXPERTS, :], hn_lo) + br_ref[...]
    eid = lax.broadcasted_iota(jnp.int32, l.shape, 0)
    vals = []
    idxs = []
    for k in range(TOP_K):
        mk = jnp.max(l, axis=0, keepdims=True)
        ik = jnp.min(jnp.where(l == mk, eid, N_EXPERTS), axis=0, keepdims=True)
        vals.append(mk)
        idxs.append(ik)
        l = jnp.where(eid == ik, -jnp.inf, l)
    ev = [jnp.exp(v - vals[0]) for v in vals]
    inv_den = 1.0 / (ev[0] + ev[1] + ev[2] + ev[3])
    gate_ref[...] = jnp.concatenate([e_k * inv_den for e_k in ev], axis=0)
    idx_ref[...] = jnp.concatenate(idxs, axis=0)
    hits = sum(jnp.where(eid == ik, 1.0, 0.0) for ik in idxs)
    cnt_ref[0] = jnp.sum(hits, axis=1, keepdims=True).astype(jnp.int32)


def _xa(x, o_a, o_b, proj, b_gates, wa, wb, wm, gxa, wq, kvmem, wo, gffn, wr, br, *,
        batch, seq, mem_len, tm, ga_blk, gb_blk):
    m, d = x.shape
    tm = min(tm, seq)
    nt = seq // tm
    row = lambda i: (i, 0)
    full = lambda i: (0, 0)
    return pl.pallas_call(
        _xa_kernel,
        grid=(m // tm,),
        in_specs=[
            pl.BlockSpec((tm, d), row),
            pl.BlockSpec((tm, d), row),
            pl.BlockSpec((tm, d), row),
            pl.BlockSpec((tm, d), lambda i: (i, ga_blk)),
            pl.BlockSpec((tm, d), lambda i: (i, gb_blk)),
            pl.BlockSpec((2, d), full),
            pl.BlockSpec((d, d), full),
            pl.BlockSpec((d, d), full),
            pl.BlockSpec((d, d), full),
            pl.BlockSpec((1, d), full),
            pl.BlockSpec((d, d), full),
            pl.BlockSpec((mem_len, d), lambda i: (i // nt, 0)),
            pl.BlockSpec((mem_len, d), lambda i: (i // nt, 1)),
            pl.BlockSpec((d, d), full),
            pl.BlockSpec((1, d), full),
            pl.BlockSpec((2 * N_EXPERTS, d), full),
            pl.BlockSpec((N_EXPERTS, 1), full),
        ],
        out_specs=[
            pl.BlockSpec((tm, d), row),
            pl.BlockSpec((tm, HALF), row),
            pl.BlockSpec((TOP_K, tm), lambda i: (0, i)),
            pl.BlockSpec((TOP_K, tm), lambda i: (0, i)),
            pl.BlockSpec((1, N_EXPERTS, 1), lambda i: (i, 0, 0)),
        ],
        out_shape=[
            jax.ShapeDtypeStruct((m, d), F32),
            jax.ShapeDtypeStruct((m, HALF), jnp.uint32),
            jax.ShapeDtypeStruct((TOP_K, m), jnp.int32),
            jax.ShapeDtypeStruct((TOP_K, m), F32),
            jax.ShapeDtypeStruct((m // tm, N_EXPERTS, 1), jnp.int32),
        ],
        compiler_params=_cparams(("parallel",)),
        name="merge_xattn_router",
    )(x, o_a, o_b, proj, proj, b_gates, wa, wb, wm, gxa, wq, kvmem, kvmem, wo, gffn, wr, br)


def _moe_kernel(bexp_ref, x_ref, wgu_ref, bgu_ref, wd_ref, bd_ref, y_ref, wgu_bf, wd_bf):
    i = pl.program_id(0)

    @pl.when((i == 0) | (bexp_ref[i] != bexp_ref[jnp.maximum(i - 1, 0)]))
    def _():
        wgu_bf[...] = wgu_ref[0].astype(BF16)
        wd_bf[...] = wd_ref[0].astype(BF16)

    n_used = bexp_ref[pl.num_programs(0)]

    @pl.when(i < n_used)
    def _():
        xb = _unpack_rows(x_ref[...]).astype(BF16)
        gu = _dot(xb, wgu_bf[...]) + bgu_ref[0]
        x_glu = jnp.minimum(gu[:, :D_FF], SWIGLU_LIMIT)
        x_lin = jnp.clip(gu[:, D_FF:], -SWIGLU_LIMIT, SWIGLU_LIMIT)
        act = x_glu / (1.0 + jnp.exp(-SWIGLU_ALPHA * x_glu)) * (x_lin + 1.0)
        y_ref[...] = _pack_rows(_dot(act.astype(BF16), wd_bf[...]) + bd_ref[0])

    @pl.when(i >= n_used)
    def _():
        y_ref[...] = jnp.zeros_like(y_ref)


def _moe(block_expert, x_sorted, wgu, bgu, wd, bd):
    rows = x_sorted.shape[0]
    d = D_MODEL
    n_blocks = rows // MOE_BLOCK
    grid_spec = pltpu.PrefetchScalarGridSpec(
        num_scalar_prefetch=1,
        grid=(n_blocks,),
        in_specs=[
            pl.BlockSpec((MOE_BLOCK, HALF), lambda i, be: (i, 0)),
            pl.BlockSpec((1, d, 2 * D_FF), lambda i, be: (be[i], 0, 0)),
            pl.BlockSpec((1, 1, 2 * D_FF), lambda i, be: (be[i], 0, 0)),
            pl.BlockSpec((1, D_FF, d), lambda i, be: (be[i], 0, 0)),
            pl.BlockSpec((1, 1, d), lambda i, be: (be[i], 0, 0)),
        ],
        out_specs=pl.BlockSpec((MOE_BLOCK, HALF), lambda i, be: (i, 0)),
        scratch_shapes=[
            pltpu.VMEM((d, 2 * D_FF), BF16),
            pltpu.VMEM((D_FF, d), BF16),
        ],
    )
    return pl.pallas_call(
        _moe_kernel,
        grid_spec=grid_spec,
        out_shape=jax.ShapeDtypeStruct((rows, HALF), jnp.uint32),
        compiler_params=_cparams(("arbitrary",)),
        name="moe_experts",
    )(block_expert, x_sorted, wgu, bgu, wd, bd)


def _sc_gather(table, idx):
    b = idx.shape[0]
    d = table.shape[1]
    per_worker = b // SC_WORKERS
    n_chunks = per_worker // SC_CHUNK
    assert b % SC_WORKERS == 0 and per_worker % (2 * SC_CHUNK) == 0, (b, SC_WORKERS, SC_CHUNK)
    mesh = plsc.VectorSubcoreMesh(core_axis_name="c", subcore_axis_name="s")

    @functools.partial(
        pl.kernel, mesh=mesh,
        out_type=jax.ShapeDtypeStruct((b, d), table.dtype),
        scratch_types=[pltpu.VMEM((per_worker,), jnp.int32),
                       pltpu.VMEM((2, SC_CHUNK, d), table.dtype),
                       pltpu.SemaphoreType.DMA((2,)),
                       pltpu.SemaphoreType.DMA((2,))],
        name="sc_row_gather",
    )
    def gather_kernel(table_hbm, idx_hbm, out_hbm, idx_v, rows_v, g_sem, w_sem):
        worker = lax.axis_index("s") * SC_CORES + lax.axis_index("c")
        base = worker * per_worker
        pltpu.sync_copy(idx_hbm.at[pl.ds(base, per_worker)], idx_v)

        def gather(t, slot):
            return pltpu.make_async_copy(table_hbm.at[idx_v.at[pl.ds(t * SC_CHUNK, SC_CHUNK)]],
                                         rows_v.at[slot], g_sem.at[slot])

        def write(t, slot):
            return pltpu.make_async_copy(rows_v.at[slot], out_hbm.at[pl.ds(base + t * SC_CHUNK, SC_CHUNK)],
                                         w_sem.at[slot])

        gather(0, 0).start()

        def body(t2, carry):
            for slot in range(2):
                t = t2 * 2 + slot
                gather(t, slot).wait()

                @pl.when(t + 1 < n_chunks)
                def _():
                    @pl.when(t >= 1)
                    def _():
                        write(t - 1, 1 - slot).wait()
                    gather(t + 1, 1 - slot).start()
                write(t, slot).start()
            return carry
        lax.fori_loop(0, n_chunks // 2, body, 0)
        write(n_chunks - 2, 0).wait()
        write(n_chunks - 1, 1).wait()

    return gather_kernel(table, idx)


def _sc_scatter(rows, idx2d, n_out):
    b, d = rows.shape
    per_worker = b // SC_WORKERS
    n_chunks = per_worker // SC_CHUNK
    assert b % SC_WORKERS == 0 and per_worker % (2 * SC_CHUNK) == 0, (b, SC_WORKERS, SC_CHUNK)
    mesh = plsc.VectorSubcoreMesh(core_axis_name="c", subcore_axis_name="s")

    @functools.partial(
        pl.kernel, mesh=mesh,
        out_type=jax.ShapeDtypeStruct((n_out, d), rows.dtype),
        scratch_types=[pltpu.VMEM((n_chunks, SC_CHUNK), jnp.int32),
                       pltpu.VMEM((2, SC_CHUNK, d), rows.dtype),
                       pltpu.SemaphoreType.DMA((2,)),
                       pltpu.SemaphoreType.DMA((2,))],
        name="sc_row_scatter",
    )
    def scatter_kernel(rows_hbm, idx_hbm, out_hbm, idx_v, rows_v, r_sem, w_sem):
        worker = lax.axis_index("s") * SC_CORES + lax.axis_index("c")
        base = worker * per_worker
        pltpu.sync_copy(idx_hbm.at[pl.ds(worker * n_chunks, n_chunks)], idx_v)

        def read(t, slot):
            return pltpu.make_async_copy(rows_hbm.at[pl.ds(base + t * SC_CHUNK, SC_CHUNK)],
                                         rows_v.at[slot], r_sem.at[slot])

        def write(t, slot):
            return pltpu.make_async_copy(rows_v.at[slot], out_hbm.at[idx_v.at[t]], w_sem.at[slot])

        read(0, 0).start()

        def body(t2, carry):
            for slot in range(2):
                t = t2 * 2 + slot
                read(t, slot).wait()

                @pl.when(t + 1 < n_chunks)
                def _():
                    @pl.when(t >= 1)
                    def _():
                        write(t - 1, 1 - slot).wait()
                    read(t + 1, 1 - slot).start()
                write(t, slot).start()
            return carry
        lax.fori_loop(0, n_chunks // 2, body, 0)
        write(n_chunks - 2, 0).wait()
        write(n_chunks - 1, 1).wait()

    return scatter_kernel(rows, idx2d)


def _combine_kernel(x_ref, y0_ref, y1_ref, y2_ref, y3_ref, gate_ref, gain_ref, *rest):
    o_ref = rest[-1]
    g = gate_ref[...]
    acc = x_ref[...]
    for k, y_ref in enumerate((y0_ref, y1_ref, y2_ref, y3_ref)):
        acc = acc + g[:, k:k + 1] * _unpack_rows(y_ref[...])
    o_ref[...] = _rms(acc, gain_ref[...])


def _combine(x2, y, gates, gain, prev, *, tm, chunk, n_chunks):
    m, d = x2.shape
    mc = m // n_chunks
    tm = min(tm, mc)
    kb = mc // tm
    off = chunk * kb
    row = lambda i: (off + i, 0)
    in_specs = ([pl.BlockSpec((tm, d), row)]
                + [pl.BlockSpec((tm, HALF), functools.partial(lambda i, k: (k * kb + i, 0), k=k))
                   for k in range(TOP_K)]
                + [pl.BlockSpec((tm, TOP_K), row), pl.BlockSpec((1, d), lambda i: (0, 0))])
    args = [x2, y, y, y, y, gates, gain]
    aliases = {}
    if prev is not None:
        in_specs.append(pl.BlockSpec(memory_space=pl.ANY))
        args.append(prev)
        aliases = {len(args) - 1: 0}
    return pl.pallas_call(
        _combine_kernel,
        grid=(kb,),
        in_specs=in_specs,
        out_specs=pl.BlockSpec((tm, d), row),
        out_shape=jax.ShapeDtypeStruct((m, d), F32),
        input_output_aliases=aliases,
        compiler_params=_cparams(("parallel",)),
        name="combine_norm",
    )(*args)


def _routing_tables(top_idx, counts, n_tok):
    n_assign = n_tok * TOP_K
    e_flat = top_idx.reshape(-1)
    assert N_EXPERTS * n_assign < 2 ** 31
    order = jnp.sort(e_flat * n_assign + jnp.arange(n_assign, dtype=jnp.int32)) % n_assign
    starts = jnp.cumsum(counts) - counts
    padded = (counts + MOE_BLOCK - 1) // MOE_BLOCK * MOE_BLOCK
    pad_ends = jnp.cumsum(padded)
    pad_starts = pad_ends - padded
    cap = -(-n_assign // MOE_BLOCK) * MOE_BLOCK + N_EXPERTS * MOE_BLOCK
    n_blocks = cap // MOE_BLOCK
    block_start = jnp.arange(n_blocks, dtype=jnp.int32) * MOE_BLOCK
    block_expert = jnp.minimum(
        jnp.sum((pad_ends[None, :] <= block_start[:, None]).astype(jnp.int32), axis=1), N_EXPERTS - 1)
    r = block_start[:, None] + jnp.arange(MOE_BLOCK, dtype=jnp.int32)[None, :]
    pos = r - pad_starts[block_expert][:, None]
    valid = pos < counts[block_expert][:, None]
    src = jnp.clip(starts[block_expert][:, None] + pos, 0, n_assign - 1)
    tok = jnp.where(valid, order[src] % n_tok, r % n_tok).reshape(-1)
    row_of = jnp.where(valid, order[src], n_assign + r % SC_SPARE_ROWS).reshape(-1, SC_CHUNK)
    n_used = pad_ends[-1:] // MOE_BLOCK
    return tok, row_of, jnp.concatenate([block_expert, n_used]).astype(jnp.int32)


def kernel(x, mem, t5_table, lb_logits, norm_mix, w_in, b_gates, hg_norm_gain, swa_sinks, w_branch_a, w_branch_b, w_mix_out, norm_xa, xa_w_q, xa_w_k, xa_w_v, xa_w_o, norm_ffn, w_router, b_router, w_gate_up, b_gate_up, w_down, b_down, norm_final):
    batch, seq, d = x.shape
    mem_len = mem.shape[1]
    m = batch * seq
    depth = norm_mix.shape[0]
    assert depth == 1, "the final norm is fused into the layer's last kernel"
    lower_bounds = jnp.cumsum(jax.nn.softmax(lb_logits.astype(F32), axis=0), axis=0)
    bias = _t5_bias_table(t5_table)
    xt = x.reshape(m, d)
    for l in range(depth):
        w = w_in[l]
        cuts = np.cumsum([0, 1024, 1024, 1024, 1024, 1024, 256, 256, 1024, 1024])
        seg = [w[:, cuts[k]:cuts[k + 1]] for k in range(9)]
        w_perm = jnp.concatenate(seg[0:5] + seg[7:9] + seg[5:7], axis=1).astype(BF16)
        proj = _norm_matmul(xt, norm_mix[l][None], w_perm, use_norm=True, tm=2048, tn=1536, out_dtype=BF16)
        o_a = _hgrn(proj, lower_bounds[l][None], hg_norm_gain[l][None], batch=batch, seq=seq, t_rows=4096)
        o_b = _swa(proj, swa_sinks[l], bias, batch=batch, seq=seq, q_blk=4, k_blk=28, v_blk=29)
        w_kv = jnp.concatenate([xa_w_k[l], xa_w_v[l]], axis=1).astype(BF16)
        kvmem = _norm_matmul(mem.reshape(batch * mem_len, d), norm_xa[l][None], w_kv, use_norm=False,
                             tm=1024, tn=1024, out_dtype=BF16)
        wr_hi = w_router[l].astype(BF16)
        wr_lo = (w_router[l] - wr_hi.astype(F32)).astype(BF16)
        x2, hn, top_idx, gates, tile_counts = _xa(xt, o_a, o_b, proj, b_gates[l], w_branch_a[l].astype(BF16),
                                     w_branch_b[l].astype(BF16), w_mix_out[l].astype(BF16),
                                     norm_xa[l][None], xa_w_q[l].astype(BF16), kvmem, xa_w_o[l].astype(BF16),
                                     norm_ffn[l][None], jnp.concatenate([wr_hi, wr_lo], axis=1).T,
                                     b_router[l][:, None], batch=batch, seq=seq, mem_len=mem_len, tm=512,
                                     ga_blk=5, gb_blk=6)
        gates = gates.T
        tok, row_of, block_expert = _routing_tables(top_idx, jnp.sum(tile_counts[:, :, 0], axis=0), m)
        x_sorted = _sc_gather(hn, tok)
        y_sorted = _moe(block_expert, x_sorted, w_gate_up[l], b_gate_up[l][:, None, :],
                        w_down[l], b_down[l][:, None, :])
        y = _sc_scatter(y_sorted, row_of, TOP_K * m + SC_SPARE_ROWS)
        xt = _combine(x2, y, gates, norm_final[None], None, tm=256, chunk=0, n_chunks=1)
    return xt.reshape(batch, seq, d)
```
